```python
import jax, jax.numpy as jnp
from jax import lax
import numpy as np

D_MODEL = 1024
BATCH = 2
SEQ = 16384
DEPTH = 2

GRID_W = 64
CTX_LEN = 256
HEAD_DIM = 128
A_HEADS = 4
A_KV_HEADS = 2
B_HEADS = 4
B_KV_HEADS = 2
Q_WIDTH = (A_HEADS + B_HEADS) * HEAD_DIM
KV_WIDTH = 2 * (A_KV_HEADS + B_KV_HEADS) * HEAD_DIM
BLOCK = 128
WINDOW = 128
ROPE_THETA = 10000.0
FNET_GROUPS = 4
D_FF = 2816
CONV_W = 3
N_MOD = 6
N_ATTN_LAYERS = (DEPTH + 1) // 2
N_FOURIER_LAYERS = DEPTH // 2
EPS = 1e-6

kernel_name = "hybrid_dit_attn_fnet_convffn"

F32 = jnp.float32


def rms_norm(x, g):
    xf = x.astype(F32)
    y = xf * lax.rsqrt(jnp.mean(xf * xf, axis=-1, keepdims=True) + EPS)
    return (y * g.astype(F32)).astype(x.dtype)


def modulate(h, shift, scale):
    return h * (1 + scale) + shift


def adaln_params(cond, w, bias):
    m = jax.nn.silu(cond) @ w + bias
    return jnp.split(m[..., None, :], N_MOD, axis=-1)


def axial_rope_tables(row, col):
    n_freq = HEAD_DIM // 4
    inv = ROPE_THETA ** (-jnp.arange(n_freq, dtype=F32) / n_freq)
    ang = jnp.stack([row.astype(F32)[:, None] * inv, col.astype(F32)[:, None] * inv], axis=1)
    return jnp.cos(ang), jnp.sin(ang)


def apply_rope(x, cos, sin):
    b, l, h, dh = x.shape
    xs = x.reshape(b, l, h, 2, 2, dh // 4)
    x1, x2 = xs[..., 0, :], xs[..., 1, :]
    cs = cos[None, :, None].astype(x.dtype)
    sn = sin[None, :, None].astype(x.dtype)
    out = jnp.stack([x1 * cs - x2 * sn, x2 * cs + x1 * sn], axis=-2)
    return out.reshape(b, l, h, dh)


def group_heads(q, n_kv):
    b, l, h, dh = q.shape
    return q.reshape(b, l, n_kv, h // n_kv, dh)


def split_kv(kv):
    b, l, _ = kv.shape
    a = A_KV_HEADS * HEAD_DIM
    bb = B_KV_HEADS * HEAD_DIM
    k_a, v_a, k_b, v_b = jnp.split(kv, [a, 2 * a, 2 * a + bb], axis=-1)
    return (k_a.reshape(b, l, A_KV_HEADS, HEAD_DIM), v_a.reshape(b, l, A_KV_HEADS, HEAD_DIM),
            k_b.reshape(b, l, B_KV_HEADS, HEAD_DIM), v_b.reshape(b, l, B_KV_HEADS, HEAD_DIM))


def attend(q, k, v, sink=None):
    s = jnp.einsum('bqkgd,bskd->bkgqs', q, k, preferred_element_type=F32) * (HEAD_DIM ** -0.5)
    if sink is not None:
        s_sink = jnp.broadcast_to(sink.astype(F32)[None, :, :, None, None], s.shape[:-1] + (1,))
        p = jax.nn.softmax(jnp.concatenate([s_sink, s], axis=-1), axis=-1)[..., 1:]
    else:
        p = jax.nn.softmax(s, axis=-1)
    return jnp.einsum('bkgqs,bskd->bqkgd', p.astype(v.dtype), v)


def global_attention(q, k, v, k_ctx, v_ctx):
    b, l, hkv, g, dh = q.shape
    k_all = jnp.concatenate([k_ctx, k], axis=1)
    v_all = jnp.concatenate([v_ctx, v], axis=1)
    qb = jnp.moveaxis(q.reshape(b, l // BLOCK, BLOCK, hkv, g, dh), 1, 0)
    o = lax.map(lambda qblk: attend(qblk, k_all, v_all), qb)
    return jnp.moveaxis(o, 0, 1).reshape(b, l, hkv * g * dh)


def window_attention(q, k, v, k_ctx, v_ctx, sink):
    b, l, hkv, g, dh = q.shape
    nb = l // BLOCK
    zpad = jnp.zeros((b, BLOCK, hkv, dh), k.dtype)

    def band(t):
        tp = jnp.concatenate([zpad, t, zpad], axis=1).reshape(b, nb + 2, BLOCK, hkv, dh)
        return jnp.concatenate([tp[:, :-2], tp[:, 1:-1], tp[:, 2:]], axis=2)

    kb, vb = band(k), band(v)
    qb = q.reshape(b, nb, BLOCK, hkv, g, dh)
    scale = HEAD_DIM ** -0.5
    s_loc = jnp.einsum('bnqkgd,bnskd->bnkgqs', qb, kb, preferred_element_type=F32) * scale
    s_ctx = jnp.einsum('bnqkgd,bskd->bnkgqs', qb, k_ctx, preferred_element_type=F32) * scale
    qi = jnp.arange(BLOCK)[:, None]
    si = jnp.arange(3 * BLOCK)[None, :]
    in_window = jnp.abs(si - BLOCK - qi) <= WINDOW
    kpos = (jnp.arange(nb)[:, None] - 1) * BLOCK + jnp.arange(3 * BLOCK)[None, :]
    in_range = (kpos >= 0) & (kpos < l)
    valid = in_window[None] & in_range[:, None, :]
    s_loc = jnp.where(valid[None, :, None, None], s_loc, -jnp.inf)
    s_sink = jnp.broadcast_to(sink.astype(F32)[None, None, :, :, None, None], s_ctx.shape[:-1] + (1,))
    p = jax.nn.softmax(jnp.concatenate([s_sink, s_ctx, s_loc], axis=-1), axis=-1)
    c_len = k_ctx.shape[1]
    p_ctx = p[..., 1:1 + c_len].astype(v.dtype)
    p_loc = p[..., 1 + c_len:].astype(v.dtype)
    o = (jnp.einsum('bnkgqs,bskd->bnqkgd', p_ctx, v_ctx)
         + jnp.einsum('bnkgqs,bnskd->bnqkgd', p_loc, vb))
    return o.reshape(b, l, hkv * g * dh)


def attention_mixers(h, hc, w_in, w_out, q_g, k_g, sink, cos, sin, with_ctx_queries):
    b, l, _ = h.shape
    c_len = hc.shape[1]
    proj = h @ w_in
    q_all, kv = proj[..., :Q_WIDTH], proj[..., Q_WIDTH:]
    q_a = q_all[..., :A_HEADS * HEAD_DIM].reshape(b, l, A_HEADS, HEAD_DIM)
    q_b = q_all[..., A_HEADS * HEAD_DIM:].reshape(b, l, B_HEADS, HEAD_DIM)
    k_a, v_a, k_b, v_b = split_kv(kv)
    q_a = apply_rope(rms_norm(q_a, q_g), cos, sin)
    k_a = apply_rope(rms_norm(k_a, k_g), cos, sin)
    q_b = apply_rope(q_b, cos, sin)
    k_b = apply_rope(k_b, cos, sin)
    k_ac, v_ac, k_bc, v_bc = split_kv(hc @ w_in[:, Q_WIDTH:])
    k_ac = rms_norm(k_ac, k_g)
    sink_hg = sink.reshape(B_KV_HEADS, B_HEADS // B_KV_HEADS)
    o_a = global_attention(group_heads(q_a, A_KV_HEADS), k_a, v_a, k_ac, v_ac)
    o_b = window_attention(group_heads(q_b, B_KV_HEADS), k_b, v_b, k_bc, v_bc, sink_hg)
    y = jnp.concatenate([o_a, o_b], axis=-1) @ w_out
    if not with_ctx_queries:
        return y, None
    qc = hc @ w_in[:, :Q_WIDTH]
    qc_a = rms_norm(qc[..., :A_HEADS * HEAD_DIM].reshape(b, c_len, A_HEADS, HEAD_DIM), q_g)
    qc_b = qc[..., A_HEADS * HEAD_DIM:].reshape(b, c_len, B_HEADS, HEAD_DIM)
    oc_a = attend(group_heads(qc_a, A_KV_HEADS), k_ac, v_ac).reshape(b, c_len, -1)
    oc_b = attend(group_heads(qc_b, B_KV_HEADS), k_bc, v_bc, sink_hg).reshape(b, c_len, -1)
    yc = jnp.concatenate([oc_a, oc_b], axis=-1) @ w_out
    return y, yc


def fourier_mix(h):
    b, l, d = h.shape
    hg = h.reshape(b, l, FNET_GROUPS, d // FNET_GROUPS).astype(F32)
    f = jnp.fft.fft2(hg, axes=(1, 3), norm="ortho").real
    return f.reshape(b, l, d).astype(h.dtype)


def conv_ffn(h, w_up, conv_w, conv_b, w_down):
    u = h @ w_up
    up = jnp.pad(u, ((0, 0), (1, 1), (0, 0)))
    u = up[:, :-2] * conv_w[0] + up[:, 1:-1] * conv_w[1] + up[:, 2:] * conv_w[2] + conv_b
    gate, val = jnp.split(u, 2, axis=-1)
    return (jax.nn.silu(gate) * val) @ w_down


def setup_inputs(seed: int = 0) -> dict:
    key = jax.random.key(seed)
    ks = jax.random.split(key, 17)

    def nrm(k, shape, scale):
        return jax.random.normal(k, shape, F32) * scale

    return {
        "x": nrm(ks[0], (BATCH, SEQ, D_MODEL), 1.0),
        "c": nrm(ks[1], (BATCH, D_MODEL), 1.0),
        "ctx": nrm(ks[2], (BATCH, CTX_LEN, D_MODEL), 1.0),
        "c_ctx": nrm(ks[3], (D_MODEL,), 1.0),
        "mod_w": nrm(ks[4], (DEPTH, D_MODEL, N_MOD * D_MODEL), 0.5 * D_MODEL ** -0.5),
        "mod_b": nrm(ks[5], (DEPTH, N_MOD * D_MODEL), 0.02),
        "norm_g": 1.0 + nrm(ks[6], (DEPTH, 4, D_MODEL), 0.02),
        "attn_w_in": nrm(ks[7], (N_ATTN_LAYERS, D_MODEL, Q_WIDTH + KV_WIDTH), D_MODEL ** -0.5),
        "attn_w_out": nrm(ks[8], (N_ATTN_LAYERS, Q_WIDTH, D_MODEL), Q_WIDTH ** -0.5),
        "q_norm_g": 1.0 + nrm(ks[9], (N_ATTN_LAYERS, HEAD_DIM), 0.02),
        "k_norm_g": 1.0 + nrm(ks[10], (N_ATTN_LAYERS, HEAD_DIM), 0.02),
        "sink": nrm(ks[11], (N_ATTN_LAYERS, B_HEADS), 0.5),
        "fourier_w_out": nrm(ks[12], (N_FOURIER_LAYERS, D_MODEL, D_MODEL), D_MODEL ** -0.5),
        "ffn_w_up": nrm(ks[13], (DEPTH, D_MODEL, 2 * D_FF), D_MODEL ** -0.5),
        "ffn_conv_w": nrm(ks[14], (DEPTH, CONV_W, 2 * D_FF), CONV_W ** -0.5),
        "ffn_conv_b": nrm(ks[15], (DEPTH, 2 * D_FF), 0.02),
        "ffn_w_down": nrm(ks[16], (DEPTH, D_FF, D_MODEL), D_FF ** -0.5),
    }


def reference(x, c, ctx, c_ctx, mod_w, mod_b, norm_g, attn_w_in, attn_w_out, q_norm_g, k_norm_g, sink,
              fourier_w_out, ffn_w_up, ffn_conv_w, ffn_conv_b, ffn_w_down):
    b, l, d = x.shape
    rows = l // GRID_W
    row = jnp.repeat(jnp.arange(rows), GRID_W)
    col = jnp.tile(jnp.arange(GRID_W), rows)
    cos, sin = axial_rope_tables(row, col)

    for i in range(DEPTH):
        ctx_next = any(j % 2 == 0 for j in range(i + 1, DEPTH))
        need_hc = (i % 2 == 0) or ctx_next
        sh1, sc1, g1, sh2, sc2, g2 = adaln_params(c, mod_w[i], mod_b[i])
        h = modulate(rms_norm(x, norm_g[i, 0]), sh1, sc1)
        if need_hc:
            csh1, csc1, cg1, csh2, csc2, cg2 = adaln_params(c_ctx, mod_w[i], mod_b[i])
            hc = modulate(rms_norm(ctx, norm_g[i, 0]), csh1, csc1)
        if i % 2 == 0:
            a = i // 2
            y, yc = attention_mixers(h, hc, attn_w_in[a], attn_w_out[a], q_norm_g[a], k_norm_g[a], sink[a],
                                     cos, sin, ctx_next)
        else:
            w_f = fourier_w_out[i // 2]
            y = fourier_mix(h) @ w_f
            yc = fourier_mix(hc) @ w_f if ctx_next else None
        x = x + g1 * rms_norm(y, norm_g[i, 1])
        h = modulate(rms_norm(x, norm_g[i, 2]), sh2, sc2)
        x = x + g2 * rms_norm(conv_ffn(h, ffn_w_up[i], ffn_conv_w[i], ffn_conv_b[i], ffn_w_down[i]), norm_g[i, 3])
        if ctx_next:
            ctx = ctx + cg1 * rms_norm(yc, norm_g[i, 1])
            hc = modulate(rms_norm(ctx, norm_g[i, 2]), csh2, csc2)
            ctx = ctx + cg2 * rms_norm(conv_ffn(hc, ffn_w_up[i], ffn_conv_w[i], ffn_conv_b[i], ffn_w_down[i]),
                                       norm_g[i, 3])
    return x
```

```python
import functools
import math

import numpy as np
import jax
import jax.numpy as jnp
from jax import lax
from jax.experimental import pallas as pl
from jax.experimental.pallas import tpu as pltpu

F32 = jnp.float32
BF16 = jnp.bfloat16

HEAD_DIM = 128
GRID_W = 64
A_HEADS = 4
A_KV_HEADS = 2
B_HEADS = 4
B_KV_HEADS = 2
GROUP = A_HEADS // A_KV_HEADS
BLOCK = 128
WINDOW = 128
ROPE_THETA = 10000.0
FNET_GROUPS = 4
N_MOD = 6
EPS = 1e-6
LOG2E = math.log2(math.e)
Q_SCALE = HEAD_DIM ** -0.5 * LOG2E
NEG_BIG = -1e30

DFT_N1 = 128
COND_ROWS = 8
VMEM_LIMIT = 56 * 1024 * 1024

QKV_TM = 512
ATT_TQ = 256
ATT_TK = QKV_TM
WIN_TQ = 512
POST_TM = 512
FFN_TM = 512
FFN_FC = 256
HALO = 16
F1_T2 = 4
F2_K1 = 4


def _cparams(sem):
    return pltpu.CompilerParams(dimension_semantics=sem, vmem_limit_bytes=VMEM_LIMIT)


def _rms(x, g):
    return x * lax.rsqrt(jnp.mean(x * x, axis=-1, keepdims=True) + EPS) * g


def _mod_kernel(cond_ref, w_ref, b_ref, o_ref):
    c = cond_ref[...]
    s = c / (1.0 + jnp.exp(-c))
    o_ref[0] = jnp.dot(s, w_ref[0], preferred_element_type=F32) + b_ref[0]


def _mods(cond, mod_w, mod_b):
    depth, d, n = mod_w.shape
    tn = n // 4
    return pl.pallas_call(
        _mod_kernel,
        grid=(depth, n // tn),
        in_specs=[
            pl.BlockSpec((COND_ROWS, d), lambda l, j: (0, 0)),
            pl.BlockSpec((1, d, tn), lambda l, j: (l, 0, j)),
            pl.BlockSpec((1, 1, tn), lambda l, j: (l, 0, j)),
        ],
        out_specs=pl.BlockSpec((1, COND_ROWS, tn), lambda l, j: (l, 0, j)),
        out_shape=jax.ShapeDtypeStruct((depth, COND_ROWS, n), F32),
        compiler_params=_cparams(("arbitrary", "arbitrary")),
        name="adaln_mods",
    )(cond, mod_w, mod_b.reshape(depth, 1, n))


def _qkv_kernel(x_ref, mod_ref, ng_ref, w_ref, qg_ref, kg_ref, cos_ref, sin_ref,
                qat_ref, ka_ref, vat_ref, qb_ref, kb_ref, vb_ref):
    x = x_ref[0]
    tm = x.shape[0]
    h = _rms(x, ng_ref[0:1, :]) * (1.0 + mod_ref[0, 1:2, :]) + mod_ref[0, 0:1, :]
    proj = jnp.dot(h.astype(BF16), w_ref[...], preferred_element_type=F32)
    cos = cos_ref[...]
    sin = sin_ref[...]
    lane = lax.broadcasted_iota(jnp.int32, (tm, HEAD_DIM), 1)
    low_half = (lane % (HEAD_DIM // 2)) < (HEAD_DIM // 4)

    def rope(t):
        partner = jnp.where(low_half,
                            pltpu.roll(t, HEAD_DIM - HEAD_DIM // 4, 1),
                            pltpu.roll(t, HEAD_DIM // 4, 1))
        return t * cos + partner * sin

    def head(j):
        return proj[:, j * HEAD_DIM:(j + 1) * HEAD_DIM]

    qg = qg_ref[...]
    kg = kg_ref[...]
    col = 0
    for hh in range(A_HEADS):
        q = rope(_rms(head(col + hh), qg)) * Q_SCALE
        qat_ref[0, hh] = q.T.astype(BF16)
    col += A_HEADS
    for hh in range(B_HEADS):
        qb_ref[0, hh] = (rope(head(col + hh)) * Q_SCALE).astype(BF16)
    col += B_HEADS
    for hh in range(A_KV_HEADS):
        ka_ref[0, hh] = rope(_rms(head(col + hh), kg)).astype(BF16)
    col += A_KV_HEADS
    for hh in range(A_KV_HEADS):
        vat_ref[0, hh, 0] = head(col + hh).T.astype(BF16)
    col += A_KV_HEADS
    for hh in range(B_KV_HEADS):
        kb_ref[0, hh] = rope(head(col + hh)).astype(BF16)
    col += B_KV_HEADS
    for hh in range(B_KV_HEADS):
        vb_ref[0, hh] = head(col + hh).astype(BF16)


def _qkv(x, mod6, mod_row, ng, w_in, qg, kg, cos, sin, tm):
    b, l, d = x.shape
    n = w_in.shape[1]
    nt = l // tm
    return pl.pallas_call(
        _qkv_kernel,
        grid=(b, nt),
        in_specs=[
            pl.BlockSpec((1, tm, d), lambda bi, i: (bi, i, 0)),
            pl.BlockSpec((1, N_MOD, d), lambda bi, i: (mod_row(bi), 0, 0)),
            pl.BlockSpec((4, d), lambda bi, i: (0, 0)),
            pl.BlockSpec((d, n), lambda bi, i: (0, 0)),
            pl.BlockSpec((1, HEAD_DIM), lambda bi, i: (0, 0)),
            pl.BlockSpec((1, HEAD_DIM), lambda bi, i: (0, 0)),
            pl.BlockSpec((tm, HEAD_DIM), lambda bi, i: (i, 0)),
            pl.BlockSpec((tm, HEAD_DIM), lambda bi, i: (i, 0)),
        ],
        out_specs=[
            pl.BlockSpec((1, A_HEADS, HEAD_DIM, tm), lambda bi, i: (bi, 0, 0, i)),
            pl.BlockSpec((1, A_KV_HEADS, tm, HEAD_DIM), lambda bi, i: (bi, 0, i, 0)),
            pl.BlockSpec((1, A_KV_HEADS, 1, HEAD_DIM, tm), lambda bi, i: (bi, 0, i, 0, 0)),
            pl.BlockSpec((1, B_HEADS, tm, HEAD_DIM), lambda bi, i: (bi, 0, i, 0)),
            pl.BlockSpec((1, B_KV_HEADS, tm, HEAD_DIM), lambda bi, i: (bi, 0, i, 0)),
            pl.BlockSpec((1, B_KV_HEADS, tm, HEAD_DIM), lambda bi, i: (bi, 0, i, 0)),
        ],
        out_shape=[
            jax.ShapeDtypeStruct((b, A_HEADS, HEAD_DIM, l), BF16),
            jax.ShapeDtypeStruct((b, A_KV_HEADS, l, HEAD_DIM), BF16),
            jax.ShapeDtypeStruct((b, A_KV_HEADS, nt, HEAD_DIM, tm), BF16),
            jax.ShapeDtypeStruct((b, B_HEADS, l, HEAD_DIM), BF16),
            jax.ShapeDtypeStruct((b, B_KV_HEADS, l, HEAD_DIM), BF16),
            jax.ShapeDtypeStruct((b, B_KV_HEADS, l, HEAD_DIM), BF16),
        ],
        compiler_params=_cparams(("parallel", "parallel")),
        name="qkv_proj",
    )(x, mod6, ng, w_in, qg, kg, cos, sin)


def _attn_a_kernel(qt_ref, kc_ref, vct_ref, k_ref, vt_ref, o_ref, m_ref, l_ref, acc_ref):
    tq = qt_ref.shape[3]
    n_chunks = vt_ref.shape[2]
    tk = vt_ref.shape[4]
    qt = jnp.concatenate([qt_ref[0, g] for g in range(GROUP)], axis=1)

    m_ref[...] = jnp.full(m_ref.shape, NEG_BIG, F32)
    l_ref[...] = jnp.zeros(l_ref.shape, F32)
    acc_ref[...] = jnp.zeros(acc_ref.shape, F32)

    def step(kblk, vtblk):
        s = jnp.dot(kblk, qt, preferred_element_type=F32)
        m_old = m_ref[...]
        m_new = jnp.maximum(m_old, jnp.max(s, axis=0, keepdims=True))
        alpha = jnp.exp2(m_old - m_new)
        p = jnp.exp2(s - m_new)
        l_ref[...] = alpha * l_ref[...] + jnp.sum(p, axis=0, keepdims=True)
        acc_ref[...] = alpha * acc_ref[...] + jnp.dot(vtblk, p.astype(BF16),
                                                      preferred_element_type=F32)
        m_ref[...] = m_new

    step(kc_ref[0, 0], vct_ref[0, 0, 0])

    def body(j, carry):
        start = pl.multiple_of(j * tk, tk)
        step(k_ref[0, 0, pl.ds(start, tk), :], vt_ref[0, 0, j])
        return carry

    lax.fori_loop(0, n_chunks, body, 0)

    o = (acc_ref[...] / l_ref[...]).T
    for g in range(GROUP):
        o_ref[0, :, g * HEAD_DIM:(g + 1) * HEAD_DIM] = o[g * tq:(g + 1) * tq].astype(o_ref.dtype)


def _attn_a(qt, kc, vct, k, vt, tq):
    b, _, _, l = qt.shape
    c_len = kc.shape[2]
    n_chunks, tk = vt.shape[2], vt.shape[4]
    return pl.pallas_call(
        _attn_a_kernel,
        grid=(b, A_KV_HEADS, l // tq),
        in_specs=[
            pl.BlockSpec((1, GROUP, HEAD_DIM, tq), lambda bi, kv, i: (bi, kv, 0, i)),
            pl.BlockSpec((1, 1, c_len, HEAD_DIM), lambda bi, kv, i: (bi, kv, 0, 0)),
            pl.BlockSpec((1, 1, 1, HEAD_DIM, c_len), lambda bi, kv, i: (bi, kv, 0, 0, 0)),
            pl.BlockSpec((1, 1, l, HEAD_DIM), lambda bi, kv, i: (bi, kv, 0, 0)),
            pl.BlockSpec((1, 1, n_chunks, HEAD_DIM, tk), lambda bi, kv, i: (bi, kv, 0, 0, 0)),
        ],
        out_specs=pl.BlockSpec((1, tq, GROUP * HEAD_DIM), lambda bi, kv, i: (bi, i, kv)),
        out_shape=jax.ShapeDtypeStruct((b, l, A_HEADS * HEAD_DIM), BF16),
        scratch_shapes=[
            pltpu.VMEM((1, GROUP * tq), F32),
            pltpu.VMEM((1, GROUP * tq), F32),
            pltpu.VMEM((HEAD_DIM, GROUP * tq), F32),
        ],
        compiler_params=_cparams(("parallel", "parallel", "arbitrary")),
        name="attn_global",
    )(qt, kc, vct, k, vt)


def _attn_b_kernel(sink_ref, q_ref, kp_ref, km_ref, kn_ref, vp_ref, vm_ref, vn_ref,
                   kc_ref, vc_ref, o_ref, *, seq_len):
    kv = pl.program_id(1)
    i = pl.program_id(2)
    tq = q_ref.shape[2]
    sub = tq // BLOCK
    band = 3 * BLOCK
    kcat = jnp.concatenate([kp_ref[0, 0], km_ref[0, 0], kn_ref[0, 0]], axis=0)
    vcat = jnp.concatenate([vp_ref[0, 0], vm_ref[0, 0], vn_ref[0, 0]], axis=0)
    kc = kc_ref[0, 0]
    vc = vc_ref[0, 0]
    rows = GROUP * BLOCK
    nt = (((1,), (1,)), ((), ()))

    r = lax.broadcasted_iota(jnp.int32, (rows, band), 0) % BLOCK
    c = lax.broadcasted_iota(jnp.int32, (rows, band), 1)
    in_window = jnp.abs(c - BLOCK - r) <= WINDOW
    head_row = lax.broadcasted_iota(jnp.int32, (rows, 1), 0) // BLOCK
    sink2 = jnp.zeros((rows, 1), F32)
    for g in range(GROUP):
        sink2 = jnp.where(head_row == g, sink_ref[kv * GROUP + g] * LOG2E, sink2)

    for j in range(sub):
        q2 = jnp.concatenate([q_ref[0, g, j * BLOCK:(j + 1) * BLOCK, :] for g in range(GROUP)], axis=0)
        kband = kcat[j * BLOCK:j * BLOCK + band]
        vband = vcat[j * BLOCK:j * BLOCK + band]
        s_loc = lax.dot_general(q2, kband, nt, preferred_element_type=F32)
        s_ctx = lax.dot_general(q2, kc, nt, preferred_element_type=F32)
        kpos = (i * sub + (j - 1)) * BLOCK + c
        valid = in_window & (kpos >= 0) & (kpos < seq_len)
        s_loc = jnp.where(valid, s_loc, NEG_BIG)
        m = jnp.maximum(jnp.maximum(jnp.max(s_loc, axis=-1, keepdims=True),
                                    jnp.max(s_ctx, axis=-1, keepdims=True)), sink2)
        p_loc = jnp.exp2(s_loc - m)
        p_ctx = jnp.exp2(s_ctx - m)
        denom = (jnp.sum(p_loc, axis=-1, keepdims=True) + jnp.sum(p_ctx, axis=-1, keepdims=True)
                 + jnp.exp2(sink2 - m))
        o = (jnp.dot(p_ctx.astype(BF16), vc, preferred_element_type=F32)
             + jnp.dot(p_loc.astype(BF16), vband, preferred_element_type=F32)) / denom
        for g in range(GROUP):
            o_ref[0, j * BLOCK:(j + 1) * BLOCK, g * HEAD_DIM:(g + 1) * HEAD_DIM] = (
                o[g * BLOCK:(g + 1) * BLOCK].astype(o_ref.dtype))


def _attn_b(sink, q, k, v, kc, vc, tq):
    b, _, l, _ = q.shape
    c_len = kc.shape[2]
    sub = tq // BLOCK
    nb = l // BLOCK

    def main(bi, kv, i):
        return (bi, kv, i, 0)

    def prev(bi, kv, i):
        return (bi, kv, jnp.maximum(i * sub - 1, 0), 0)

    def nxt(bi, kv, i):
        return (bi, kv, jnp.minimum((i + 1) * sub, nb - 1), 0)

    def ctx_map(bi, kv, i):
        return (bi, kv, 0, 0)

    edge = lambda fn: pl.BlockSpec((1, 1, BLOCK, HEAD_DIM), fn)
    return pl.pallas_call(
        functools.partial(_attn_b_kernel, seq_len=l),
        grid=(b, B_KV_HEADS, l // tq),
        in_specs=[
            pl.BlockSpec(memory_space=pltpu.SMEM),
            pl.BlockSpec((1, GROUP, tq, HEAD_DIM), main),
            edge(prev), pl.BlockSpec((1, 1, tq, HEAD_DIM), main), edge(nxt),
            edge(prev), pl.BlockSpec((1, 1, tq, HEAD_DIM), main), edge(nxt),
            pl.BlockSpec((1, 1, c_len, HEAD_DIM), ctx_map),
            pl.BlockSpec((1, 1, c_len, HEAD_DIM), ctx_map),
        ],
        out_specs=pl.BlockSpec((1, tq, GROUP * HEAD_DIM), lambda bi, kv, i: (bi, i, kv)),
        out_shape=jax.ShapeDtypeStruct((b, l, B_HEADS * HEAD_DIM), BF16),
        compiler_params=_cparams(("parallel", "parallel", "arbitrary")),
        name="attn_window",
    )(sink, q, k, k, k, v, v, v, kc, vc)


def _residual_and_next(y, x, mod_ref, ng_ref):
    x1 = x + mod_ref[0, 2:3, :] * _rms(y, ng_ref[1:2, :])
    h2 = _rms(x1, ng_ref[2:3, :]) * (1.0 + mod_ref[0, 4:5, :]) + mod_ref[0, 3:4, :]
    return x1, h2


def _post_attn_kernel(oa_ref, ob_ref, w_ref, x_ref, mod_ref, ng_ref, x1_ref, h2_ref):
    na = oa_ref.shape[2]
    y = (jnp.dot(oa_ref[0], w_ref[0:na, :], preferred_element_type=F32)
         + jnp.dot(ob_ref[0], w_ref[na:, :], preferred_element_type=F32))
    x1, h2 = _residual_and_next(y, x_ref[0], mod_ref, ng_ref)
    x1_ref[0] = x1
    h2_ref[0] = h2.astype(BF16)


def _post_attn(oa, ob, w_out, x, mod6, ng, tm):
    b, l, d = x.shape
    na, nb_ = oa.shape[2], ob.shape[2]
    return pl.pallas_call(
        _post_attn_kernel,
        grid=(b, l // tm),
        in_specs=[
            pl.BlockSpec((1, tm, na), lambda bi, i: (bi, i, 0)),
            pl.BlockSpec((1, tm, nb_), lambda bi, i: (bi, i, 0)),
            pl.BlockSpec((na + nb_, d), lambda bi, i: (0, 0)),
            pl.BlockSpec((1, tm, d), lambda bi, i: (bi, i, 0)),
            pl.BlockSpec((1, N_MOD, d), lambda bi, i: (bi, 0, 0)),
            pl.BlockSpec((4, d), lambda bi, i: (0, 0)),
        ],
        out_specs=[
            pl.BlockSpec((1, tm, d), lambda bi, i: (bi, i, 0)),
            pl.BlockSpec((1, tm, d), lambda bi, i: (bi, i, 0)),
        ],
        out_shape=[jax.ShapeDtypeStruct((b, l, d), F32), jax.ShapeDtypeStruct((b, l, d), BF16)],
        compiler_params=_cparams(("parallel", "parallel")),
        name="attn_out_proj",
    )(oa, ob, w_out, x, mod6, ng)


def _ffn_kernel(hp_ref, hm_ref, hn_ref, x_ref, mod_ref, ng_ref, wg_ref, wv_ref, cwg_ref, cwv_ref,
                cbg_ref, cbv_ref, wd_ref, o_ref, hcat_ref, ug_ref, uv_ref, acc_ref):
    i = pl.program_id(1)
    last = pl.num_programs(1) - 1
    tm = hm_ref.shape[1]
    n_chunks = wg_ref.shape[0]
    hp = hp_ref[0]
    hn = hn_ref[0]
    hcat_ref[0:HALO, :] = jnp.where(i > 0, hp, jnp.zeros_like(hp))
    hcat_ref[HALO:HALO + tm, :] = hm_ref[0]
    hcat_ref[HALO + tm:, :] = jnp.where(i < last, hn, jnp.zeros_like(hn))
    acc_ref[...] = jnp.zeros(acc_ref.shape, F32)

    def conv(u_ref, cw, cb):
        return (u_ref[pl.ds(HALO - 1, tm), :] * cw[0:1, :] + u_ref[pl.ds(HALO, tm), :] * cw[1:2, :]
                + u_ref[pl.ds(HALO + 1, tm), :] * cw[2:3, :] + cb)

    def body(c, carry):
        hcat = hcat_ref[...]
        ug_ref[...] = jnp.dot(hcat, wg_ref[c], preferred_element_type=F32)
        uv_ref[...] = jnp.dot(hcat, wv_ref[c], preferred_element_type=F32)
        gate = conv(ug_ref, cwg_ref[c], cbg_ref[c])
        val = conv(uv_ref, cwv_ref[c], cbv_ref[c])
        act = gate / (1.0 + jnp.exp(-gate)) * val
        acc_ref[...] += jnp.dot(act.astype(BF16), wd_ref[c], preferred_element_type=F32)
        return carry

    lax.fori_loop(0, n_chunks, body, 0)
    o_ref[0] = x_ref[0] + mod_ref[0, 5:6, :] * _rms(acc_ref[...], ng_ref[3:4, :])


def _ffn(h2, x1, mod6, ng, w_up, conv_w, conv_b, w_down, tm, fc):
    b, l, d = x1.shape
    f = w_down.shape[0]
    nc = f // fc
    wg = w_up[:, :f].reshape(d, nc, fc).transpose(1, 0, 2).astype(BF16)
    wv = w_up[:, f:].reshape(d, nc, fc).transpose(1, 0, 2).astype(BF16)
    wd = w_down.reshape(nc, fc, d).astype(BF16)
    cwg = conv_w[:, :f].reshape(3, nc, fc).transpose(1, 0, 2)
    cwv = conv_w[:, f:].reshape(3, nc, fc).transpose(1, 0, 2)
    cbg = conv_b[:f].reshape(nc, 1, fc)
    cbv = conv_b[f:].reshape(nc, 1, fc)
    per = tm // HALO
    n_halo = l // HALO
    resident = lambda shape: pl.BlockSpec(shape, lambda bi, i: (0,) * len(shape),
                                          pipeline_mode=pl.Buffered(1))
    return pl.pallas_call(
        _ffn_kernel,
        grid=(b, l // tm),
        in_specs=[
            pl.BlockSpec((1, HALO, d), lambda bi, i: (bi, jnp.maximum(i * per - 1, 0), 0)),
            pl.BlockSpec((1, tm, d), lambda bi, i: (bi, i, 0)),
            pl.BlockSpec((1, HALO, d), lambda bi, i: (bi, jnp.minimum((i + 1) * per, n_halo - 1), 0)),
            pl.BlockSpec((1, tm, d), lambda bi, i: (bi, i, 0)),
            pl.BlockSpec((1, N_MOD, d), lambda bi, i: (bi, 0, 0)),
            pl.BlockSpec((4, d), lambda bi, i: (0, 0)),
            resident((nc, d, fc)), resident((nc, d, fc)),
            resident((nc, 3, fc)), resident((nc, 3, fc)),
            resident((nc, 1, fc)), resident((nc, 1, fc)),
            resident((nc, fc, d)),
        ],
        out_specs=pl.BlockSpec((1, tm, d), lambda bi, i: (bi, i, 0)),
        out_shape=jax.ShapeDtypeStruct((b, l, d), F32),
        scratch_shapes=[
            pltpu.VMEM((tm + 2 * HALO, d), BF16),
            pltpu.VMEM((tm + 2 * HALO, fc), F32),
            pltpu.VMEM((tm + 2 * HALO, fc), F32),
            pltpu.VMEM((tm, d), F32),
        ],
        compiler_params=_cparams(("parallel", "arbitrary")),
        name="conv_ffn",
    )(h2, h2, h2, x1, mod6, ng, wg, wv, cwg, cwv, cbg, cbv, wd)


def _dft_tables(l, d):
    n1, n2, cg = DFT_N1, l // DFT_N1, d // FNET_GROUPS
    a = np.arange(cg)
    ang_c = 2.0 * np.pi * np.outer(a, a) / cg
    csc = np.concatenate([np.cos(ang_c), np.sin(ang_c)], axis=1) / math.sqrt(cg)
    a1 = np.arange(n1)
    ang1 = 2.0 * np.pi * np.outer(a1, a1) / n1
    c1, s1 = np.cos(ang1), np.sin(ang1)
    m1 = np.block([[c1, -s1], [s1, c1]])
    a2 = np.arange(n2)
    ang2 = 2.0 * np.pi * np.outer(a2, a2) / n2
    c2 = np.cos(ang2) / math.sqrt(l)
    s2 = np.sin(ang2) / math.sqrt(l)
    ang_t = 2.0 * np.pi * np.outer(a1, a2) / l
    f32 = lambda t: jnp.asarray(t, F32)
    return (f32(csc).astype(BF16), f32(m1).astype(BF16), f32(c2).astype(BF16), f32(s2).astype(BF16),
            f32(np.cos(ang_t)), f32(np.sin(ang_t)))


def _fourier1_kernel(x_ref, mod_ref, ng_ref, csc_ref, m1_ref, twc_ref, tws_ref, zr_ref, zi_ref):
    d = mod_ref.shape[2]
    cg = csc_ref.shape[0]
    n1 = x_ref.shape[1]
    t2s = x_ref.shape[2] // d
    csc = csc_ref[...]
    m1 = m1_ref[...]
    for t in range(t2s):
        xb = x_ref[0, :, t * d:(t + 1) * d]
        h = (_rms(xb, ng_ref[0:1, :]) * (1.0 + mod_ref[0, 1:2, :]) + mod_ref[0, 0:1, :]).astype(BF16)
        ab = [jnp.dot(h[:, g * cg:(g + 1) * cg], csc, preferred_element_type=F32)
              for g in range(d // cg)]
        a = jnp.concatenate([t_[:, :cg] for t_ in ab], axis=1)
        bm = jnp.concatenate([t_[:, cg:] for t_ in ab], axis=1)
        y = jnp.dot(m1, jnp.concatenate([a, bm], axis=0).astype(BF16), preferred_element_type=F32)
        yr = y[:n1]
        yin = y[n1:]
        cc = jnp.tile(twc_ref[:, t * HEAD_DIM:(t + 1) * HEAD_DIM], (1, d // HEAD_DIM))
        ss = jnp.tile(tws_ref[:, t * HEAD_DIM:(t + 1) * HEAD_DIM], (1, d // HEAD_DIM))
        zr_ref[0, :, t * d:(t + 1) * d] = (yr * cc - yin * ss).astype(BF16)
        zi_ref[0, :, t * d:(t + 1) * d] = (-(yin * cc) - yr * ss).astype(BF16)


def _fourier1(x, mod6, ng, csc, m1, twc, tws, t2s):
    b, l, d = x.shape
    n1, n2 = DFT_N1, l // DFT_N1
    xv = x.reshape(b, n1, n2 * d)
    lanes = HEAD_DIM
    twc_x = jnp.repeat(twc, lanes, axis=1)
    tws_x = jnp.repeat(tws, lanes, axis=1)
    cg = csc.shape[0]
    return pl.pallas_call(
        _fourier1_kernel,
        grid=(b, n2 // t2s),
        in_specs=[
            pl.BlockSpec((1, n1, t2s * d), lambda bi, j: (bi, 0, j)),
            pl.BlockSpec((1, N_MOD, d), lambda bi, j: (bi, 0, 0)),
            pl.BlockSpec((4, d), lambda bi, j: (0, 0)),
            pl.BlockSpec((cg, 2 * cg), lambda bi, j: (0, 0)),
            pl.BlockSpec((2 * n1, 2 * n1), lambda bi, j: (0, 0)),
            pl.BlockSpec((n1, t2s * lanes), lambda bi, j: (0, j)),
            pl.BlockSpec((n1, t2s * lanes), lambda bi, j: (0, j)),
        ],
        out_specs=[
            pl.BlockSpec((1, n1, t2s * d), lambda bi, j: (bi, 0, j)),
            pl.BlockSpec((1, n1, t2s * d), lambda bi, j: (bi, 0, j)),
        ],
        out_shape=[jax.ShapeDtypeStruct((b, n1, n2 * d), BF16)] * 2,
        compiler_params=_cparams(("parallel", "parallel")),
        name="fnet_stage1",
    )(xv, mod6, ng, csc, m1, twc_x, tws_x)


def _fourier2_kernel(zr_ref, zi_ref, x_ref, c2_ref, s2_ref, w_ref, mod_ref, ng_ref, x1_ref, h2_ref):
    k1s = zr_ref.shape[1]
    n2 = zr_ref.shape[2]
    d = zr_ref.shape[3]
    c2 = c2_ref[...]
    s2 = s2_ref[...]
    fs = []
    for k in range(k1s):
        f = (jnp.dot(c2, zr_ref[0, k], preferred_element_type=F32)
             + jnp.dot(s2, zi_ref[0, k], preferred_element_type=F32))
        fs.append(f.astype(BF16))
    y = jnp.dot(jnp.concatenate(fs, axis=0), w_ref[...], preferred_element_type=F32)
    for k in range(k1s):
        x1, h2 = _residual_and_next(y[k * n2:(k + 1) * n2], x_ref[0, :, k * d:(k + 1) * d], mod_ref, ng_ref)
        x1_ref[0, :, k * d:(k + 1) * d] = x1
        h2_ref[0, :, k * d:(k + 1) * d] = h2.astype(BF16)


def _fourier2(zr, zi, x, c2, s2, w_f, mod6, ng, k1s):
    b, l, d = x.shape
    n1, n2 = DFT_N1, l // DFT_N1
    zr4 = zr.reshape(b, n1, n2, d)
    zi4 = zi.reshape(b, n1, n2, d)
    xv = x.reshape(b, n2, n1 * d)
    x1, h2 = pl.pallas_call(
        _fourier2_kernel,
        grid=(b, n1 // k1s),
        in_specs=[
            pl.BlockSpec((1, k1s, n2, d), lambda bi, j: (bi, j, 0, 0)),
            pl.BlockSpec((1, k1s, n2, d), lambda bi, j: (bi, j, 0, 0)),
            pl.BlockSpec((1, n2, k1s * d), lambda bi, j: (bi, 0, j)),
            pl.BlockSpec((n2, n2), lambda bi, j: (0, 0)),
            pl.BlockSpec((n2, n2), lambda bi, j: (0, 0)),
            pl.BlockSpec((d, d), lambda bi, j: (0, 0)),
            pl.BlockSpec((1, N_MOD, d), lambda bi, j: (bi, 0, 0)),
            pl.BlockSpec((4, d), lambda bi, j: (0, 0)),
        ],
        out_specs=[
            pl.BlockSpec((1, n2, k1s * d), lambda bi, j: (bi, 0, j)),
            pl.BlockSpec((1, n2, k1s * d), lambda bi, j: (bi, 0, j)),
        ],
        out_shape=[jax.ShapeDtypeStruct((b, n2, n1 * d), F32), jax.ShapeDtypeStruct((b, n2, n1 * d), BF16)],
        compiler_params=_cparams(("parallel", "parallel")),
        name="fnet_stage2",
    )(zr4, zi4, xv, c2, s2, w_f, mod6, ng)
    return x1.reshape(b, l, d), h2.reshape(b, l, d)


def _rope_tables(l):
    n_freq = HEAD_DIM // 4
    t = jnp.arange(l)
    inv = ROPE_THETA ** (-jnp.arange(n_freq, dtype=F32) / n_freq)
    ang_row = (t // GRID_W).astype(F32)[:, None] * inv
    ang_col = (t % GRID_W).astype(F32)[:, None] * inv
    cos = jnp.concatenate([jnp.cos(ang_row)] * 2 + [jnp.cos(ang_col)] * 2, axis=1)
    sin = jnp.concatenate([-jnp.sin(ang_row), jnp.sin(ang_row), -jnp.sin(ang_col), jnp.sin(ang_col)], axis=1)
    return cos, sin


def kernel(x, c, ctx, c_ctx, mod_w, mod_b, norm_g, attn_w_in, attn_w_out, q_norm_g, k_norm_g, sink,
           fourier_w_out, ffn_w_up, ffn_conv_w, ffn_conv_b, ffn_w_down):
    b, l, d = x.shape
    c_len = ctx.shape[1]
    assert mod_w.shape[0] == 2 and b + 1 <= COND_ROWS
    assert l % max(QKV_TM, WIN_TQ, FFN_TM, POST_TM, DFT_N1 * F1_T2) == 0

    cond = jnp.zeros((COND_ROWS, d), F32).at[:b].set(c).at[b].set(c_ctx)
    mods = _mods(cond, mod_w, mod_b)
    mod_l0 = mods[0].reshape(COND_ROWS, N_MOD, d)
    mod_l1 = mods[1].reshape(COND_ROWS, N_MOD, d)

    cos, sin = _rope_tables(l)
    w_in = attn_w_in[0].astype(BF16)
    qg = q_norm_g[0].reshape(1, HEAD_DIM)
    kg = k_norm_g[0].reshape(1, HEAD_DIM)
    qat, ka, vat, qb, kb, vb = _qkv(x, mod_l0, lambda bi: bi, norm_g[0], w_in, qg, kg, cos, sin, QKV_TM)
    ones = jnp.ones((c_len, HEAD_DIM), F32)
    _, kac, vact, _, kbc, vbc = _qkv(ctx, mod_l0, lambda bi: b, norm_g[0], w_in, qg, kg,
                                     ones, jnp.zeros_like(ones), c_len)
    oa = _attn_a(qat, kac, vact, ka, vat, ATT_TQ)
    ob = _attn_b(sink[0], qb, kb, vb, kbc, vbc, WIN_TQ)
    x1, h2 = _post_attn(oa, ob, attn_w_out[0].astype(BF16), x, mod_l0, norm_g[0], POST_TM)
    x2 = _ffn(h2, x1, mod_l0, norm_g[0], ffn_w_up[0], ffn_conv_w[0], ffn_conv_b[0], ffn_w_down[0],
              FFN_TM, FFN_FC)

    csc, m1, c2, s2, twc, tws = _dft_tables(l, d)
    zr, zi = _fourier1(x2, mod_l1, norm_g[1], csc, m1, twc, tws, F1_T2)
    x3, h2 = _fourier2(zr, zi, x2, c2, s2, fourier_w_out[0].astype(BF16), mod_l1, norm_g[1], F2_K1)
    return _ffn(h2, x3, mod_l1, norm_g[1], ffn_w_up[1], ffn_conv_w[1], ffn_conv_b[1], ffn_w_down[1],
                FFN_TM, FFN_FC)
```

```python
import functools
import math

import numpy as np
import jax
import jax.numpy as jnp
from jax import lax
from jax.experimental import pallas as pl
from jax.experimental.pallas import tpu as pltpu

F32 = jnp.float32
BF16 = jnp.bfloat16

HEAD_DIM = 128
GRID_W = 64
A_HEADS = 4
A_KV_HEADS = 2
B_HEADS = 4
B_KV_HEADS = 2
GROUP = A_HEADS // A_KV_HEADS
BLOCK = 128
WINDOW = 128
ROPE_THETA = 10000.0
FNET_GROUPS = 4
N_MOD = 6
EPS = 1e-6
LOG2E = math.log2(math.e)
Q_SCALE = HEAD_DIM ** -0.5 * LOG2E
NEG_BIG = -1e30

DFT_N1 = 128
COND_ROWS = 8
VMEM_LIMIT = 56 * 1024 * 1024

QKV_TM = 512
ATT_TQ = 512
ATT_TK = 1024
V_ROWS = HEAD_DIM + 16
WIN_TQ = 512
POST_TM = 512
FFN_TM = 512
FFN_FC = 256
HALO = 16
F1_T2 = 4
F2_K1 = 4


def _cparams(sem):
    return pltpu.CompilerParams(dimension_semantics=sem, vmem_limit_bytes=VMEM_LIMIT)


def _rms(x, g):
    return x * lax.rsqrt(jnp.mean(x * x, axis=-1, keepdims=True) + EPS) * g


def _mod_kernel(cond_ref, w_ref, b_ref, o_ref):
    c = cond_ref[...]
    s = c / (1.0 + jnp.exp(-c))
    o_ref[0] = jnp.dot(s, w_ref[0], preferred_element_type=F32) + b_ref[0]


def _mods(cond, mod_w, mod_b):
    depth, d, n = mod_w.shape
    tn = n // 4
    return pl.pallas_call(
        _mod_kernel,
        grid=(depth, n // tn),
        in_specs=[
            pl.BlockSpec((COND_ROWS, d), lambda l, j: (0, 0)),
            pl.BlockSpec((1, d, tn), lambda l, j: (l, 0, j)),
            pl.BlockSpec((1, 1, tn), lambda l, j: (l, 0, j)),
        ],
        out_specs=pl.BlockSpec((1, COND_ROWS, tn), lambda l, j: (l, 0, j)),
        out_shape=jax.ShapeDtypeStruct((depth, COND_ROWS, n), F32),
        compiler_params=_cparams(("arbitrary", "arbitrary")),
        name="adaln_mods",
    )(cond, mod_w, mod_b.reshape(depth, 1, n))


def _qkv_kernel(x_ref, mod_ref, ng_ref, w_ref, qg_ref, kg_ref, cos_ref, sin_ref,
                qat_ref, ka_ref, vat_ref, qb_ref, kb_ref, vb_ref):
    x = x_ref[0]
    tm = x.shape[0]
    h = _rms(x, ng_ref[0:1, :]) * (1.0 + mod_ref[0, 1:2, :]) + mod_ref[0, 0:1, :]
    proj = jnp.dot(h.astype(BF16), w_ref[...], preferred_element_type=F32)
    cos = cos_ref[...]
    sin = sin_ref[...]
    lane = lax.broadcasted_iota(jnp.int32, (tm, HEAD_DIM), 1)
    low_half = (lane % (HEAD_DIM // 2)) < (HEAD_DIM // 4)

    def rope(t):
        partner = jnp.where(low_half,
                            pltpu.roll(t, HEAD_DIM - HEAD_DIM // 4, 1),
                            pltpu.roll(t, HEAD_DIM // 4, 1))
        return t * cos + partner * sin

    def head(j):
        return proj[:, j * HEAD_DIM:(j + 1) * HEAD_DIM]

    qg = qg_ref[...]
    kg = kg_ref[...]
    col = 0
    for hh in range(A_HEADS):
        q = rope(_rms(head(col + hh), qg)) * Q_SCALE
        qat_ref[0, hh] = q.T.astype(BF16)
    col += A_HEADS
    for hh in range(B_HEADS):
        qb_ref[0, hh] = (rope(head(col + hh)) * Q_SCALE).astype(BF16)
    col += B_HEADS
    for hh in range(A_KV_HEADS):
        ka_ref[0, hh] = rope(_rms(head(col + hh), kg)).astype(BF16)
    col += A_KV_HEADS
    for hh in range(A_KV_HEADS):
        vat_ref[0, hh, 0, 0:HEAD_DIM, :] = head(col + hh).T.astype(BF16)
        vat_ref[0, hh, 0, HEAD_DIM:, :] = jnp.ones((V_ROWS - HEAD_DIM, tm), BF16)
    col += A_KV_HEADS
    for hh in range(B_KV_HEADS):
        kb_ref[0, hh] = rope(head(col + hh)).astype(BF16)
    col += B_KV_HEADS
    for hh in range(B_KV_HEADS):
        vb_ref[0, hh] = head(col + hh).astype(BF16)


def _qkv(x, mod6, mod_row, ng, w_in, qg, kg, cos, sin, tm):
    b, l, d = x.shape
    n = w_in.shape[1]
    nt = l // tm
    return pl.pallas_call(
        _qkv_kernel,
        grid=(b, nt),
        in_specs=[
            pl.BlockSpec((1, tm, d), lambda bi, i: (bi, i, 0)),
            pl.BlockSpec((1, N_MOD, d), lambda bi, i: (mod_row(bi), 0, 0)),
            pl.BlockSpec((4, d), lambda bi, i: (0, 0)),
            pl.BlockSpec((d, n), lambda bi, i: (0, 0)),
            pl.BlockSpec((1, HEAD_DIM), lambda bi, i: (0, 0)),
            pl.BlockSpec((1, HEAD_DIM), lambda bi, i: (0, 0)),
            pl.BlockSpec((tm, HEAD_DIM), lambda bi, i: (i, 0)),
            pl.BlockSpec((tm, HEAD_DIM), lambda bi, i: (i, 0)),
        ],
        out_specs=[
            pl.BlockSpec((1, A_HEADS, HEAD_DIM, tm), lambda bi, i: (bi, 0, 0, i)),
            pl.BlockSpec((1, A_KV_HEADS, tm, HEAD_DIM), lambda bi, i: (bi, 0, i, 0)),
            pl.BlockSpec((1, A_KV_HEADS, 1, V_ROWS, tm), lambda bi, i: (bi, 0, i, 0, 0)),
            pl.BlockSpec((1, B_HEADS, tm, HEAD_DIM), lambda bi, i: (bi, 0, i, 0)),
            pl.BlockSpec((1, B_KV_HEADS, tm, HEAD_DIM), lambda bi, i: (bi, 0, i, 0)),
            pl.BlockSpec((1, B_KV_HEADS, tm, HEAD_DIM), lambda bi, i: (bi, 0, i, 0)),
        ],
        out_shape=[
            jax.ShapeDtypeStruct((b, A_HEADS, HEAD_DIM, l), BF16),
            jax.ShapeDtypeStruct((b, A_KV_HEADS, l, HEAD_DIM), BF16),
            jax.ShapeDtypeStruct((b, A_KV_HEADS, nt, V_ROWS, tm), BF16),
            jax.ShapeDtypeStruct((b, B_HEADS, l, HEAD_DIM), BF16),
            jax.ShapeDtypeStruct((b, B_KV_HEADS, l, HEAD_DIM), BF16),
            jax.ShapeDtypeStruct((b, B_KV_HEADS, l, HEAD_DIM), BF16),
        ],
        compiler_params=_cparams(("parallel", "parallel")),
        name="qkv_proj",
    )(x, mod6, ng, w_in, qg, kg, cos, sin)


def _attn_a_kernel(qt_ref, kc_ref, vct_ref, k_ref, vt_ref, o_ref,
                   sc_ref, pc_ref, s0_ref, s1_ref, p0_ref, p1_ref, acc_ref):
    tq = qt_ref.shape[3]
    ts = vt_ref.shape[4]
    tk = s0_ref.shape[0]
    sub = tk // ts
    n_chunks = vt_ref.shape[2] // sub
    assert n_chunks % 2 == 0 and n_chunks >= 4
    qt = jnp.concatenate([qt_ref[0, g] for g in range(GROUP)], axis=1)

    def k_chunk(j):
        return k_ref[0, 0, pl.ds(pl.multiple_of(j * tk, tk), tk), :]

    def scores(kblk, s_ref):
        s = jnp.dot(kblk, qt, preferred_element_type=F32)
        s_ref[...] = s
        return jnp.max(s, axis=0, keepdims=True)

    def softmax(s_ref, p_ref, mc, m):
        m_new = jnp.maximum(m, mc)
        p_ref[...] = jnp.exp2(s_ref[...] - m_new).astype(BF16)
        return m_new, jnp.exp2(m - m_new)

    def accumulate(vt_blocks, p_ref, alpha):
        pv = None
        for u, vtblk in enumerate(vt_blocks):
            part = jnp.dot(vtblk, p_ref[u * vtblk.shape[1]:(u + 1) * vtblk.shape[1], :],
                           preferred_element_type=F32)
            pv = part if pv is None else pv + part
        acc_ref[...] = alpha * acc_ref[...] + pv

    def v_chunk(j):
        return [vt_ref[0, 0, j * sub + u] for u in range(sub)]

    acc_ref[...] = jnp.zeros(acc_ref.shape, F32)
    m = jnp.full((1, GROUP * tq), NEG_BIG, F32)
    mc_ctx = scores(kc_ref[0, 0], sc_ref)
    mc0 = scores(k_chunk(0), s0_ref)
    m, alpha_ctx = softmax(sc_ref, pc_ref, mc_ctx, m)
    mc1 = scores(k_chunk(1), s1_ref)
    m, alpha0 = softmax(s0_ref, p0_ref, mc0, m)
    accumulate([vct_ref[0, 0, 0]], pc_ref, alpha_ctx)

    def pair(jj, carry):
        m, alpha_prev, mc = carry
        j = 2 * jj + 1
        mc_next = scores(k_chunk(j + 1), s0_ref)
        m, alpha = softmax(s1_ref, p1_ref, mc, m)
        accumulate(v_chunk(j - 1), p0_ref, alpha_prev)
        mc_next2 = scores(k_chunk(j + 2), s1_ref)
        m, alpha2 = softmax(s0_ref, p0_ref, mc_next, m)
        accumulate(v_chunk(j), p1_ref, alpha)
        return m, alpha2, mc_next2

    m, alpha_prev, mc = lax.fori_loop(0, (n_chunks - 2) // 2, pair, (m, alpha0, mc1))
    m, alpha = softmax(s1_ref, p1_ref, mc, m)
    accumulate(v_chunk(n_chunks - 2), p0_ref, alpha_prev)
    accumulate(v_chunk(n_chunks - 1), p1_ref, alpha)

    acc = acc_ref[...]
    o = (acc[:HEAD_DIM] / acc[HEAD_DIM:HEAD_DIM + 1]).T
    for g in range(GROUP):
        o_ref[0, :, g * HEAD_DIM:(g + 1) * HEAD_DIM] = o[g * tq:(g + 1) * tq].astype(o_ref.dtype)


def _attn_a(qt, kc, vct, k, vt, tq, tk):
    b, _, _, l = qt.shape
    c_len = kc.shape[2]
    n_stored, ts = vt.shape[2], vt.shape[4]
    assert tk % ts == 0 and l % tk == 0
    return pl.pallas_call(
        _attn_a_kernel,
        grid=(b, A_KV_HEADS, l // tq),
        in_specs=[
            pl.BlockSpec((1, GROUP, HEAD_DIM, tq), lambda bi, kv, i: (bi, kv, 0, i)),
            pl.BlockSpec((1, 1, c_len, HEAD_DIM), lambda bi, kv, i: (bi, kv, 0, 0)),
            pl.BlockSpec((1, 1, 1, V_ROWS, c_len), lambda bi, kv, i: (bi, kv, 0, 0, 0)),
            pl.BlockSpec((1, 1, l, HEAD_DIM), lambda bi, kv, i: (bi, kv, 0, 0)),
            pl.BlockSpec((1, 1, n_stored, V_ROWS, ts), lambda bi, kv, i: (bi, kv, 0, 0, 0)),
        ],
        out_specs=pl.BlockSpec((1, tq, GROUP * HEAD_DIM), lambda bi, kv, i: (bi, i, kv)),
        out_shape=jax.ShapeDtypeStruct((b, l, A_HEADS * HEAD_DIM), BF16),
        scratch_shapes=[
            pltpu.VMEM((c_len, GROUP * tq), F32),
            pltpu.VMEM((c_len, GROUP * tq), BF16),
            pltpu.VMEM((tk, GROUP * tq), F32),
            pltpu.VMEM((tk, GROUP * tq), F32),
            pltpu.VMEM((tk, GROUP * tq), BF16),
            pltpu.VMEM((tk, GROUP * tq), BF16),
            pltpu.VMEM((V_ROWS, GROUP * tq), F32),
        ],
        compiler_params=_cparams(("parallel", "parallel", "arbitrary")),
        name="attn_global",
    )(qt, kc, vct, k, vt)


def _attn_b_kernel(sink_ref, q_ref, kp_ref, km_ref, kn_ref, vp_ref, vm_ref, vn_ref,
                   kc_ref, vc_ref, o_ref, *, seq_len):
    kv = pl.program_id(1)
    i = pl.program_id(2)
    tq = q_ref.shape[2]
    sub = tq // BLOCK
    band = 3 * BLOCK
    kcat = jnp.concatenate([kp_ref[0, 0], km_ref[0, 0], kn_ref[0, 0]], axis=0)
    vcat = jnp.concatenate([vp_ref[0, 0], vm_ref[0, 0], vn_ref[0, 0]], axis=0)
    kc = kc_ref[0, 0]
    vc = vc_ref[0, 0]
    rows = GROUP * BLOCK
    nt = (((1,), (1,)), ((), ()))

    r = lax.broadcasted_iota(jnp.int32, (rows, band), 0) % BLOCK
    c = lax.broadcasted_iota(jnp.int32, (rows, band), 1)
    in_window = jnp.abs(c - BLOCK - r) <= WINDOW
    head_row = lax.broadcasted_iota(jnp.int32, (rows, 1), 0) // BLOCK
    sink2 = jnp.zeros((rows, 1), F32)
    for g in range(GROUP):
        sink2 = jnp.where(head_row == g, sink_ref[kv * GROUP + g] * LOG2E, sink2)

    for j in range(sub):
        q2 = jnp.concatenate([q_ref[0, g, j * BLOCK:(j + 1) * BLOCK, :] for g in range(GROUP)], axis=0)
        kband = kcat[j * BLOCK:j * BLOCK + band]
        vband = vcat[j * BLOCK:j * BLOCK + band]
        s_loc = lax.dot_general(q2, kband, nt, preferred_element_type=F32)
        s_ctx = lax.dot_general(q2, kc, nt, preferred_element_type=F32)
        kpos = (i * sub + (j - 1)) * BLOCK + c
        valid = in_window & (kpos >= 0) & (kpos < seq_len)
        s_loc = jnp.where(valid, s_loc, NEG_BIG)
        m = jnp.maximum(jnp.maximum(jnp.max(s_loc, axis=-1, keepdims=True),
                                    jnp.max(s_ctx, axis=-1, keepdims=True)), sink2)
        p_loc = jnp.exp2(s_loc - m)
        p_ctx = jnp.exp2(s_ctx - m)
        denom = (jnp.sum(p_loc, axis=-1, keepdims=True) + jnp.sum(p_ctx, axis=-1, keepdims=True)
                 + jnp.exp2(sink2 - m))
        o = (jnp.dot(p_ctx.astype(BF16), vc, preferred_element_type=F32)
             + jnp.dot(p_loc.astype(BF16), vband, preferred_element_type=F32)) / denom
        for g in range(GROUP):
            o_ref[0, j * BLOCK:(j + 1) * BLOCK, g * HEAD_DIM:(g + 1) * HEAD_DIM] = (
                o[g * BLOCK:(g + 1) * BLOCK].astype(o_ref.dtype))


def _attn_b(sink, q, k, v, kc, vc, tq):
    b, _, l, _ = q.shape
    c_len = kc.shape[2]
    sub = tq // BLOCK
    nb = l // BLOCK

    def main(bi, kv, i):
        return (bi, kv, i, 0)

    def prev(bi, kv, i):
        return (bi, kv, jnp.maximum(i * sub - 1, 0), 0)

    def nxt(bi, kv, i):
        return (bi, kv, jnp.minimum((i + 1) * sub, nb - 1), 0)

    def ctx_map(bi, kv, i):
        return (bi, kv, 0, 0)

    edge = lambda fn: pl.BlockSpec((1, 1, BLOCK, HEAD_DIM), fn)
    return pl.pallas_call(
        functools.partial(_attn_b_kernel, seq_len=l),
        grid=(b, B_KV_HEADS, l // tq),
        in_specs=[
            pl.BlockSpec(memory_space=pltpu.SMEM),
            pl.BlockSpec((1, GROUP, tq, HEAD_DIM), main),
            edge(prev), pl.BlockSpec((1, 1, tq, HEAD_DIM), main), edge(nxt),
            edge(prev), pl.BlockSpec((1, 1, tq, HEAD_DIM), main), edge(nxt),
            pl.BlockSpec((1, 1, c_len, HEAD_DIM), ctx_map),
            pl.BlockSpec((1, 1, c_len, HEAD_DIM), ctx_map),
        ],
        out_specs=pl.BlockSpec((1, tq, GROUP * HEAD_DIM), lambda bi, kv, i: (bi, i, kv)),
        out_shape=jax.ShapeDtypeStruct((b, l, B_HEADS * HEAD_DIM), BF16),
        compiler_params=_cparams(("parallel", "parallel", "arbitrary")),
        name="attn_window",
    )(sink, q, k, k, k, v, v, v, kc, vc)


def _residual_and_next(y, x, mod_ref, ng_ref):
    x1 = x + mod_ref[0, 2:3, :] * _rms(y, ng_ref[1:2, :])
    h2 = _rms(x1, ng_ref[2:3, :]) * (1.0 + mod_ref[0, 4:5, :]) + mod_ref[0, 3:4, :]
    return x1, h2


def _post_attn_kernel(oa_ref, ob_ref, w_ref, x_ref, mod_ref, ng_ref, x1_ref, h2_ref):
    na = oa_ref.shape[2]
    y = (jnp.dot(oa_ref[0], w_ref[0:na, :], preferred_element_type=F32)
         + jnp.dot(ob_ref[0], w_ref[na:, :], preferred_element_type=F32))
    x1, h2 = _residual_and_next(y, x_ref[0], mod_ref, ng_ref)
    x1_ref[0] = x1
    h2_ref[0] = h2.astype(BF16)


def _post_attn(oa, ob, w_out, x, mod6, ng, tm):
    b, l, d = x.shape
    na, nb_ = oa.shape[2], ob.shape[2]
    return pl.pallas_call(
        _post_attn_kernel,
        grid=(b, l // tm),
        in_specs=[
            pl.BlockSpec((1, tm, na), lambda bi, i: (bi, i, 0)),
            pl.BlockSpec((1, tm, nb_), lambda bi, i: (bi, i, 0)),
            pl.BlockSpec((na + nb_, d), lambda bi, i: (0, 0)),
            pl.BlockSpec((1, tm, d), lambda bi, i: (bi, i, 0)),
            pl.BlockSpec((1, N_MOD, d), lambda bi, i: (bi, 0, 0)),
            pl.BlockSpec((4, d), lambda bi, i: (0, 0)),
        ],
        out_specs=[
            pl.BlockSpec((1, tm, d), lambda bi, i: (bi, i, 0)),
            pl.BlockSpec((1, tm, d), lambda bi, i: (bi, i, 0)),
        ],
        out_shape=[jax.ShapeDtypeStruct((b, l, d), F32), jax.ShapeDtypeStruct((b, l, d), BF16)],
        compiler_params=_cparams(("parallel", "parallel")),
        name="attn_out_proj",
    )(oa, ob, w_out, x, mod6, ng)


def _ffn_kernel(hp_ref, hm_ref, hn_ref, x_ref, mod_ref, ng_ref, wg_ref, wv_ref, cwg_ref, cwv_ref,
                cbg_ref, cbv_ref, wd_ref, o_ref, hcat_ref, ug_ref, uv_ref, acc_ref):
    i = pl.program_id(1)
    last = pl.num_programs(1) - 1
    tm = hm_ref.shape[1]
    n_chunks = wg_ref.shape[0]
    hp = hp_ref[0]
    hn = hn_ref[0]
    hcat_ref[0:HALO, :] = jnp.where(i > 0, hp, jnp.zeros_like(hp))
    hcat_ref[HALO:HALO + tm, :] = hm_ref[0]
    hcat_ref[HALO + tm:, :] = jnp.where(i < last, hn, jnp.zeros_like(hn))
    acc_ref[...] = jnp.zeros(acc_ref.shape, F32)

    def conv(u_ref, cw, cb):
        return (u_ref[pl.ds(HALO - 1, tm), :] * cw[0:1, :] + u_ref[pl.ds(HALO, tm), :] * cw[1:2, :]
                + u_ref[pl.ds(HALO + 1, tm), :] * cw[2:3, :] + cb)

    def body(c, carry):
        hcat = hcat_ref[...]
        ug_ref[...] = jnp.dot(hcat, wg_ref[c], preferred_element_type=F32)
        uv_ref[...] = jnp.dot(hcat, wv_ref[c], preferred_element_type=F32)
        gate = conv(ug_ref, cwg_ref[c], cbg_ref[c])
        val = conv(uv_ref, cwv_ref[c], cbv_ref[c])
        act = gate / (1.0 + jnp.exp(-gate)) * val
        acc_ref[...] += jnp.dot(act.astype(BF16), wd_ref[c], preferred_element_type=F32)
        return carry

    lax.fori_loop(0, n_chunks, body, 0)
    o_ref[0] = x_ref[0] + mod_ref[0, 5:6, :] * _rms(acc_ref[...], ng_ref[3:4, :])


def _ffn(h2, x1, mod6, ng, w_up, conv_w, conv_b, w_down, tm, fc):
    b, l, d = x1.shape
    f = w_down.shape[0]
    nc = f // fc
    wg = w_up[:, :f].reshape(d, nc, fc).transpose(1, 0, 2).astype(BF16)
    wv = w_up[:, f:].reshape(d, nc, fc).transpose(1, 0, 2).astype(BF16)
    wd = w_down.reshape(nc, fc, d).astype(BF16)
    cwg = conv_w[:, :f].reshape(3, nc, fc).transpose(1, 0, 2)
    cwv = conv_w[:, f:].reshape(3, nc, fc).transpose(1, 0, 2)
    cbg = conv_b[:f].reshape(nc, 1, fc)
    cbv = conv_b[f:].reshape(nc, 1, fc)
    per = tm // HALO
    n_halo = l // HALO
    resident = lambda shape: pl.BlockSpec(shape, lambda bi, i: (0,) * len(shape),
                                          pipeline_mode=pl.Buffered(1))
    return pl.pallas_call(
        _ffn_kernel,
        grid=(b, l // tm),
        in_specs=[
            pl.BlockSpec((1, HALO, d), lambda bi, i: (bi, jnp.maximum(i * per - 1, 0), 0)),
            pl.BlockSpec((1, tm, d), lambda bi, i: (bi, i, 0)),
            pl.BlockSpec((1, HALO, d), lambda bi, i: (bi, jnp.minimum((i + 1) * per, n_halo - 1), 0)),
            pl.BlockSpec((1, tm, d), lambda bi, i: (bi, i, 0)),
            pl.BlockSpec((1, N_MOD, d), lambda bi, i: (bi, 0, 0)),
            pl.BlockSpec((4, d), lambda bi, i: (0, 0)),
            resident((nc, d, fc)), resident((nc, d, fc)),
            resident((nc, 3, fc)), resident((nc, 3, fc)),
            resident((nc, 1, fc)), resident((nc, 1, fc)),
            resident((nc, fc, d)),
        ],
        out_specs=pl.BlockSpec((1, tm, d), lambda bi, i: (bi, i, 0)),
        out_shape=jax.ShapeDtypeStruct((b, l, d), F32),
        scratch_shapes=[
            pltpu.VMEM((tm + 2 * HALO, d), BF16),
            pltpu.VMEM((tm + 2 * HALO, fc), F32),
            pltpu.VMEM((tm + 2 * HALO, fc), F32),
            pltpu.VMEM((tm, d), F32),
        ],
        compiler_params=_cparams(("parallel", "arbitrary")),
        name="conv_ffn",
    )(h2, h2, h2, x1, mod6, ng, wg, wv, cwg, cwv, cbg, cbv, wd)


def _dft_tables(l, d):
    n1, n2, cg = DFT_N1, l // DFT_N1, d // FNET_GROUPS
    a = np.arange(cg)
    ang_c = 2.0 * np.pi * np.outer(a, a) / cg
    csc = np.concatenate([np.cos(ang_c), np.sin(ang_c)], axis=1) / math.sqrt(cg)
    a1 = np.arange(n1)
    ang1 = 2.0 * np.pi * np.outer(a1, a1) / n1
    c1, s1 = np.cos(ang1), np.sin(ang1)
    m1 = np.block([[c1, -s1], [s1, c1]])
    a2 = np.arange(n2)
    ang2 = 2.0 * np.pi * np.outer(a2, a2) / n2
    c2 = np.cos(ang2) / math.sqrt(l)
    s2 = np.sin(ang2) / math.sqrt(l)
    ang_t = 2.0 * np.pi * np.outer(a1, a2) / l
    f32 = lambda t: jnp.asarray(t, F32)
    return (f32(csc).astype(BF16), f32(m1).astype(BF16), f32(c2).astype(BF16), f32(s2).astype(BF16),
            f32(np.cos(ang_t)), f32(np.sin(ang_t)))


def _fourier1_kernel(x_ref, mod_ref, ng_ref, csc_ref, m1_ref, twc_ref, tws_ref, zr_ref, zi_ref):
    d = mod_ref.shape[2]
    cg = csc_ref.shape[0]
    n1 = x_ref.shape[1]
    t2s = x_ref.shape[2] // d
    csc = csc_ref[...]
    m1 = m1_ref[...]
    for t in range(t2s):
        xb = x_ref[0, :, t * d:(t + 1) * d]
        h = (_rms(xb, ng_ref[0:1, :]) * (1.0 + mod_ref[0, 1:2, :]) + mod_ref[0, 0:1, :]).astype(BF16)
        ab = [jnp.dot(h[:, g * cg:(g + 1) * cg], csc, preferred_element_type=F32)
              for g in range(d // cg)]
        a = jnp.concatenate([t_[:, :cg] for t_ in ab], axis=1)
        bm = jnp.concatenate([t_[:, cg:] for t_ in ab], axis=1)
        y = jnp.dot(m1, jnp.concatenate([a, bm], axis=0).astype(BF16), preferred_element_type=F32)
        yr = y[:n1]
        yin = y[n1:]
        cc = jnp.tile(twc_ref[:, t * HEAD_DIM:(t + 1) * HEAD_DIM], (1, d // HEAD_DIM))
        ss = jnp.tile(tws_ref[:, t * HEAD_DIM:(t + 1) * HEAD_DIM], (1, d // HEAD_DIM))
        zr_ref[0, :, t * d:(t + 1) * d] = (yr * cc - yin * ss).astype(BF16)
        zi_ref[0, :, t * d:(t + 1) * d] = (-(yin * cc) - yr * ss).astype(BF16)


def _fourier1(x, mod6, ng, csc, m1, twc, tws, t2s):
    b, l, d = x.shape
    n1, n2 = DFT_N1, l // DFT_N1
    xv = x.reshape(b, n1, n2 * d)
    lanes = HEAD_DIM
    twc_x = jnp.repeat(twc, lanes, axis=1)
    tws_x = jnp.repeat(tws, lanes, axis=1)
    cg = csc.shape[0]
    return pl.pallas_call(
        _fourier1_kernel,
        grid=(b, n2 // t2s),
        in_specs=[
            pl.BlockSpec((1, n1, t2s * d), lambda bi, j: (bi, 0, j)),
            pl.BlockSpec((1, N_MOD, d), lambda bi, j: (bi, 0, 0)),
            pl.BlockSpec((4, d), lambda bi, j: (0, 0)),
            pl.BlockSpec((cg, 2 * cg), lambda bi, j: (0, 0)),
            pl.BlockSpec((2 * n1, 2 * n1), lambda bi, j: (0, 0)),
            pl.BlockSpec((n1, t2s * lanes), lambda bi, j: (0, j)),
            pl.BlockSpec((n1, t2s * lanes), lambda bi, j: (0, j)),
        ],
        out_specs=[
            pl.BlockSpec((1, n1, t2s * d), lambda bi, j: (bi, 0, j)),
            pl.BlockSpec((1, n1, t2s * d), lambda bi, j: (bi, 0, j)),
        ],
        out_shape=[jax.ShapeDtypeStruct((b, n1, n2 * d), BF16)] * 2,
        compiler_params=_cparams(("parallel", "parallel")),
        name="fnet_stage1",
    )(xv, mod6, ng, csc, m1, twc_x, tws_x)


def _fourier2_kernel(zr_ref, zi_ref, x_ref, c2_ref, s2_ref, w_ref, mod_ref, ng_ref, x1_ref, h2_ref):
    k1s = zr_ref.shape[1]
    n2 = zr_ref.shape[2]
    d = zr_ref.shape[3]
    c2 = c2_ref[...]
    s2 = s2_ref[...]
    fs = []
    for k in range(k1s):
        f = (jnp.dot(c2, zr_ref[0, k], preferred_element_type=F32)
             + jnp.dot(s2, zi_ref[0, k], preferred_element_type=F32))
        fs.append(f.astype(BF16))
    y = jnp.dot(jnp.concatenate(fs, axis=0), w_ref[...], preferred_element_type=F32)
    for k in range(k1s):
        x1, h2 = _residual_and_next(y[k * n2:(k + 1) * n2], x_ref[0, :, k * d:(k + 1) * d], mod_ref, ng_ref)
        x1_ref[0, :, k * d:(k + 1) * d] = x1
        h2_ref[0, :, k * d:(k + 1) * d] = h2.astype(BF16)


def _fourier2(zr, zi, x, c2, s2, w_f, mod6, ng, k1s):
    b, l, d = x.shape
    n1, n2 = DFT_N1, l // DFT_N1
    zr4 = zr.reshape(b, n1, n2, d)
    zi4 = zi.reshape(b, n1, n2, d)
    xv = x.reshape(b, n2, n1 * d)
    x1, h2 = pl.pallas_call(
        _fourier2_kernel,
        grid=(b, n1 // k1s),
        in_specs=[
            pl.BlockSpec((1, k1s, n2, d), lambda bi, j: (bi, j, 0, 0)),
            pl.BlockSpec((1, k1s, n2, d), lambda bi, j: (bi, j, 0, 0)),
            pl.BlockSpec((1, n2, k1s * d), lambda bi, j: (bi, 0, j)),
            pl.BlockSpec((n2, n2), lambda bi, j: (0, 0)),
            pl.BlockSpec((n2, n2), lambda bi, j: (0, 0)),
            pl.BlockSpec((d, d), lambda bi, j: (0, 0)),
            pl.BlockSpec((1, N_MOD, d), lambda bi, j: (bi, 0, 0)),
            pl.BlockSpec((4, d), lambda bi, j: (0, 0)),
        ],
        out_specs=[
            pl.BlockSpec((1, n2, k1s * d), lambda bi, j: (bi, 0, j)),
            pl.BlockSpec((1, n2, k1s * d), lambda bi, j: (bi, 0, j)),
        ],
        out_shape=[jax.ShapeDtypeStruct((b, n2, n1 * d), F32), jax.ShapeDtypeStruct((b, n2, n1 * d), BF16)],
        compiler_params=_cparams(("parallel", "parallel")),
        name="fnet_stage2",
    )(zr4, zi4, xv, c2, s2, w_f, mod6, ng)
    return x1.reshape(b, l, d), h2.reshape(b, l, d)


def _rope_tables(l):
    n_freq = HEAD_DIM // 4
    t = jnp.arange(l)
    inv = ROPE_THETA ** (-jnp.arange(n_freq, dtype=F32) / n_freq)
    ang_row = (t // GRID_W).astype(F32)[:, None] * inv
    ang_col = (t % GRID_W).astype(F32)[:, None] * inv
    cos = jnp.concatenate([jnp.cos(ang_row)] * 2 + [jnp.cos(ang_col)] * 2, axis=1)
    sin = jnp.concatenate([-jnp.sin(ang_row), jnp.sin(ang_row), -jnp.sin(ang_col), jnp.sin(ang_col)], axis=1)
    return cos, sin


def kernel(x, c, ctx, c_ctx, mod_w, mod_b, norm_g, attn_w_in, attn_w_out, q_norm_g, k_norm_g, sink,
           fourier_w_out, ffn_w_up, ffn_conv_w, ffn_conv_b, ffn_w_down):
    b, l, d = x.shape
    c_len = ctx.shape[1]
    assert mod_w.shape[0] == 2 and b + 1 <= COND_ROWS
    assert l % max(QKV_TM, WIN_TQ, FFN_TM, POST_TM, DFT_N1 * F1_T2) == 0

    cond = jnp.zeros((COND_ROWS, d), F32).at[:b].set(c).at[b].set(c_ctx)
    mods = _mods(cond, mod_w, mod_b)
    mod_l0 = mods[0].reshape(COND_ROWS, N_MOD, d)
    mod_l1 = mods[1].reshape(COND_ROWS, N_MOD, d)

    cos, sin = _rope_tables(l)
    w_in = attn_w_in[0].astype(BF16)
    qg = q_norm_g[0].reshape(1, HEAD_DIM)
    kg = k_norm_g[0].reshape(1, HEAD_DIM)
    qat, ka, vat, qb, kb, vb = _qkv(x, mod_l0, lambda bi: bi, norm_g[0], w_in, qg, kg, cos, sin, QKV_TM)
    ones = jnp.ones((c_len, HEAD_DIM), F32)
    _, kac, vact, _, kbc, vbc = _qkv(ctx, mod_l0, lambda bi: b, norm_g[0], w_in, qg, kg,
                                     ones, jnp.zeros_like(ones), c_len)
    oa = _attn_a(qat, kac, vact, ka, vat, ATT_TQ, ATT_TK)
    ob = _attn_b(sink[0], qb, kb, vb, kbc, vbc, WIN_TQ)
    x1, h2 = _post_attn(oa, ob, attn_w_out[0].astype(BF16), x, mod_l0, norm_g[0], POST_TM)
    x2 = _ffn(h2, x1, mod_l0, norm_g[0], ffn_w_up[0], ffn_conv_w[0], ffn_conv_b[0], ffn_w_down[0],
              FFN_TM, FFN_FC)

    csc, m1, c2, s2, twc, tws = _dft_tables(l, d)
    zr, zi = _fourier1(x2, mod_l1, norm_g[1], csc, m1, twc, tws, F1_T2)
    x3, h2 = _fourier2(zr, zi, x2, c2, s2, fourier_w_out[0].astype(BF16), mod_l1, norm_g[1], F2_K1)
    return _ffn(h2, x3, mod_l1, norm_g[1], ffn_w_up[1], ffn_conv_w[1], ffn_conv_b[1], ffn_w_down[1],
                FFN_TM, FFN_FC)
```

```python
import functools
import math

import numpy as np
import jax
import jax.numpy as jnp
from jax import lax
from jax.experimental import pallas as pl
from jax.experimental.pallas import tpu as pltpu

F32 = jnp.float32
BF16 = jnp.bfloat16

HEAD_DIM = 128
GRID_W = 64
A_HEADS = 4
A_KV_HEADS = 2
B_HEADS = 4
B_KV_HEADS = 2
GROUP = A_HEADS // A_KV_HEADS
BLOCK = 128
WINDOW = 128
ROPE_THETA = 10000.0
FNET_GROUPS = 4
N_MOD = 6
EPS = 1e-6
LOG2E = math.log2(math.e)
Q_SCALE = HEAD_DIM ** -0.5 * LOG2E
NEG_BIG = -1e30

DFT_N1 = 128
COND_ROWS = 8
VMEM_LIMIT = 56 * 1024 * 1024

QKV_TM = 512
ATT_TQ = 512
ATT_TK = 1024
V_ROWS = HEAD_DIM + 16
WIN_TQ = 512
POST_TM = 512
FFN_TM = 512
FFN_FC = 256
HALO = 16
F1_T2 = 8
F2_K1 = 8


def _cparams(sem):
    return pltpu.CompilerParams(dimension_semantics=sem, vmem_limit_bytes=VMEM_LIMIT)


def _rms(x, g):
    return x * lax.rsqrt(jnp.mean(x * x, axis=-1, keepdims=True) + EPS) * g


def _mod_kernel(cond_ref, w_ref, b_ref, o_ref):
    c = cond_ref[...]
    s = c / (1.0 + jnp.exp(-c))
    o_ref[0] = jnp.dot(s, w_ref[0], preferred_element_type=F32) + b_ref[0]


def _mods(cond, mod_w, mod_b):
    depth, d, n = mod_w.shape
    tn = n // 4
    return pl.pallas_call(
        _mod_kernel,
        grid=(depth, n // tn),
        in_specs=[
            pl.BlockSpec((COND_ROWS, d), lambda l, j: (0, 0)),
            pl.BlockSpec((1, d, tn), lambda l, j: (l, 0, j)),
            pl.BlockSpec((1, 1, tn), lambda l, j: (l, 0, j)),
        ],
        out_specs=pl.BlockSpec((1, COND_ROWS, tn), lambda l, j: (l, 0, j)),
        out_shape=jax.ShapeDtypeStruct((depth, COND_ROWS, n), F32),
        compiler_params=_cparams(("arbitrary", "arbitrary")),
        name="adaln_mods",
    )(cond, mod_w, mod_b.reshape(depth, 1, n))


def _qkv_kernel(x_ref, mod_ref, ng_ref, w_ref, qg_ref, kg_ref, cos_ref, sin_ref,
                qat_ref, ka_ref, vat_ref, qb_ref, kb_ref, vb_ref):
    x = x_ref[0]
    tm = x.shape[0]
    h = _rms(x, ng_ref[0:1, :]) * (1.0 + mod_ref[0, 1:2, :]) + mod_ref[0, 0:1, :]
    proj = jnp.dot(h.astype(BF16), w_ref[...], preferred_element_type=F32)
    cos = cos_ref[...]
    sin = sin_ref[...]
    lane = lax.broadcasted_iota(jnp.int32, (tm, HEAD_DIM), 1)
    low_half = (lane % (HEAD_DIM // 2)) < (HEAD_DIM // 4)

    def rope(t):
        partner = jnp.where(low_half,
                            pltpu.roll(t, HEAD_DIM - HEAD_DIM // 4, 1),
                            pltpu.roll(t, HEAD_DIM // 4, 1))
        return t * cos + partner * sin

    def head(j):
        return proj[:, j * HEAD_DIM:(j + 1) * HEAD_DIM]

    qg = qg_ref[...]
    kg = kg_ref[...]
    col = 0
    for hh in range(A_HEADS):
        q = rope(_rms(head(col + hh), qg)) * Q_SCALE
        qat_ref[0, hh] = q.T.astype(BF16)
    col += A_HEADS
    for hh in range(B_HEADS):
        qb_ref[0, hh] = (rope(head(col + hh)) * Q_SCALE).astype(BF16)
    col += B_HEADS
    for hh in range(A_KV_HEADS):
        ka_ref[0, hh] = rope(_rms(head(col + hh), kg)).astype(BF16)
    col += A_KV_HEADS
    for hh in range(A_KV_HEADS):
        vat_ref[0, hh, 0, 0:HEAD_DIM, :] = head(col + hh).T.astype(BF16)
        vat_ref[0, hh, 0, HEAD_DIM:, :] = jnp.ones((V_ROWS - HEAD_DIM, tm), BF16)
    col += A_KV_HEADS
    for hh in range(B_KV_HEADS):
        kb_ref[0, hh] = rope(head(col + hh)).astype(BF16)
    col += B_KV_HEADS
    for hh in range(B_KV_HEADS):
        vb_ref[0, hh] = head(col + hh).astype(BF16)


def _qkv(x, mod6, mod_row, ng, w_in, qg, kg, cos, sin, tm):
    b, l, d = x.shape
    n = w_in.shape[1]
    nt = l // tm
    return pl.pallas_call(
        _qkv_kernel,
        grid=(b, nt),
        in_specs=[
            pl.BlockSpec((1, tm, d), lambda bi, i: (bi, i, 0)),
            pl.BlockSpec((1, N_MOD, d), lambda bi, i: (mod_row(bi), 0, 0)),
            pl.BlockSpec((4, d), lambda bi, i: (0, 0)),
            pl.BlockSpec((d, n), lambda bi, i: (0, 0)),
            pl.BlockSpec((1, HEAD_DIM), lambda bi, i: (0, 0)),
            pl.BlockSpec((1, HEAD_DIM), lambda bi, i: (0, 0)),
            pl.BlockSpec((tm, HEAD_DIM), lambda bi, i: (i, 0)),
            pl.BlockSpec((tm, HEAD_DIM), lambda bi, i: (i, 0)),
        ],
        out_specs=[
            pl.BlockSpec((1, A_HEADS, HEAD_DIM, tm), lambda bi, i: (bi, 0, 0, i)),
            pl.BlockSpec((1, A_KV_HEADS, tm, HEAD_DIM), lambda bi, i: (bi, 0, i, 0)),
            pl.BlockSpec((1, A_KV_HEADS, 1, V_ROWS, tm), lambda bi, i: (bi, 0, i, 0, 0)),
            pl.BlockSpec((1, B_HEADS, tm, HEAD_DIM), lambda bi, i: (bi, 0, i, 0)),
            pl.BlockSpec((1, B_KV_HEADS, tm, HEAD_DIM), lambda bi, i: (bi, 0, i, 0)),
            pl.BlockSpec((1, B_KV_HEADS, tm, HEAD_DIM), lambda bi, i: (bi, 0, i, 0)),
        ],
        out_shape=[
            jax.ShapeDtypeStruct((b, A_HEADS, HEAD_DIM, l), BF16),
            jax.ShapeDtypeStruct((b, A_KV_HEADS, l, HEAD_DIM), BF16),
            jax.ShapeDtypeStruct((b, A_KV_HEADS, nt, V_ROWS, tm), BF16),
            jax.ShapeDtypeStruct((b, B_HEADS, l, HEAD_DIM), BF16),
            jax.ShapeDtypeStruct((b, B_KV_HEADS, l, HEAD_DIM), BF16),
            jax.ShapeDtypeStruct((b, B_KV_HEADS, l, HEAD_DIM), BF16),
        ],
        compiler_params=_cparams(("parallel", "parallel")),
        name="qkv_proj",
    )(x, mod6, ng, w_in, qg, kg, cos, sin)


def _attn_a_kernel(qt_ref, kc_ref, vct_ref, k_ref, vt_ref, o_ref,
                   sc_ref, pc_ref, s0_ref, s1_ref, p0_ref, p1_ref, acc_ref):
    tq = qt_ref.shape[3]
    ts = vt_ref.shape[4]
    tk = s0_ref.shape[0]
    sub = tk // ts
    n_chunks = vt_ref.shape[2] // sub
    assert n_chunks % 2 == 0 and n_chunks >= 4
    qt = jnp.concatenate([qt_ref[0, g] for g in range(GROUP)], axis=1)

    def k_chunk(j):
        return k_ref[0, 0, pl.ds(pl.multiple_of(j * tk, tk), tk), :]

    def scores(kblk, s_ref):
        s = jnp.dot(kblk, qt, preferred_element_type=F32)
        s_ref[...] = s
        return jnp.max(s, axis=0, keepdims=True)

    def softmax(s_ref, p_ref, mc, m):
        m_new = jnp.maximum(m, mc)
        p_ref[...] = jnp.exp2(s_ref[...] - m_new).astype(BF16)
        return m_new, jnp.exp2(m - m_new)

    def accumulate(vt_blocks, p_ref, alpha):
        pv = None
        for u, vtblk in enumerate(vt_blocks):
            part = jnp.dot(vtblk, p_ref[u * vtblk.shape[1]:(u + 1) * vtblk.shape[1], :],
                           preferred_element_type=F32)
            pv = part if pv is None else pv + part
        acc_ref[...] = alpha * acc_ref[...] + pv

    def v_chunk(j):
        return [vt_ref[0, 0, j * sub + u] for u in range(sub)]

    acc_ref[...] = jnp.zeros(acc_ref.shape, F32)
    m = jnp.full((1, GROUP * tq), NEG_BIG, F32)
    mc_ctx = scores(kc_ref[0, 0], sc_ref)
    mc0 = scores(k_chunk(0), s0_ref)
    m, alpha_ctx = softmax(sc_ref, pc_ref, mc_ctx, m)
    mc1 = scores(k_chunk(1), s1_ref)
    m, alpha0 = softmax(s0_ref, p0_ref, mc0, m)
    accumulate([vct_ref[0, 0, 0]], pc_ref, alpha_ctx)

    def pair(jj, carry):
        m, alpha_prev, mc = carry
        j = 2 * jj + 1
        mc_next = scores(k_chunk(j + 1), s0_ref)
        m, alpha = softmax(s1_ref, p1_ref, mc, m)
        accumulate(v_chunk(j - 1), p0_ref, alpha_prev)
        mc_next2 = scores(k_chunk(j + 2), s1_ref)
        m, alpha2 = softmax(s0_ref, p0_ref, mc_next, m)
        accumulate(v_chunk(j), p1_ref, alpha)
        return m, alpha2, mc_next2

    m, alpha_prev, mc = lax.fori_loop(0, (n_chunks - 2) // 2, pair, (m, alpha0, mc1))
    m, alpha = softmax(s1_ref, p1_ref, mc, m)
    accumulate(v_chunk(n_chunks - 2), p0_ref, alpha_prev)
    accumulate(v_chunk(n_chunks - 1), p1_ref, alpha)

    acc = acc_ref[...]
    o = (acc[:HEAD_DIM] / acc[HEAD_DIM:HEAD_DIM + 1]).T
    for g in range(GROUP):
        o_ref[0, :, g * HEAD_DIM:(g + 1) * HEAD_DIM] = o[g * tq:(g + 1) * tq].astype(o_ref.dtype)


def _attn_a(qt, kc, vct, k, vt, tq, tk):
    b, _, _, l = qt.shape
    c_len = kc.shape[2]
    n_stored, ts = vt.shape[2], vt.shape[4]
    assert tk % ts == 0 and l % tk == 0
    return pl.pallas_call(
        _attn_a_kernel,
        grid=(b, A_KV_HEADS, l // tq),
        in_specs=[
            pl.BlockSpec((1, GROUP, HEAD_DIM, tq), lambda bi, kv, i: (bi, kv, 0, i)),
            pl.BlockSpec((1, 1, c_len, HEAD_DIM), lambda bi, kv, i: (bi, kv, 0, 0)),
            pl.BlockSpec((1, 1, 1, V_ROWS, c_len), lambda bi, kv, i: (bi, kv, 0, 0, 0)),
            pl.BlockSpec((1, 1, l, HEAD_DIM), lambda bi, kv, i: (bi, kv, 0, 0)),
            pl.BlockSpec((1, 1, n_stored, V_ROWS, ts), lambda bi, kv, i: (bi, kv, 0, 0, 0)),
        ],
        out_specs=pl.BlockSpec((1, tq, GROUP * HEAD_DIM), lambda bi, kv, i: (bi, i, kv)),
        out_shape=jax.ShapeDtypeStruct((b, l, A_HEADS * HEAD_DIM), BF16),
        scratch_shapes=[
            pltpu.VMEM((c_len, GROUP * tq), F32),
            pltpu.VMEM((c_len, GROUP * tq), BF16),
            pltpu.VMEM((tk, GROUP * tq), F32),
            pltpu.VMEM((tk, GROUP * tq), F32),
            pltpu.VMEM((tk, GROUP * tq), BF16),
            pltpu.VMEM((tk, GROUP * tq), BF16),
            pltpu.VMEM((V_ROWS, GROUP * tq), F32),
        ],
        compiler_params=_cparams(("parallel", "parallel", "arbitrary")),
        name="attn_global",
    )(qt, kc, vct, k, vt)


def _attn_b_kernel(sink_ref, q_ref, kp_ref, km_ref, kn_ref, vp_ref, vm_ref, vn_ref,
                   kc_ref, vc_ref, o_ref, *, seq_len):
    kv = pl.program_id(1)
    i = pl.program_id(2)
    tq = q_ref.shape[2]
    sub = tq // BLOCK
    band = 3 * BLOCK
    kcat = jnp.concatenate([kp_ref[0, 0], km_ref[0, 0], kn_ref[0, 0]], axis=0)
    vcat = jnp.concatenate([vp_ref[0, 0], vm_ref[0, 0], vn_ref[0, 0]], axis=0)
    kc = kc_ref[0, 0]
    vc = vc_ref[0, 0]
    rows = GROUP * BLOCK
    nt = (((1,), (1,)), ((), ()))

    r = lax.broadcasted_iota(jnp.int32, (rows, band), 0) % BLOCK
    c = lax.broadcasted_iota(jnp.int32, (rows, band), 1)
    in_window = jnp.abs(c - BLOCK - r) <= WINDOW
    head_row = lax.broadcasted_iota(jnp.int32, (rows, 1), 0) // BLOCK
    sink2 = jnp.zeros((rows, 1), F32)
    for g in range(GROUP):
        sink2 = jnp.where(head_row == g, sink_ref[kv * GROUP + g] * LOG2E, sink2)

    for j in range(sub):
        q2 = jnp.concatenate([q_ref[0, g, j * BLOCK:(j + 1) * BLOCK, :] for g in range(GROUP)], axis=0)
        kband = kcat[j * BLOCK:j * BLOCK + band]
        vband = vcat[j * BLOCK:j * BLOCK + band]
        s_loc = lax.dot_general(q2, kband, nt, preferred_element_type=F32)
        s_ctx = lax.dot_general(q2, kc, nt, preferred_element_type=F32)
        kpos = (i * sub + (j - 1)) * BLOCK + c
        valid = in_window & (kpos >= 0) & (kpos < seq_len)
        s_loc = jnp.where(valid, s_loc, NEG_BIG)
        m = jnp.maximum(jnp.maximum(jnp.max(s_loc, axis=-1, keepdims=True),
                                    jnp.max(s_ctx, axis=-1, keepdims=True)), sink2)
        p_loc = jnp.exp2(s_loc - m)
        p_ctx = jnp.exp2(s_ctx - m)
        denom = (jnp.sum(p_loc, axis=-1, keepdims=True) + jnp.sum(p_ctx, axis=-1, keepdims=True)
                 + jnp.exp2(sink2 - m))
        o = (jnp.dot(p_ctx.astype(BF16), vc, preferred_element_type=F32)
             + jnp.dot(p_loc.astype(BF16), vband, preferred_element_type=F32)) / denom
        for g in range(GROUP):
            o_ref[0, j * BLOCK:(j + 1) * BLOCK, g * HEAD_DIM:(g + 1) * HEAD_DIM] = (
                o[g * BLOCK:(g + 1) * BLOCK].astype(o_ref.dtype))


def _attn_b(sink, q, k, v, kc, vc, tq):
    b, _, l, _ = q.shape
    c_len = kc.shape[2]
    sub = tq // BLOCK
    nb = l // BLOCK

    def main(bi, kv, i):
        return (bi, kv, i, 0)

    def prev(bi, kv, i):
        return (bi, kv, jnp.maximum(i * sub - 1, 0), 0)

    def nxt(bi, kv, i):
        return (bi, kv, jnp.minimum((i + 1) * sub, nb - 1), 0)

    def ctx_map(bi, kv, i):
        return (bi, kv, 0, 0)

    edge = lambda fn: pl.BlockSpec((1, 1, BLOCK, HEAD_DIM), fn)
    return pl.pallas_call(
        functools.partial(_attn_b_kernel, seq_len=l),
        grid=(b, B_KV_HEADS, l // tq),
        in_specs=[
            pl.BlockSpec(memory_space=pltpu.SMEM),
            pl.BlockSpec((1, GROUP, tq, HEAD_DIM), main),
            edge(prev), pl.BlockSpec((1, 1, tq, HEAD_DIM), main), edge(nxt),
            edge(prev), pl.BlockSpec((1, 1, tq, HEAD_DIM), main), edge(nxt),
            pl.BlockSpec((1, 1, c_len, HEAD_DIM), ctx_map),
            pl.BlockSpec((1, 1, c_len, HEAD_DIM), ctx_map),
        ],
        out_specs=pl.BlockSpec((1, tq, GROUP * HEAD_DIM), lambda bi, kv, i: (bi, i, kv)),
        out_shape=jax.ShapeDtypeStruct((b, l, B_HEADS * HEAD_DIM), BF16),
        compiler_params=_cparams(("parallel", "parallel", "arbitrary")),
        name="attn_window",
    )(sink, q, k, k, k, v, v, v, kc, vc)


def _residual_and_next(y, x, mod_ref, ng_ref):
    x1 = x + mod_ref[0, 2:3, :] * _rms(y, ng_ref[1:2, :])
    h2 = _rms(x1, ng_ref[2:3, :]) * (1.0 + mod_ref[0, 4:5, :]) + mod_ref[0, 3:4, :]
    return x1, h2


def _post_attn_kernel(oa_ref, ob_ref, w_ref, x_ref, mod_ref, ng_ref, x1_ref, h2_ref):
    na = oa_ref.shape[2]
    y = (jnp.dot(oa_ref[0], w_ref[0:na, :], preferred_element_type=F32)
         + jnp.dot(ob_ref[0], w_ref[na:, :], preferred_element_type=F32))
    x1, h2 = _residual_and_next(y, x_ref[0], mod_ref, ng_ref)
    x1_ref[0] = x1
    h2_ref[0] = h2.astype(BF16)


def _post_attn(oa, ob, w_out, x, mod6, ng, tm):
    b, l, d = x.shape
    na, nb_ = oa.shape[2], ob.shape[2]
    return pl.pallas_call(
        _post_attn_kernel,
        grid=(b, l // tm),
        in_specs=[
            pl.BlockSpec((1, tm, na), lambda bi, i: (bi, i, 0)),
            pl.BlockSpec((1, tm, nb_), lambda bi, i: (bi, i, 0)),
            pl.BlockSpec((na + nb_, d), lambda bi, i: (0, 0)),
            pl.BlockSpec((1, tm, d), lambda bi, i: (bi, i, 0)),
            pl.BlockSpec((1, N_MOD, d), lambda bi, i: (bi, 0, 0)),
            pl.BlockSpec((4, d), lambda bi, i: (0, 0)),
        ],
        out_specs=[
            pl.BlockSpec((1, tm, d), lambda bi, i: (bi, i, 0)),
            pl.BlockSpec((1, tm, d), lambda bi, i: (bi, i, 0)),
        ],
        out_shape=[jax.ShapeDtypeStruct((b, l, d), F32), jax.ShapeDtypeStruct((b, l, d), BF16)],
        compiler_params=_cparams(("parallel", "parallel")),
        name="attn_out_proj",
    )(oa, ob, w_out, x, mod6, ng)


def _ffn_kernel(hp_ref, hm_ref, hn_ref, x_ref, mod_ref, ng_ref, wu_ref, cw_ref, cb_ref, wd_ref, o_ref,
                hcat_ref, ug0_ref, uv0_ref, ug1_ref, uv1_ref, act_ref):
    i = pl.program_id(1)
    last = pl.num_programs(1) - 1
    tm = hm_ref.shape[1]
    f = wd_ref.shape[0]
    fc = ug0_ref.shape[1]
    n_chunks = f // fc
    u_bufs = ((ug0_ref, uv0_ref), (ug1_ref, uv1_ref))
    hp = hp_ref[0]
    hn = hn_ref[0]
    hcat_ref[0:HALO, :] = jnp.where(i > 0, hp, jnp.zeros_like(hp))
    hcat_ref[HALO:HALO + tm, :] = hm_ref[0]
    hcat_ref[HALO + tm:, :] = jnp.where(i < last, hn, jnp.zeros_like(hn))

    def cols(c, half):
        return pl.ds(pl.multiple_of(half * f + c * fc, fc), fc)

    def conv(u_ref, c, half):
        cw = cw_ref[:, cols(c, half)]
        return (u_ref[pl.ds(HALO - 1, tm), :] * cw[0:1, :] + u_ref[pl.ds(HALO, tm), :] * cw[1:2, :]
                + u_ref[pl.ds(HALO + 1, tm), :] * cw[2:3, :] + cb_ref[:, cols(c, half)])

    def up(c, bufs):
        hcat = hcat_ref[...]
        for half in range(2):
            bufs[half][...] = jnp.dot(hcat, wu_ref[:, cols(c, half)], preferred_element_type=F32)

    def mid(c, bufs):
        gate = conv(bufs[0], c, 0)
        val = conv(bufs[1], c, 1)
        act_ref[:, cols(c, 0)] = (gate / (1.0 + jnp.exp(-gate)) * val).astype(BF16)

    def stage(c, parity, do_up):
        if do_up:
            up(c + 1, u_bufs[1 - parity])
        mid(c, u_bufs[parity])

    up(0, u_bufs[0])
    n_pairs = (n_chunks - 1) // 2

    def pair(jj, carry):
        stage(2 * jj, 0, True)
        stage(2 * jj + 1, 1, True)
        return carry

    lax.fori_loop(0, n_pairs, pair, 0)
    for c in range(2 * n_pairs, n_chunks):
        stage(c, c % 2, c + 1 < n_chunks)
    y = jnp.dot(act_ref[...], wd_ref[...], preferred_element_type=F32)
    o_ref[0] = x_ref[0] + mod_ref[0, 5:6, :] * _rms(y, ng_ref[3:4, :])


def _ffn(h2, x1, mod6, ng, w_up, conv_w, conv_b, w_down, tm, fc):
    b, l, d = x1.shape
    f = w_down.shape[0]
    assert f % fc == 0
    per = tm // HALO
    n_halo = l // HALO
    resident = lambda shape: pl.BlockSpec(shape, lambda bi, i: (0,) * len(shape),
                                          pipeline_mode=pl.Buffered(1))
    return pl.pallas_call(
        _ffn_kernel,
        grid=(b, l // tm),
        in_specs=[
            pl.BlockSpec((1, HALO, d), lambda bi, i: (bi, jnp.maximum(i * per - 1, 0), 0)),
            pl.BlockSpec((1, tm, d), lambda bi, i: (bi, i, 0)),
            pl.BlockSpec((1, HALO, d), lambda bi, i: (bi, jnp.minimum((i + 1) * per, n_halo - 1), 0)),
            pl.BlockSpec((1, tm, d), lambda bi, i: (bi, i, 0)),
            pl.BlockSpec((1, N_MOD, d), lambda bi, i: (bi, 0, 0)),
            pl.BlockSpec((4, d), lambda bi, i: (0, 0)),
            resident((d, 2 * f)), resident((3, 2 * f)), resident((1, 2 * f)), resident((f, d)),
        ],
        out_specs=pl.BlockSpec((1, tm, d), lambda bi, i: (bi, i, 0)),
        out_shape=jax.ShapeDtypeStruct((b, l, d), F32),
        scratch_shapes=[
            pltpu.VMEM((tm + 2 * HALO, d), BF16),
            pltpu.VMEM((tm + 2 * HALO, fc), F32),
            pltpu.VMEM((tm + 2 * HALO, fc), F32),
            pltpu.VMEM((tm + 2 * HALO, fc), F32),
            pltpu.VMEM((tm + 2 * HALO, fc), F32),
            pltpu.VMEM((tm, f), BF16),
        ],
        compiler_params=_cparams(("parallel", "arbitrary")),
        name="conv_ffn",
    )(h2, h2, h2, x1, mod6, ng, w_up.astype(BF16), conv_w, conv_b.reshape(1, 2 * f), w_down.astype(BF16))


def _dft_tables(l, d):
    n1, n2, cg = DFT_N1, l // DFT_N1, d // FNET_GROUPS
    a = np.arange(cg)
    ang_c = 2.0 * np.pi * np.outer(a, a) / cg
    csc = np.concatenate([np.cos(ang_c), np.sin(ang_c)], axis=1) / math.sqrt(cg)
    a1 = np.arange(n1)
    ang1 = 2.0 * np.pi * np.outer(a1, a1) / n1
    c1, s1 = np.cos(ang1), np.sin(ang1)
    m1 = np.block([[c1, -s1], [s1, c1]])
    a2 = np.arange(n2)
    ang2 = 2.0 * np.pi * np.outer(a2, a2) / n2
    c2 = np.cos(ang2) / math.sqrt(l)
    s2 = np.sin(ang2) / math.sqrt(l)
    ang_t = 2.0 * np.pi * np.outer(a1, a2) / l
    f32 = lambda t: jnp.asarray(t, F32)
    return (f32(csc).astype(BF16), f32(m1).astype(BF16), f32(c2).astype(BF16), f32(s2).astype(BF16),
            f32(np.cos(ang_t)), f32(np.sin(ang_t)))


def _fourier1_kernel(x_ref, mod_ref, ng_ref, csc_ref, m1_ref, twc_ref, tws_ref, zr_ref, zi_ref):
    d = mod_ref.shape[2]
    cg = csc_ref.shape[0]
    n1 = x_ref.shape[1]
    t2s = x_ref.shape[2]
    csc = csc_ref[...]
    m1 = m1_ref[...]
    for t in range(t2s):
        xb = x_ref[0, :, t, :]
        h = (_rms(xb, ng_ref[0:1, :]) * (1.0 + mod_ref[0, 1:2, :]) + mod_ref[0, 0:1, :]).astype(BF16)
        ab = [jnp.dot(h[:, g * cg:(g + 1) * cg], csc, preferred_element_type=F32)
              for g in range(d // cg)]
        a = jnp.concatenate([t_[:, :cg] for t_ in ab], axis=1)
        bm = jnp.concatenate([t_[:, cg:] for t_ in ab], axis=1)
        y = jnp.dot(m1, jnp.concatenate([a, bm], axis=0).astype(BF16), preferred_element_type=F32)
        yr = y[:n1]
        yin = y[n1:]
        cc = jnp.tile(twc_ref[:, t * HEAD_DIM:(t + 1) * HEAD_DIM], (1, d // HEAD_DIM))
        ss = jnp.tile(tws_ref[:, t * HEAD_DIM:(t + 1) * HEAD_DIM], (1, d // HEAD_DIM))
        zr_ref[0, :, t * d:(t + 1) * d] = (yr * cc - yin * ss).astype(BF16)
        zi_ref[0, :, t * d:(t + 1) * d] = (-(yin * cc) - yr * ss).astype(BF16)


def _fourier1(x, mod6, ng, csc, m1, twc, tws, t2s):
    b, l, d = x.shape
    n1, n2 = DFT_N1, l // DFT_N1
    xv = x.reshape(b, n1, n2, d)
    lanes = HEAD_DIM
    twc_x = jnp.repeat(twc, lanes, axis=1)
    tws_x = jnp.repeat(tws, lanes, axis=1)
    cg = csc.shape[0]
    return pl.pallas_call(
        _fourier1_kernel,
        grid=(b, n2 // t2s),
        in_specs=[
            pl.BlockSpec((1, n1, t2s, d), lambda bi, j: (bi, 0, j, 0)),
            pl.BlockSpec((1, N_MOD, d), lambda bi, j: (bi, 0, 0)),
            pl.BlockSpec((4, d), lambda bi, j: (0, 0)),
            pl.BlockSpec((cg, 2 * cg), lambda bi, j: (0, 0)),
            pl.BlockSpec((2 * n1, 2 * n1), lambda bi, j: (0, 0)),
            pl.BlockSpec((n1, t2s * lanes), lambda bi, j: (0, j)),
            pl.BlockSpec((n1, t2s * lanes), lambda bi, j: (0, j)),
        ],
        out_specs=[
            pl.BlockSpec((1, n1, t2s * d), lambda bi, j: (bi, 0, j)),
            pl.BlockSpec((1, n1, t2s * d), lambda bi, j: (bi, 0, j)),
        ],
        out_shape=[jax.ShapeDtypeStruct((b, n1, n2 * d), BF16)] * 2,
        compiler_params=_cparams(("parallel", "parallel")),
        name="fnet_stage1",
    )(xv, mod6, ng, csc, m1, twc_x, tws_x)


def _fourier2_kernel(zr_ref, zi_ref, x_ref, c2_ref, s2_ref, w_ref, mod_ref, ng_ref, x1_ref, h2_ref):
    k1s = zr_ref.shape[1]
    n2 = zr_ref.shape[2]
    d = zr_ref.shape[3]
    c2 = c2_ref[...]
    s2 = s2_ref[...]
    fs = []
    for k in range(k1s):
        f = (jnp.dot(c2, zr_ref[0, k], preferred_element_type=F32)
             + jnp.dot(s2, zi_ref[0, k], preferred_element_type=F32))
        fs.append(f.astype(BF16))
    y = jnp.dot(jnp.concatenate(fs, axis=0), w_ref[...], preferred_element_type=F32)
    for k in range(k1s):
        x1, h2 = _residual_and_next(y[k * n2:(k + 1) * n2], x_ref[0, :, k, :], mod_ref, ng_ref)
        x1_ref[0, :, k, :] = x1
        h2_ref[0, :, k * d:(k + 1) * d] = h2.astype(BF16)


def _fourier2(zr, zi, x, c2, s2, w_f, mod6, ng, k1s):
    b, l, d = x.shape
    n1, n2 = DFT_N1, l // DFT_N1
    zr4 = zr.reshape(b, n1, n2, d)
    zi4 = zi.reshape(b, n1, n2, d)
    xv = x.reshape(b, n2, n1, d)
    x1, h2 = pl.pallas_call(
        _fourier2_kernel,
        grid=(b, n1 // k1s),
        in_specs=[
            pl.BlockSpec((1, k1s, n2, d), lambda bi, j: (bi, j, 0, 0)),
            pl.BlockSpec((1, k1s, n2, d), lambda bi, j: (bi, j, 0, 0)),
            pl.BlockSpec((1, n2, k1s, d), lambda bi, j: (bi, 0, j, 0)),
            pl.BlockSpec((n2, n2), lambda bi, j: (0, 0)),
            pl.BlockSpec((n2, n2), lambda bi, j: (0, 0)),
            pl.BlockSpec((d, d), lambda bi, j: (0, 0)),
            pl.BlockSpec((1, N_MOD, d), lambda bi, j: (bi, 0, 0)),
            pl.BlockSpec((4, d), lambda bi, j: (0, 0)),
        ],
        out_specs=[
            pl.BlockSpec((1, n2, k1s, d), lambda bi, j: (bi, 0, j, 0)),
            pl.BlockSpec((1, n2, k1s * d), lambda bi, j: (bi, 0, j)),
        ],
        out_shape=[jax.ShapeDtypeStruct((b, n2, n1, d), F32), jax.ShapeDtypeStruct((b, n2, n1 * d), BF16)],
        compiler_params=_cparams(("parallel", "parallel")),
        name="fnet_stage2",
    )(zr4, zi4, xv, c2, s2, w_f, mod6, ng)
    return x1.reshape(b, l, d), h2.reshape(b, l, d)


def _rope_tables(l):
    n_freq = HEAD_DIM // 4
    t = jnp.arange(l)
    inv = ROPE_THETA ** (-jnp.arange(n_freq, dtype=F32) / n_freq)
    ang_row = (t // GRID_W).astype(F32)[:, None] * inv
    ang_col = (t % GRID_W).astype(F32)[:, None] * inv
    cos = jnp.concatenate([jnp.cos(ang_row)] * 2 + [jnp.cos(ang_col)] * 2, axis=1)
    sin = jnp.concatenate([-jnp.sin(ang_row), jnp.sin(ang_row), -jnp.sin(ang_col), jnp.sin(ang_col)], axis=1)
    return cos, sin


def kernel(x, c, ctx, c_ctx, mod_w, mod_b, norm_g, attn_w_in, attn_w_out, q_norm_g, k_norm_g, sink,
           fourier_w_out, ffn_w_up, ffn_conv_w, ffn_conv_b, ffn_w_down):
    b, l, d = x.shape
    c_len = ctx.shape[1]
    assert mod_w.shape[0] == 2 and b + 1 <= COND_ROWS
    assert l % max(QKV_TM, WIN_TQ, FFN_TM, POST_TM, DFT_N1 * F1_T2) == 0

    cond = jnp.zeros((COND_ROWS, d), F32).at[:b].set(c).at[b].set(c_ctx)
    mods = _mods(cond, mod_w, mod_b)
    mod_l0 = mods[0].reshape(COND_ROWS, N_MOD, d)
    mod_l1 = mods[1].reshape(COND_ROWS, N_MOD, d)

    cos, sin = _rope_tables(l)
    w_in = attn_w_in[0].astype(BF16)
    qg = q_norm_g[0].reshape(1, HEAD_DIM)
    kg = k_norm_g[0].reshape(1, HEAD_DIM)
    qat, ka, vat, qb, kb, vb = _qkv(x, mod_l0, lambda bi: bi, norm_g[0], w_in, qg, kg, cos, sin, QKV_TM)
    ones = jnp.ones((c_len, HEAD_DIM), F32)
    _, kac, vact, _, kbc, vbc = _qkv(ctx, mod_l0, lambda bi: b, norm_g[0], w_in, qg, kg,
                                     ones, jnp.zeros_like(ones), c_len)
    oa = _attn_a(qat, kac, vact, ka, vat, ATT_TQ, ATT_TK)
    ob = _attn_b(sink[0], qb, kb, vb, kbc, vbc, WIN_TQ)
    x1, h2 = _post_attn(oa, ob, attn_w_out[0].astype(BF16), x, mod_l0, norm_g[0], POST_TM)
    x2 = _ffn(h2, x1, mod_l0, norm_g[0], ffn_w_up[0], ffn_conv_w[0], ffn_conv_b[0], ffn_w_down[0],
              FFN_TM, FFN_FC)

    csc, m1, c2, s2, twc, tws = _dft_tables(l, d)
    zr, zi = _fourier1(x2, mod_l1, norm_g[1], csc, m1, twc, tws, F1_T2)
    x3, h2 = _fourier2(zr, zi, x2, c2, s2, fourier_w_out[0].astype(BF16), mod_l1, norm_g[1], F2_K1)
    return _ffn(h2, x3, mod_l1, norm_g[1], ffn_w_up[1], ffn_conv_w[1], ffn_conv_b[1], ffn_w_down[1],
                FFN_TM, FFN_FC)
```

```python
import functools
import math

import numpy as np
import jax
import jax.numpy as jnp
from jax import lax
from jax.experimental import pallas as pl
from jax.experimental.pallas import tpu as pltpu

F32 = jnp.float32
BF16 = jnp.bfloat16

HEAD_DIM = 128
GRID_W = 64
A_HEADS = 4
A_KV_HEADS = 2
B_HEADS = 4
B_KV_HEADS = 2
GROUP = A_HEADS // A_KV_HEADS
BLOCK = 128
WINDOW = 128
ROPE_THETA = 10000.0
FNET_GROUPS = 4
N_MOD = 6
EPS = 1e-6
LOG2E = math.log2(math.e)
Q_SCALE = HEAD_DIM ** -0.5 * LOG2E
NEG_BIG = -1e30

DFT_N1 = 128
COND_ROWS = 8
VMEM_LIMIT = 56 * 1024 * 1024

QKV_TM = 512
ATT_TQ = 1024
ATT_TK = 512
ATT_UNROLL = 2
V_ROWS = HEAD_DIM + 16
WIN_TQ = 512
POST_TM = 512
FFN_TM = 1024
FFN_FC = 256
FFN_UNROLL = 10
HALO = 16
F1_T2 = 4
F2_K1 = 8


def _cparams(sem):
    return pltpu.CompilerParams(dimension_semantics=sem, vmem_limit_bytes=VMEM_LIMIT)


def _rms(x, g):
    return x * lax.rsqrt(jnp.mean(x * x, axis=-1, keepdims=True) + EPS) * g


def _mod_kernel(cond_ref, w_ref, b_ref, o_ref):
    c = cond_ref[...]
    s = c / (1.0 + jnp.exp(-c))
    o_ref[0] = jnp.dot(s, w_ref[0], preferred_element_type=F32) + b_ref[0]


def _mods(cond, mod_w, mod_b):
    depth, d, n = mod_w.shape
    tn = n // 4
    return pl.pallas_call(
        _mod_kernel,
        grid=(depth, n // tn),
        in_specs=[
            pl.BlockSpec((COND_ROWS, d), lambda l, j: (0, 0)),
            pl.BlockSpec((1, d, tn), lambda l, j: (l, 0, j)),
            pl.BlockSpec((1, 1, tn), lambda l, j: (l, 0, j)),
        ],
        out_specs=pl.BlockSpec((1, COND_ROWS, tn), lambda l, j: (l, 0, j)),
        out_shape=jax.ShapeDtypeStruct((depth, COND_ROWS, n), F32),
        compiler_params=_cparams(("arbitrary", "arbitrary")),
        name="adaln_mods",
    )(cond, mod_w, mod_b.reshape(depth, 1, n))


def _qkv_kernel(x_ref, mod_ref, ng_ref, w_ref, qg_ref, kg_ref, cr_ref, sr_ref, cc_ref, sc_ref,
                qat_ref, ka_ref, vat_ref, qb_ref, kb_ref, vb_ref):
    x = x_ref[0]
    tm = x.shape[0]
    h = _rms(x, ng_ref[0:1, :]) * (1.0 + mod_ref[0, 1:2, :]) + mod_ref[0, 0:1, :]
    proj = jnp.dot(h.astype(BF16), w_ref[...], preferred_element_type=F32)

    def table(row_ref, col_ref):
        col_part = col_ref[...]
        return jnp.concatenate([jnp.broadcast_to(row_ref[r:r + 1, :], (GRID_W, HEAD_DIM)) + col_part
                                for r in range(tm // GRID_W)], axis=0)

    cos = table(cr_ref, cc_ref)
    sin = table(sr_ref, sc_ref)
    lane = lax.broadcasted_iota(jnp.int32, (tm, HEAD_DIM), 1)
    low_half = (lane % (HEAD_DIM // 2)) < (HEAD_DIM // 4)

    def rope(t):
        partner = jnp.where(low_half,
                            pltpu.roll(t, HEAD_DIM - HEAD_DIM // 4, 1),
                            pltpu.roll(t, HEAD_DIM // 4, 1))
        return t * cos + partner * sin

    def head(j):
        return proj[:, j * HEAD_DIM:(j + 1) * HEAD_DIM]

    qg = qg_ref[...]
    kg = kg_ref[...]
    col = 0
    for hh in range(A_HEADS):
        q = rope(_rms(head(col + hh), qg)) * Q_SCALE
        qat_ref[0, hh] = q.T.astype(BF16)
    col += A_HEADS
    for hh in range(B_HEADS):
        qb_ref[0, hh] = (rope(head(col + hh)) * Q_SCALE).astype(BF16)
    col += B_HEADS
    for hh in range(A_KV_HEADS):
        ka_ref[0, hh] = rope(_rms(head(col + hh), kg)).astype(BF16)
    col += A_KV_HEADS
    for hh in range(A_KV_HEADS):
        vat_ref[0, hh, 0, 0:HEAD_DIM, :] = head(col + hh).T.astype(BF16)
        vat_ref[0, hh, 0, HEAD_DIM:, :] = jnp.ones((V_ROWS - HEAD_DIM, tm), BF16)
    col += A_KV_HEADS
    for hh in range(B_KV_HEADS):
        kb_ref[0, hh] = rope(head(col + hh)).astype(BF16)
    col += B_KV_HEADS
    for hh in range(B_KV_HEADS):
        vb_ref[0, hh] = head(col + hh).astype(BF16)


def _qkv(x, mod6, mod_row, ng, w_in, qg, kg, rope, tm):
    b, l, d = x.shape
    n = w_in.shape[1]
    nt = l // tm
    assert tm % GRID_W == 0
    rt = max(tm // GRID_W, 8)
    return pl.pallas_call(
        _qkv_kernel,
        grid=(b, nt),
        in_specs=[
            pl.BlockSpec((1, tm, d), lambda bi, i: (bi, i, 0)),
            pl.BlockSpec((1, N_MOD, d), lambda bi, i: (mod_row(bi), 0, 0)),
            pl.BlockSpec((4, d), lambda bi, i: (0, 0)),
            pl.BlockSpec((d, n), lambda bi, i: (0, 0)),
            pl.BlockSpec((1, HEAD_DIM), lambda bi, i: (0, 0)),
            pl.BlockSpec((1, HEAD_DIM), lambda bi, i: (0, 0)),
            pl.BlockSpec((rt, HEAD_DIM), lambda bi, i: (i, 0)),
            pl.BlockSpec((rt, HEAD_DIM), lambda bi, i: (i, 0)),
            pl.BlockSpec((GRID_W, HEAD_DIM), lambda bi, i: (0, 0)),
            pl.BlockSpec((GRID_W, HEAD_DIM), lambda bi, i: (0, 0)),
        ],
        out_specs=[
            pl.BlockSpec((1, A_HEADS, HEAD_DIM, tm), lambda bi, i: (bi, 0, 0, i)),
            pl.BlockSpec((1, A_KV_HEADS, tm, HEAD_DIM), lambda bi, i: (bi, 0, i, 0)),
            pl.BlockSpec((1, A_KV_HEADS, 1, V_ROWS, tm), lambda bi, i: (bi, 0, i, 0, 0)),
            pl.BlockSpec((1, B_HEADS, tm, HEAD_DIM), lambda bi, i: (bi, 0, i, 0)),
            pl.BlockSpec((1, B_KV_HEADS, tm, HEAD_DIM), lambda bi, i: (bi, 0, i, 0)),
            pl.BlockSpec((1, B_KV_HEADS, tm, HEAD_DIM), lambda bi, i: (bi, 0, i, 0)),
        ],
        out_shape=[
            jax.ShapeDtypeStruct((b, A_HEADS, HEAD_DIM, l), BF16),
            jax.ShapeDtypeStruct((b, A_KV_HEADS, l, HEAD_DIM), BF16),
            jax.ShapeDtypeStruct((b, A_KV_HEADS, nt, V_ROWS, tm), BF16),
            jax.ShapeDtypeStruct((b, B_HEADS, l, HEAD_DIM), BF16),
            jax.ShapeDtypeStruct((b, B_KV_HEADS, l, HEAD_DIM), BF16),
            jax.ShapeDtypeStruct((b, B_KV_HEADS, l, HEAD_DIM), BF16),
        ],
        compiler_params=_cparams(("parallel", "parallel")),
        name="qkv_proj",
    )(x, mod6, ng, w_in, qg, kg, *rope)


def _attn_a_kernel(qt_ref, kc_ref, vct_ref, k_ref, vt_ref, o_ref,
                   sc_ref, pc_ref, s0_ref, s1_ref, p0_ref, p1_ref, acc_ref):
    tq = qt_ref.shape[3]
    ts = vt_ref.shape[4]
    tk = s0_ref.shape[0]
    sub = tk // ts
    n_chunks = vt_ref.shape[2] // sub
    assert n_chunks >= 2 and ATT_UNROLL % 2 == 0
    qt = jnp.concatenate([qt_ref[0, g] for g in range(GROUP)], axis=1)

    def k_chunk(j):
        return k_ref[0, 0, pl.ds(pl.multiple_of(j * tk, tk), tk), :]

    def scores(kblk, s_ref):
        s = jnp.dot(kblk, qt, preferred_element_type=F32)
        s_ref[...] = s
        return jnp.max(s, axis=0, keepdims=True)

    def softmax(s_ref, p_ref, mc, m):
        m_new = jnp.maximum(m, mc)
        p_ref[...] = jnp.exp2(s_ref[...] - m_new).astype(BF16)
        return m_new, jnp.exp2(m - m_new)

    def accumulate(vt_blocks, p_ref, alpha):
        pv = None
        for u, vtblk in enumerate(vt_blocks):
            part = jnp.dot(vtblk, p_ref[u * vtblk.shape[1]:(u + 1) * vtblk.shape[1], :],
                           preferred_element_type=F32)
            pv = part if pv is None else pv + part
        acc_ref[...] = alpha * acc_ref[...] + pv

    def v_chunk(j):
        return [vt_ref[0, 0, j * sub + u] for u in range(sub)]

    acc_ref[...] = jnp.zeros(acc_ref.shape, F32)
    m = jnp.full((1, GROUP * tq), NEG_BIG, F32)
    mc_ctx = scores(kc_ref[0, 0], sc_ref)
    mc0 = scores(k_chunk(0), s0_ref)
    m, alpha_ctx = softmax(sc_ref, pc_ref, mc_ctx, m)
    mc1 = scores(k_chunk(1), s1_ref)
    m, alpha0 = softmax(s0_ref, p0_ref, mc0, m)
    accumulate([vct_ref[0, 0, 0]], pc_ref, alpha_ctx)

    s_bufs = (s0_ref, s1_ref)
    p_bufs = (p0_ref, p1_ref)

    def stage(c, parity, carry, has_next):
        m, alpha_prev, mc = carry
        mc_next = scores(k_chunk(c + 1), s_bufs[1 - parity]) if has_next else None
        m, alpha = softmax(s_bufs[parity], p_bufs[parity], mc, m)
        accumulate(v_chunk(c - 1), p_bufs[1 - parity], alpha_prev)
        return m, alpha, mc_next

    unroll = ATT_UNROLL
    n_groups = (n_chunks - 2) // unroll

    def group(g, carry):
        for u in range(unroll):
            carry = stage(1 + g * unroll + u, (1 + u) % 2, carry, True)
        return carry

    carry = lax.fori_loop(0, n_groups, group, (m, alpha0, mc1))
    for c in range(1 + n_groups * unroll, n_chunks):
        carry = stage(c, c % 2, carry, c + 1 < n_chunks)
    accumulate(v_chunk(n_chunks - 1), p_bufs[(n_chunks - 1) % 2], carry[1])

    acc = acc_ref[...]
    o = (acc[:HEAD_DIM] / acc[HEAD_DIM:HEAD_DIM + 1]).T
    for g in range(GROUP):
        o_ref[0, :, g * HEAD_DIM:(g + 1) * HEAD_DIM] = o[g * tq:(g + 1) * tq].astype(o_ref.dtype)


def _attn_a(qt, kc, vct, k, vt, tq, tk):
    b, _, _, l = qt.shape
    c_len = kc.shape[2]
    n_stored, ts = vt.shape[2], vt.shape[4]
    assert tk % ts == 0 and l % tk == 0
    return pl.pallas_call(
        _attn_a_kernel,
        grid=(b, A_KV_HEADS, l // tq),
        in_specs=[
            pl.BlockSpec((1, GROUP, HEAD_DIM, tq), lambda bi, kv, i: (bi, kv, 0, i)),
            pl.BlockSpec((1, 1, c_len, HEAD_DIM), lambda bi, kv, i: (bi, kv, 0, 0)),
            pl.BlockSpec((1, 1, 1, V_ROWS, c_len), lambda bi, kv, i: (bi, kv, 0, 0, 0)),
            pl.BlockSpec((1, 1, l, HEAD_DIM), lambda bi, kv, i: (bi, kv, 0, 0)),
            pl.BlockSpec((1, 1, n_stored, V_ROWS, ts), lambda bi, kv, i: (bi, kv, 0, 0, 0)),
        ],
        out_specs=pl.BlockSpec((1, tq, GROUP * HEAD_DIM), lambda bi, kv, i: (bi, i, kv)),
        out_shape=jax.ShapeDtypeStruct((b, l, A_HEADS * HEAD_DIM), BF16),
        scratch_shapes=[
            pltpu.VMEM((c_len, GROUP * tq), F32),
            pltpu.VMEM((c_len, GROUP * tq), BF16),
            pltpu.VMEM((tk, GROUP * tq), F32),
            pltpu.VMEM((tk, GROUP * tq), F32),
            pltpu.VMEM((tk, GROUP * tq), BF16),
            pltpu.VMEM((tk, GROUP * tq), BF16),
            pltpu.VMEM((V_ROWS, GROUP * tq), F32),
        ],
        compiler_params=_cparams(("parallel", "parallel", "arbitrary")),
        name="attn_global",
    )(qt, kc, vct, k, vt)


def _attn_b_kernel(sink_ref, q_ref, kp_ref, km_ref, kn_ref, vp_ref, vm_ref, vn_ref,
                   kc_ref, vc_ref, o_ref, *, seq_len):
    kv = pl.program_id(1)
    i = pl.program_id(2)
    tq = q_ref.shape[2]
    sub = tq // BLOCK
    band = 3 * BLOCK
    kcat = jnp.concatenate([kp_ref[0, 0], km_ref[0, 0], kn_ref[0, 0]], axis=0)
    vcat = jnp.concatenate([vp_ref[0, 0], vm_ref[0, 0], vn_ref[0, 0]], axis=0)
    kc = kc_ref[0, 0]
    vc = vc_ref[0, 0]
    rows = GROUP * BLOCK
    nt = (((1,), (1,)), ((), ()))

    r = lax.broadcasted_iota(jnp.int32, (rows, band), 0) % BLOCK
    c = lax.broadcasted_iota(jnp.int32, (rows, band), 1)
    in_window = jnp.abs(c - BLOCK - r) <= WINDOW
    head_row = lax.broadcasted_iota(jnp.int32, (rows, 1), 0) // BLOCK
    sink2 = jnp.zeros((rows, 1), F32)
    for g in range(GROUP):
        sink2 = jnp.where(head_row == g, sink_ref[kv * GROUP + g] * LOG2E, sink2)

    for j in range(sub):
        q2 = jnp.concatenate([q_ref[0, g, j * BLOCK:(j + 1) * BLOCK, :] for g in range(GROUP)], axis=0)
        kband = kcat[j * BLOCK:j * BLOCK + band]
        vband = vcat[j * BLOCK:j * BLOCK + band]
        s_loc = lax.dot_general(q2, kband, nt, preferred_element_type=F32)
        s_ctx = lax.dot_general(q2, kc, nt, preferred_element_type=F32)
        kpos = (i * sub + (j - 1)) * BLOCK + c
        valid = in_window & (kpos >= 0) & (kpos < seq_len)
        s_loc = jnp.where(valid, s_loc, NEG_BIG)
        m = jnp.maximum(jnp.maximum(jnp.max(s_loc, axis=-1, keepdims=True),
                                    jnp.max(s_ctx, axis=-1, keepdims=True)), sink2)
        p_loc = jnp.exp2(s_loc - m)
        p_ctx = jnp.exp2(s_ctx - m)
        denom = (jnp.sum(p_loc, axis=-1, keepdims=True) + jnp.sum(p_ctx, axis=-1, keepdims=True)
                 + jnp.exp2(sink2 - m))
        o = (jnp.dot(p_ctx.astype(BF16), vc, preferred_element_type=F32)
             + jnp.dot(p_loc.astype(BF16), vband, preferred_element_type=F32)) / denom
        for g in range(GROUP):
            o_ref[0, j * BLOCK:(j + 1) * BLOCK, g * HEAD_DIM:(g + 1) * HEAD_DIM] = (
                o[g * BLOCK:(g + 1) * BLOCK].astype(o_ref.dtype))


def _attn_b(sink, q, k, v, kc, vc, tq):
    b, _, l, _ = q.shape
    c_len = kc.shape[2]
    sub = tq // BLOCK
    nb = l // BLOCK

    def main(bi, kv, i):
        return (bi, kv, i, 0)

    def prev(bi, kv, i):
        return (bi, kv, jnp.maximum(i * sub - 1, 0), 0)

    def nxt(bi, kv, i):
        return (bi, kv, jnp.minimum((i + 1) * sub, nb - 1), 0)

    def ctx_map(bi, kv, i):
        return (bi, kv, 0, 0)

    edge = lambda fn: pl.BlockSpec((1, 1, BLOCK, HEAD_DIM), fn)
    return pl.pallas_call(
        functools.partial(_attn_b_kernel, seq_len=l),
        grid=(b, B_KV_HEADS, l // tq),
        in_specs=[
            pl.BlockSpec(memory_space=pltpu.SMEM),
            pl.BlockSpec((1, GROUP, tq, HEAD_DIM), main),
            edge(prev), pl.BlockSpec((1, 1, tq, HEAD_DIM), main), edge(nxt),
            edge(prev), pl.BlockSpec((1, 1, tq, HEAD_DIM), main), edge(nxt),
            pl.BlockSpec((1, 1, c_len, HEAD_DIM), ctx_map),
            pl.BlockSpec((1, 1, c_len, HEAD_DIM), ctx_map),
        ],
        out_specs=pl.BlockSpec((1, tq, GROUP * HEAD_DIM), lambda bi, kv, i: (bi, i, kv)),
        out_shape=jax.ShapeDtypeStruct((b, l, B_HEADS * HEAD_DIM), BF16),
        compiler_params=_cparams(("parallel", "parallel", "arbitrary")),
        name="attn_window",
    )(sink, q, k, k, k, v, v, v, kc, vc)


def _residual_and_next(y, x, mod_ref, ng_ref):
    x1 = x + mod_ref[0, 2:3, :] * _rms(y, ng_ref[1:2, :])
    h2 = _rms(x1, ng_ref[2:3, :]) * (1.0 + mod_ref[0, 4:5, :]) + mod_ref[0, 3:4, :]
    return x1, h2


def _post_attn_kernel(oa_ref, ob_ref, w_ref, x_ref, mod_ref, ng_ref, x1_ref, h2_ref):
    na = oa_ref.shape[2]
    y = (jnp.dot(oa_ref[0], w_ref[0:na, :], preferred_element_type=F32)
         + jnp.dot(ob_ref[0], w_ref[na:, :], preferred_element_type=F32))
    x1, h2 = _residual_and_next(y, x_ref[0], mod_ref, ng_ref)
    x1_ref[0] = x1
    h2_ref[0] = h2.astype(BF16)


def _post_attn(oa, ob, w_out, x, mod6, ng, tm):
    b, l, d = x.shape
    na, nb_ = oa.shape[2], ob.shape[2]
    return pl.pallas_call(
        _post_attn_kernel,
        grid=(b, l // tm),
        in_specs=[
            pl.BlockSpec((1, tm, na), lambda bi, i: (bi, i, 0)),
            pl.BlockSpec((1, tm, nb_), lambda bi, i: (bi, i, 0)),
            pl.BlockSpec((na + nb_, d), lambda bi, i: (0, 0)),
            pl.BlockSpec((1, tm, d), lambda bi, i: (bi, i, 0)),
            pl.BlockSpec((1, N_MOD, d), lambda bi, i: (bi, 0, 0)),
            pl.BlockSpec((4, d), lambda bi, i: (0, 0)),
        ],
        out_specs=[
            pl.BlockSpec((1, tm, d), lambda bi, i: (bi, i, 0)),
            pl.BlockSpec((1, tm, d), lambda bi, i: (bi, i, 0)),
        ],
        out_shape=[jax.ShapeDtypeStruct((b, l, d), F32), jax.ShapeDtypeStruct((b, l, d), BF16)],
        compiler_params=_cparams(("parallel", "parallel")),
        name="attn_out_proj",
    )(oa, ob, w_out, x, mod6, ng)


def _ffn_kernel(hp_ref, hm_ref, hn_ref, x_ref, mod_ref, ng_ref, wu_ref, cw_ref, cb_ref, wd_ref, o_ref,
                hcat_ref, ug0_ref, uv0_ref, ug1_ref, uv1_ref, act_ref):
    i = pl.program_id(1)
    last = pl.num_programs(1) - 1
    tm = hm_ref.shape[1]
    f = wd_ref.shape[0]
    fc = ug0_ref.shape[1]
    n_chunks = f // fc
    u_bufs = ((ug0_ref, uv0_ref), (ug1_ref, uv1_ref))
    hp = hp_ref[0]
    hn = hn_ref[0]
    hcat_ref[0:HALO, :] = jnp.where(i > 0, hp, jnp.zeros_like(hp))
    hcat_ref[HALO:HALO + tm, :] = hm_ref[0]
    hcat_ref[HALO + tm:, :] = jnp.where(i < last, hn, jnp.zeros_like(hn))

    def cols(c, half):
        return pl.ds(pl.multiple_of(half * f + c * fc, fc), fc)

    def conv(u_ref, c, half):
        cw = cw_ref[:, cols(c, half)]
        u = u_ref[...]
        rows = u.shape[0]
        prev = pltpu.roll(u, 1, 0)[HALO:HALO + tm]
        nxt = pltpu.roll(u, rows - 1, 0)[HALO:HALO + tm]
        return (prev * cw[0:1, :] + u[HALO:HALO + tm] * cw[1:2, :] + nxt * cw[2:3, :]
                + cb_ref[:, cols(c, half)])

    def up(c, bufs):
        hcat = hcat_ref[...]
        for half in range(2):
            bufs[half][...] = jnp.dot(hcat, wu_ref[:, cols(c, half)], preferred_element_type=F32)

    def mid(c, bufs):
        gate = conv(bufs[0], c, 0)
        val = conv(bufs[1], c, 1)
        act_ref[:, cols(c, 0)] = (gate / (1.0 + jnp.exp(-gate)) * val).astype(BF16)

    def stage(c, parity, do_up):
        if do_up:
            up(c + 1, u_bufs[1 - parity])
        mid(c, u_bufs[parity])

    up(0, u_bufs[0])
    unroll = FFN_UNROLL
    n_groups = (n_chunks - 1) // unroll

    def group(g, carry):
        for u in range(unroll):
            stage(g * unroll + u, u % 2, True)
        return carry

    lax.fori_loop(0, n_groups, group, 0)
    for c in range(n_groups * unroll, n_chunks):
        stage(c, c % 2, c + 1 < n_chunks)
    y = jnp.dot(act_ref[...], wd_ref[...], preferred_element_type=F32)
    o_ref[0] = x_ref[0] + mod_ref[0, 5:6, :] * _rms(y, ng_ref[3:4, :])


def _ffn(h2, x1, mod6, ng, w_up, conv_w, conv_b, w_down, tm, fc):
    b, l, d = x1.shape
    f = w_down.shape[0]
    assert f % fc == 0
    per = tm // HALO
    n_halo = l // HALO
    resident = lambda shape: pl.BlockSpec(shape, lambda bi, i: (0,) * len(shape),
                                          pipeline_mode=pl.Buffered(1))
    return pl.pallas_call(
        _ffn_kernel,
        grid=(b, l // tm),
        in_specs=[
            pl.BlockSpec((1, HALO, d), lambda bi, i: (bi, jnp.maximum(i * per - 1, 0), 0)),
            pl.BlockSpec((1, tm, d), lambda bi, i: (bi, i, 0)),
            pl.BlockSpec((1, HALO, d), lambda bi, i: (bi, jnp.minimum((i + 1) * per, n_halo - 1), 0)),
            pl.BlockSpec((1, tm, d), lambda bi, i: (bi, i, 0)),
            pl.BlockSpec((1, N_MOD, d), lambda bi, i: (bi, 0, 0)),
            pl.BlockSpec((4, d), lambda bi, i: (0, 0)),
            resident((d, 2 * f)), resident((3, 2 * f)), resident((1, 2 * f)), resident((f, d)),
        ],
        out_specs=pl.BlockSpec((1, tm, d), lambda bi, i: (bi, i, 0)),
        out_shape=jax.ShapeDtypeStruct((b, l, d), F32),
        scratch_shapes=[
            pltpu.VMEM((tm + 2 * HALO, d), BF16),
            pltpu.VMEM((tm + 2 * HALO, fc), F32),
            pltpu.VMEM((tm + 2 * HALO, fc), F32),
            pltpu.VMEM((tm + 2 * HALO, fc), F32),
            pltpu.VMEM((tm + 2 * HALO, fc), F32),
            pltpu.VMEM((tm, f), BF16),
        ],
        compiler_params=_cparams(("parallel", "arbitrary")),
        name="conv_ffn",
    )(h2, h2, h2, x1, mod6, ng, w_up.astype(BF16), conv_w, conv_b.reshape(1, 2 * f), w_down.astype(BF16))


def _dft_tables(l, d):
    n1, n2, cg = DFT_N1, l // DFT_N1, d // FNET_GROUPS
    a = np.arange(cg)
    ang_c = 2.0 * np.pi * np.outer(a, a) / cg
    csc = np.concatenate([np.cos(ang_c), np.sin(ang_c)], axis=1) / math.sqrt(cg)
    a1 = np.arange(n1)
    ang1 = 2.0 * np.pi * np.outer(a1, a1) / n1
    c1, s1 = np.cos(ang1), np.sin(ang1)
    m1 = np.block([[c1, -s1], [s1, c1]])
    a2 = np.arange(n2)
    ang2 = 2.0 * np.pi * np.outer(a2, a2) / n2
    c2 = np.cos(ang2) / math.sqrt(l)
    s2 = np.sin(ang2) / math.sqrt(l)
    ang_t = 2.0 * np.pi * np.outer(a1, a2) / l
    f32 = lambda t: jnp.asarray(t, F32)
    return (f32(csc).astype(BF16), f32(m1).astype(BF16), f32(c2).astype(BF16), f32(s2).astype(BF16),
            f32(np.cos(ang_t)), f32(np.sin(ang_t)))


def _fourier1_kernel(x_ref, mod_ref, ng_ref, csc_ref, m1_ref, twc_ref, tws_ref, zr_ref, zi_ref):
    d = mod_ref.shape[2]
    cg = csc_ref.shape[0]
    n1 = x_ref.shape[1]
    t2s = x_ref.shape[2] // d
    csc = csc_ref[...]
    m1 = m1_ref[...]
    for t in range(t2s):
        xb = x_ref[0, :, t * d:(t + 1) * d]
        h = (_rms(xb, ng_ref[0:1, :]) * (1.0 + mod_ref[0, 1:2, :]) + mod_ref[0, 0:1, :]).astype(BF16)
        ab = [jnp.dot(h[:, g * cg:(g + 1) * cg], csc, preferred_element_type=F32)
              for g in range(d // cg)]
        a = jnp.concatenate([t_[:, :cg] for t_ in ab], axis=1)
        bm = jnp.concatenate([t_[:, cg:] for t_ in ab], axis=1)
        y = jnp.dot(m1, jnp.concatenate([a, bm], axis=0).astype(BF16), preferred_element_type=F32)
        yr = y[:n1]
        yin = y[n1:]
        cc = jnp.tile(twc_ref[:, t * HEAD_DIM:(t + 1) * HEAD_DIM], (1, d // HEAD_DIM))
        ss = jnp.tile(tws_ref[:, t * HEAD_DIM:(t + 1) * HEAD_DIM], (1, d // HEAD_DIM))
        zr_ref[0, :, t * d:(t + 1) * d] = (yr * cc - yin * ss).astype(BF16)
        zi_ref[0, :, t * d:(t + 1) * d] = (-(yin * cc) - yr * ss).astype(BF16)


def _fourier1(x, mod6, ng, csc, m1, twc, tws, t2s):
    b, l, d = x.shape
    n1, n2 = DFT_N1, l // DFT_N1
    xv = x.reshape(b, n1, n2 * d)
    lanes = HEAD_DIM
    twc_x = jnp.repeat(twc, lanes, axis=1)
    tws_x = jnp.repeat(tws, lanes, axis=1)
    cg = csc.shape[0]
    return pl.pallas_call(
        _fourier1_kernel,
        grid=(b, n2 // t2s),
        in_specs=[
            pl.BlockSpec((1, n1, t2s * d), lambda bi, j: (bi, 0, j)),
            pl.BlockSpec((1, N_MOD, d), lambda bi, j: (bi, 0, 0)),
            pl.BlockSpec((4, d), lambda bi, j: (0, 0)),
            pl.BlockSpec((cg, 2 * cg), lambda bi, j: (0, 0)),
            pl.BlockSpec((2 * n1, 2 * n1), lambda bi, j: (0, 0)),
            pl.BlockSpec((n1, t2s * lanes), lambda bi, j: (0, j)),
            pl.BlockSpec((n1, t2s * lanes), lambda bi, j: (0, j)),
        ],
        out_specs=[
            pl.BlockSpec((1, n1, t2s * d), lambda bi, j: (bi, 0, j)),
            pl.BlockSpec((1, n1, t2s * d), lambda bi, j: (bi, 0, j)),
        ],
        out_shape=[jax.ShapeDtypeStruct((b, n1, n2 * d), BF16)] * 2,
        compiler_params=_cparams(("parallel", "parallel")),
        name="fnet_stage1",
    )(xv, mod6, ng, csc, m1, twc_x, tws_x)


def _fourier2_kernel(zr_ref, zi_ref, x_ref, c2_ref, s2_ref, w_ref, mod_ref, ng_ref, x1_ref, h2_ref):
    k1s = zr_ref.shape[1]
    n2 = zr_ref.shape[2]
    d = zr_ref.shape[3]
    c2 = c2_ref[...]
    s2 = s2_ref[...]
    fs = []
    for k in range(k1s):
        f = (jnp.dot(c2, zr_ref[0, k], preferred_element_type=F32)
             + jnp.dot(s2, zi_ref[0, k], preferred_element_type=F32))
        fs.append(f.astype(BF16))
    y = jnp.dot(jnp.concatenate(fs, axis=0), w_ref[...], preferred_element_type=F32)
    for k in range(k1s):
        x1, h2 = _residual_and_next(y[k * n2:(k + 1) * n2], x_ref[0, :, k * d:(k + 1) * d], mod_ref, ng_ref)
        x1_ref[0, :, k * d:(k + 1) * d] = x1
        h2_ref[0, :, k * d:(k + 1) * d] = h2.astype(BF16)


def _fourier2(zr, zi, x, c2, s2, w_f, mod6, ng, k1s):
    b, l, d = x.shape
    n1, n2 = DFT_N1, l // DFT_N1
    zr4 = zr.reshape(b, n1, n2, d)
    zi4 = zi.reshape(b, n1, n2, d)
    xv = x.reshape(b, n2, n1 * d)
    x1, h2 = pl.pallas_call(
        _fourier2_kernel,
        grid=(b, n1 // k1s),
        in_specs=[
            pl.BlockSpec((1, k1s, n2, d), lambda bi, j: (bi, j, 0, 0)),
            pl.BlockSpec((1, k1s, n2, d), lambda bi, j: (bi, j, 0, 0)),
            pl.BlockSpec((1, n2, k1s * d), lambda bi, j: (bi, 0, j)),
            pl.BlockSpec((n2, n2), lambda bi, j: (0, 0)),
            pl.BlockSpec((n2, n2), lambda bi, j: (0, 0)),
            pl.BlockSpec((d, d), lambda bi, j: (0, 0)),
            pl.BlockSpec((1, N_MOD, d), lambda bi, j: (bi, 0, 0)),
            pl.BlockSpec((4, d), lambda bi, j: (0, 0)),
        ],
        out_specs=[
            pl.BlockSpec((1, n2, k1s * d), lambda bi, j: (bi, 0, j)),
            pl.BlockSpec((1, n2, k1s * d), lambda bi, j: (bi, 0, j)),
        ],
        out_shape=[jax.ShapeDtypeStruct((b, n2, n1 * d), F32), jax.ShapeDtypeStruct((b, n2, n1 * d), BF16)],
        compiler_params=_cparams(("parallel", "parallel")),
        name="fnet_stage2",
    )(zr4, zi4, xv, c2, s2, w_f, mod6, ng)
    return x1.reshape(b, l, d), h2.reshape(b, l, d)


def _rope_tables(l):
    n_freq = HEAD_DIM // 4
    inv = ROPE_THETA ** (-jnp.arange(n_freq, dtype=F32) / n_freq)
    ang_row = jnp.arange(l // GRID_W).astype(F32)[:, None] * inv
    ang_col = jnp.arange(GRID_W).astype(F32)[:, None] * inv
    zr = jnp.zeros_like(ang_row)
    zc = jnp.zeros_like(ang_col)
    cos_r = jnp.concatenate([jnp.cos(ang_row), jnp.cos(ang_row), zr, zr], axis=1)
    sin_r = jnp.concatenate([-jnp.sin(ang_row), jnp.sin(ang_row), zr, zr], axis=1)
    cos_c = jnp.concatenate([zc, zc, jnp.cos(ang_col), jnp.cos(ang_col)], axis=1)
    sin_c = jnp.concatenate([zc, zc, -jnp.sin(ang_col), jnp.sin(ang_col)], axis=1)
    return cos_r, sin_r, cos_c, sin_c


def kernel(x, c, ctx, c_ctx, mod_w, mod_b, norm_g, attn_w_in, attn_w_out, q_norm_g, k_norm_g, sink,
           fourier_w_out, ffn_w_up, ffn_conv_w, ffn_conv_b, ffn_w_down):
    b, l, d = x.shape
    c_len = ctx.shape[1]
    assert mod_w.shape[0] == 2 and b + 1 <= COND_ROWS
    assert l % max(QKV_TM, WIN_TQ, FFN_TM, POST_TM, DFT_N1 * F1_T2) == 0

    cond = jnp.zeros((COND_ROWS, d), F32).at[:b].set(c).at[b].set(c_ctx)
    mods = _mods(cond, mod_w, mod_b)
    mod_l0 = mods[0].reshape(COND_ROWS, N_MOD, d)
    mod_l1 = mods[1].reshape(COND_ROWS, N_MOD, d)

    rope = _rope_tables(l)
    w_in = attn_w_in[0].astype(BF16)
    qg = q_norm_g[0].reshape(1, HEAD_DIM)
    kg = k_norm_g[0].reshape(1, HEAD_DIM)
    qat, ka, vat, qb, kb, vb = _qkv(x, mod_l0, lambda bi: bi, norm_g[0], w_in, qg, kg, rope, QKV_TM)
    zeros_r = jnp.zeros((8, HEAD_DIM), F32)
    zeros_c = jnp.zeros((GRID_W, HEAD_DIM), F32)
    no_rope = (jnp.ones_like(zeros_r), zeros_r, zeros_c, zeros_c)
    _, kac, vact, _, kbc, vbc = _qkv(ctx, mod_l0, lambda bi: b, norm_g[0], w_in, qg, kg, no_rope, c_len)
    oa = _attn_a(qat, kac, vact, ka, vat, ATT_TQ, ATT_TK)
    ob = _attn_b(sink[0], qb, kb, vb, kbc, vbc, WIN_TQ)
    x1, h2 = _post_attn(oa, ob, attn_w_out[0].astype(BF16), x, mod_l0, norm_g[0], POST_TM)
    x2 = _ffn(h2, x1, mod_l0, norm_g[0], ffn_w_up[0], ffn_conv_w[0], ffn_conv_b[0], ffn_w_down[0],
              FFN_TM, FFN_FC)

    csc, m1, c2, s2, twc, tws = _dft_tables(l, d)
    zr, zi = _fourier1(x2, mod_l1, norm_g[1], csc, m1, twc, tws, F1_T2)
    x3, h2 = _fourier2(zr, zi, x2, c2, s2, fourier_w_out[0].astype(BF16), mod_l1, norm_g[1], F2_K1)
    return _ffn(h2, x3, mod_l1, norm_g[1], ffn_w_up[1], ffn_conv_w[1], ffn_conv_b[1], ffn_w_down[1],
                FFN_TM, FFN_FC)
```

```python
import functools
import math

import numpy as np
import jax
import jax.numpy as jnp
from jax import lax
from jax.experimental import pallas as pl
from jax.experimental.pallas import tpu as pltpu

F32 = jnp.float32
BF16 = jnp.bfloat16

HEAD_DIM = 128
GRID_W = 64
A_HEADS = 4
A_KV_HEADS = 2
B_HEADS = 4
B_KV_HEADS = 2
GROUP = A_HEADS // A_KV_HEADS
BLOCK = 128
WINDOW = 128
ROPE_THETA = 10000.0
FNET_GROUPS = 4
N_MOD = 6
EPS = 1e-6
LOG2E = math.log2(math.e)
Q_SCALE = HEAD_DIM ** -0.5 * LOG2E
NEG_BIG = -1e30

DFT_N1 = 128
COND_ROWS = 8
VMEM_LIMIT = 56 * 1024 * 1024

QKV_TM = 512
QKV_SUB = 4
ATT_TQ = 1024
ATT_TK = 512
ATT_UNROLL = 2
V_ROWS = HEAD_DIM + 16
WIN_TQ = 1024
POST_TM = 1024
POST_SUB = 8
FFN_TM = 1024
FFN_FC = 256
FFN_UNROLL = 10
HALO = 16
F1_T2 = 4
F2_K1 = 8


def _cparams(sem):
    return pltpu.CompilerParams(dimension_semantics=sem, vmem_limit_bytes=VMEM_LIMIT)


def _rms(x, g):
    return x * lax.rsqrt(jnp.mean(x * x, axis=-1, keepdims=True) + EPS) * g


def _mod_kernel(cond_ref, w_ref, b_ref, o_ref):
    c = cond_ref[...]
    s = c / (1.0 + jnp.exp(-c))
    o_ref[0] = jnp.dot(s, w_ref[0], preferred_element_type=F32) + b_ref[0]


def _mods(cond, mod_w, mod_b):
    depth, d, n = mod_w.shape
    tn = n // 4
    return pl.pallas_call(
        _mod_kernel,
        grid=(depth, n // tn),
        in_specs=[
            pl.BlockSpec((COND_ROWS, d), lambda l, j: (0, 0)),
            pl.BlockSpec((1, d, tn), lambda l, j: (l, 0, j)),
            pl.BlockSpec((1, 1, tn), lambda l, j: (l, 0, j)),
        ],
        out_specs=pl.BlockSpec((1, COND_ROWS, tn), lambda l, j: (l, 0, j)),
        out_shape=jax.ShapeDtypeStruct((depth, COND_ROWS, n), F32),
        compiler_params=_cparams(("arbitrary", "arbitrary")),
        name="adaln_mods",
    )(cond, mod_w, mod_b.reshape(depth, 1, n))


def _qkv_kernel(x_ref, mod_ref, ng_ref, w_ref, qg_ref, kg_ref, cr_ref, sr_ref, cc_ref, sc_ref,
                qa_ref, ka_ref, vat_ref, qb_ref, kb_ref, vb_ref):
    tm = x_ref.shape[1]
    rows = tm // QKV_SUB
    assert rows % GRID_W == 0
    projs = []
    for r in range(QKV_SUB):
        x = x_ref[0, r * rows:(r + 1) * rows, :]
        h = _rms(x, ng_ref[0:1, :]) * (1.0 + mod_ref[0, 1:2, :]) + mod_ref[0, 0:1, :]
        projs.append(jnp.dot(h.astype(BF16), w_ref[...], preferred_element_type=F32))

    lane = lax.broadcasted_iota(jnp.int32, (rows, HEAD_DIM), 1)
    low_half = (lane % (HEAD_DIM // 2)) < (HEAD_DIM // 4)
    qg = qg_ref[...]
    kg = kg_ref[...]
    col_cos = cc_ref[...]
    col_sin = sc_ref[...]

    for r, proj in enumerate(projs):
        sl = slice(r * rows, (r + 1) * rows)
        g0 = r * (rows // GRID_W)

        def table(row_ref, col_part):
            return jnp.concatenate([jnp.broadcast_to(row_ref[g0 + i:g0 + i + 1, :], (GRID_W, HEAD_DIM)) + col_part
                                    for i in range(rows // GRID_W)], axis=0)

        cos = table(cr_ref, col_cos)
        sin = table(sr_ref, col_sin)

        def rope(t):
            partner = jnp.where(low_half,
                                pltpu.roll(t, HEAD_DIM - HEAD_DIM // 4, 1),
                                pltpu.roll(t, HEAD_DIM // 4, 1))
            return t * cos + partner * sin

        def head(j):
            return proj[:, j * HEAD_DIM:(j + 1) * HEAD_DIM]

        col = 0
        for hh in range(A_HEADS):
            qa_ref[0, hh, sl, :] = (rope(_rms(head(col + hh), qg)) * Q_SCALE).astype(BF16)
        col += A_HEADS
        for hh in range(B_HEADS):
            qb_ref[0, hh, sl, :] = (rope(head(col + hh)) * Q_SCALE).astype(BF16)
        col += B_HEADS
        for hh in range(A_KV_HEADS):
            ka_ref[0, hh, sl, :] = rope(_rms(head(col + hh), kg)).astype(BF16)
        col += A_KV_HEADS
        for hh in range(A_KV_HEADS):
            vat_ref[0, hh, 0, 0:HEAD_DIM, sl] = head(col + hh).T.astype(BF16)
            vat_ref[0, hh, 0, HEAD_DIM:, sl] = jnp.ones((V_ROWS - HEAD_DIM, rows), BF16)
        col += A_KV_HEADS
        for hh in range(B_KV_HEADS):
            kb_ref[0, hh, sl, :] = rope(head(col + hh)).astype(BF16)
        col += B_KV_HEADS
        for hh in range(B_KV_HEADS):
            vb_ref[0, hh, sl, :] = head(col + hh).astype(BF16)


def _qkv(x, mod6, mod_row, ng, w_in, qg, kg, rope, tm):
    b, l, d = x.shape
    n = w_in.shape[1]
    nt = l // tm
    assert tm % (GRID_W * QKV_SUB) == 0
    rt = max(tm // GRID_W, 8)
    return pl.pallas_call(
        _qkv_kernel,
        grid=(b, nt),
        in_specs=[
            pl.BlockSpec((1, tm, d), lambda bi, i: (bi, i, 0)),
            pl.BlockSpec((1, N_MOD, d), lambda bi, i: (mod_row(bi), 0, 0)),
            pl.BlockSpec((4, d), lambda bi, i: (0, 0)),
            pl.BlockSpec((d, n), lambda bi, i: (0, 0)),
            pl.BlockSpec((1, HEAD_DIM), lambda bi, i: (0, 0)),
            pl.BlockSpec((1, HEAD_DIM), lambda bi, i: (0, 0)),
            pl.BlockSpec((rt, HEAD_DIM), lambda bi, i: (i, 0)),
            pl.BlockSpec((rt, HEAD_DIM), lambda bi, i: (i, 0)),
            pl.BlockSpec((GRID_W, HEAD_DIM), lambda bi, i: (0, 0)),
            pl.BlockSpec((GRID_W, HEAD_DIM), lambda bi, i: (0, 0)),
        ],
        out_specs=[
            pl.BlockSpec((1, A_HEADS, tm, HEAD_DIM), lambda bi, i: (bi, 0, i, 0)),
            pl.BlockSpec((1, A_KV_HEADS, tm, HEAD_DIM), lambda bi, i: (bi, 0, i, 0)),
            pl.BlockSpec((1, A_KV_HEADS, 1, V_ROWS, tm), lambda bi, i: (bi, 0, i, 0, 0)),
            pl.BlockSpec((1, B_HEADS, tm, HEAD_DIM), lambda bi, i: (bi, 0, i, 0)),
            pl.BlockSpec((1, B_KV_HEADS, tm, HEAD_DIM), lambda bi, i: (bi, 0, i, 0)),
            pl.BlockSpec((1, B_KV_HEADS, tm, HEAD_DIM), lambda bi, i: (bi, 0, i, 0)),
        ],
        out_shape=[
            jax.ShapeDtypeStruct((b, A_HEADS, l, HEAD_DIM), BF16),
            jax.ShapeDtypeStruct((b, A_KV_HEADS, l, HEAD_DIM), BF16),
            jax.ShapeDtypeStruct((b, A_KV_HEADS, nt, V_ROWS, tm), BF16),
            jax.ShapeDtypeStruct((b, B_HEADS, l, HEAD_DIM), BF16),
            jax.ShapeDtypeStruct((b, B_KV_HEADS, l, HEAD_DIM), BF16),
            jax.ShapeDtypeStruct((b, B_KV_HEADS, l, HEAD_DIM), BF16),
        ],
        compiler_params=_cparams(("parallel", "parallel")),
        name="qkv_proj",
    )(x, mod6, ng, w_in, qg, kg, *rope)


def _attn_a_kernel(q_ref, kc_ref, vct_ref, k_ref, vt_ref, o_ref,
                   sc_ref, pc_ref, s0_ref, s1_ref, p0_ref, p1_ref, acc_ref):
    tq = q_ref.shape[2]
    ts = vt_ref.shape[4]
    tk = s0_ref.shape[0]
    sub = tk // ts
    n_chunks = vt_ref.shape[2] // sub
    assert n_chunks >= 2 and ATT_UNROLL % 2 == 0
    q = jnp.concatenate([q_ref[0, g] for g in range(GROUP)], axis=0)

    def k_chunk(j):
        return k_ref[0, 0, pl.ds(pl.multiple_of(j * tk, tk), tk), :]

    def scores(kblk, s_ref):
        s = lax.dot_general(kblk, q, (((1,), (1,)), ((), ())), preferred_element_type=F32)
        s_ref[...] = s
        return jnp.max(s, axis=0, keepdims=True)

    def softmax(s_ref, p_ref, mc, m):
        m_new = jnp.maximum(m, mc)
        p_ref[...] = jnp.exp2(s_ref[...] - m_new).astype(BF16)
        return m_new, jnp.exp2(m - m_new)

    def accumulate(vt_blocks, p_ref, alpha):
        pv = None
        for u, vtblk in enumerate(vt_blocks):
            part = jnp.dot(vtblk, p_ref[u * vtblk.shape[1]:(u + 1) * vtblk.shape[1], :],
                           preferred_element_type=F32)
            pv = part if pv is None else pv + part
        acc_ref[...] = alpha * acc_ref[...] + pv

    def v_chunk(j):
        return [vt_ref[0, 0, j * sub + u] for u in range(sub)]

    acc_ref[...] = jnp.zeros(acc_ref.shape, F32)
    m = jnp.full((1, GROUP * tq), NEG_BIG, F32)
    mc_ctx = scores(kc_ref[0, 0], sc_ref)
    mc0 = scores(k_chunk(0), s0_ref)
    m, alpha_ctx = softmax(sc_ref, pc_ref, mc_ctx, m)
    mc1 = scores(k_chunk(1), s1_ref)
    m, alpha0 = softmax(s0_ref, p0_ref, mc0, m)
    accumulate([vct_ref[0, 0, 0]], pc_ref, alpha_ctx)

    s_bufs = (s0_ref, s1_ref)
    p_bufs = (p0_ref, p1_ref)

    def stage(c, parity, carry, has_next):
        m, alpha_prev, mc = carry
        mc_next = scores(k_chunk(c + 1), s_bufs[1 - parity]) if has_next else None
        m, alpha = softmax(s_bufs[parity], p_bufs[parity], mc, m)
        accumulate(v_chunk(c - 1), p_bufs[1 - parity], alpha_prev)
        return m, alpha, mc_next

    unroll = ATT_UNROLL
    n_groups = (n_chunks - 2) // unroll

    def group(g, carry):
        for u in range(unroll):
            carry = stage(1 + g * unroll + u, (1 + u) % 2, carry, True)
        return carry

    carry = lax.fori_loop(0, n_groups, group, (m, alpha0, mc1))
    for c in range(1 + n_groups * unroll, n_chunks):
        carry = stage(c, c % 2, carry, c + 1 < n_chunks)
    accumulate(v_chunk(n_chunks - 1), p_bufs[(n_chunks - 1) % 2], carry[1])

    acc = acc_ref[...]
    o = (acc[:HEAD_DIM] / acc[HEAD_DIM:HEAD_DIM + 1]).T
    for g in range(GROUP):
        o_ref[0, :, g * HEAD_DIM:(g + 1) * HEAD_DIM] = o[g * tq:(g + 1) * tq].astype(o_ref.dtype)


def _attn_a(q, kc, vct, k, vt, tq, tk):
    b, _, l, _ = q.shape
    c_len = kc.shape[2]
    n_stored, ts = vt.shape[2], vt.shape[4]
    assert tk % ts == 0 and l % tk == 0
    return pl.pallas_call(
        _attn_a_kernel,
        grid=(b, A_KV_HEADS, l // tq),
        in_specs=[
            pl.BlockSpec((1, GROUP, tq, HEAD_DIM), lambda bi, kv, i: (bi, kv, i, 0)),
            pl.BlockSpec((1, 1, c_len, HEAD_DIM), lambda bi, kv, i: (bi, kv, 0, 0)),
            pl.BlockSpec((1, 1, 1, V_ROWS, c_len), lambda bi, kv, i: (bi, kv, 0, 0, 0)),
            pl.BlockSpec((1, 1, l, HEAD_DIM), lambda bi, kv, i: (bi, kv, 0, 0)),
            pl.BlockSpec((1, 1, n_stored, V_ROWS, ts), lambda bi, kv, i: (bi, kv, 0, 0, 0)),
        ],
        out_specs=pl.BlockSpec((1, tq, GROUP * HEAD_DIM), lambda bi, kv, i: (bi, i, kv)),
        out_shape=jax.ShapeDtypeStruct((b, l, A_HEADS * HEAD_DIM), BF16),
        scratch_shapes=[
            pltpu.VMEM((c_len, GROUP * tq), F32),
            pltpu.VMEM((c_len, GROUP * tq), BF16),
            pltpu.VMEM((tk, GROUP * tq), F32),
            pltpu.VMEM((tk, GROUP * tq), F32),
            pltpu.VMEM((tk, GROUP * tq), BF16),
            pltpu.VMEM((tk, GROUP * tq), BF16),
            pltpu.VMEM((V_ROWS, GROUP * tq), F32),
        ],
        compiler_params=_cparams(("parallel", "parallel", "arbitrary")),
        name="attn_global",
    )(q, kc, vct, k, vt)


def _attn_b_kernel(sink_ref, q_ref, kp_ref, km_ref, kn_ref, vp_ref, vm_ref, vn_ref,
                   kc_ref, vc_ref, o_ref, *, seq_len):
    kv = pl.program_id(1)
    i = pl.program_id(2)
    tq = q_ref.shape[2]
    sub = tq // BLOCK
    band = 3 * BLOCK
    kcat = jnp.concatenate([kp_ref[0, 0], km_ref[0, 0], kn_ref[0, 0]], axis=0)
    vcat = jnp.concatenate([vp_ref[0, 0], vm_ref[0, 0], vn_ref[0, 0]], axis=0)
    kc = kc_ref[0, 0]
    vc = vc_ref[0, 0]
    rows = GROUP * BLOCK
    nt = (((1,), (1,)), ((), ()))

    r = lax.broadcasted_iota(jnp.int32, (rows, band), 0) % BLOCK
    c = lax.broadcasted_iota(jnp.int32, (rows, band), 1)
    in_window = jnp.abs(c - BLOCK - r) <= WINDOW
    head_row = lax.broadcasted_iota(jnp.int32, (rows, 1), 0) // BLOCK
    sink2 = jnp.zeros((rows, 1), F32)
    for g in range(GROUP):
        sink2 = jnp.where(head_row == g, sink_ref[kv * GROUP + g] * LOG2E, sink2)

    scores = []
    for j in range(sub):
        q2 = jnp.concatenate([q_ref[0, g, j * BLOCK:(j + 1) * BLOCK, :] for g in range(GROUP)], axis=0)
        s_loc = lax.dot_general(q2, kcat[j * BLOCK:j * BLOCK + band], nt, preferred_element_type=F32)
        s_ctx = lax.dot_general(q2, kc, nt, preferred_element_type=F32)
        scores.append((s_loc, s_ctx))
    probs = []
    for j, (s_loc, s_ctx) in enumerate(scores):
        kpos = (i * sub + (j - 1)) * BLOCK + c
        valid = in_window & (kpos >= 0) & (kpos < seq_len)
        s_loc = jnp.where(valid, s_loc, NEG_BIG)
        m = jnp.maximum(jnp.maximum(jnp.max(s_loc, axis=-1, keepdims=True),
                                    jnp.max(s_ctx, axis=-1, keepdims=True)), sink2)
        p_loc = jnp.exp2(s_loc - m)
        p_ctx = jnp.exp2(s_ctx - m)
        denom = (jnp.sum(p_loc, axis=-1, keepdims=True) + jnp.sum(p_ctx, axis=-1, keepdims=True)
                 + jnp.exp2(sink2 - m))
        probs.append((p_loc.astype(BF16), p_ctx.astype(BF16), denom))
    for j, (p_loc, p_ctx, denom) in enumerate(probs):
        o = (jnp.dot(p_ctx, vc, preferred_element_type=F32)
             + jnp.dot(p_loc, vcat[j * BLOCK:j * BLOCK + band], preferred_element_type=F32)) / denom
        for g in range(GROUP):
            o_ref[0, j * BLOCK:(j + 1) * BLOCK, g * HEAD_DIM:(g + 1) * HEAD_DIM] = (
                o[g * BLOCK:(g + 1) * BLOCK].astype(o_ref.dtype))


def _attn_b(sink, q, k, v, kc, vc, tq):
    b, _, l, _ = q.shape
    c_len = kc.shape[2]
    sub = tq // BLOCK
    nb = l // BLOCK

    def main(bi, kv, i):
        return (bi, kv, i, 0)

    def prev(bi, kv, i):
        return (bi, kv, jnp.maximum(i * sub - 1, 0), 0)

    def nxt(bi, kv, i):
        return (bi, kv, jnp.minimum((i + 1) * sub, nb - 1), 0)

    def ctx_map(bi, kv, i):
        return (bi, kv, 0, 0)

    edge = lambda fn: pl.BlockSpec((1, 1, BLOCK, HEAD_DIM), fn)
    return pl.pallas_call(
        functools.partial(_attn_b_kernel, seq_len=l),
        grid=(b, B_KV_HEADS, l // tq),
        in_specs=[
            pl.BlockSpec(memory_space=pltpu.SMEM),
            pl.BlockSpec((1, GROUP, tq, HEAD_DIM), main),
            edge(prev), pl.BlockSpec((1, 1, tq, HEAD_DIM), main), edge(nxt),
            edge(prev), pl.BlockSpec((1, 1, tq, HEAD_DIM), main), edge(nxt),
            pl.BlockSpec((1, 1, c_len, HEAD_DIM), ctx_map),
            pl.BlockSpec((1, 1, c_len, HEAD_DIM), ctx_map),
        ],
        out_specs=pl.BlockSpec((1, tq, GROUP * HEAD_DIM), lambda bi, kv, i: (bi, i, kv)),
        out_shape=jax.ShapeDtypeStruct((b, l, B_HEADS * HEAD_DIM), BF16),
        compiler_params=_cparams(("parallel", "parallel", "arbitrary")),
        name="attn_window",
    )(sink, q, k, k, k, v, v, v, kc, vc)


def _residual_and_next(y, x, mod_ref, ng_ref):
    x1 = x + mod_ref[0, 2:3, :] * _rms(y, ng_ref[1:2, :])
    h2 = _rms(x1, ng_ref[2:3, :]) * (1.0 + mod_ref[0, 4:5, :]) + mod_ref[0, 3:4, :]
    return x1, h2


def _post_attn_kernel(oa_ref, ob_ref, w_ref, x_ref, mod_ref, ng_ref, x1_ref, h2_ref):
    na = oa_ref.shape[2]
    rows = oa_ref.shape[1] // POST_SUB
    ys = []
    for r in range(POST_SUB):
        sl = slice(r * rows, (r + 1) * rows)
        ys.append(jnp.dot(oa_ref[0, sl, :], w_ref[0:na, :], preferred_element_type=F32)
                  + jnp.dot(ob_ref[0, sl, :], w_ref[na:, :], preferred_element_type=F32))
    for r, y in enumerate(ys):
        sl = slice(r * rows, (r + 1) * rows)
        x1, h2 = _residual_and_next(y, x_ref[0, sl, :], mod_ref, ng_ref)
        x1_ref[0, sl, :] = x1
        h2_ref[0, sl, :] = h2.astype(BF16)


def _post_attn(oa, ob, w_out, x, mod6, ng, tm):
    b, l, d = x.shape
    na, nb_ = oa.shape[2], ob.shape[2]
    return pl.pallas_call(
        _post_attn_kernel,
        grid=(b, l // tm),
        in_specs=[
            pl.BlockSpec((1, tm, na), lambda bi, i: (bi, i, 0)),
            pl.BlockSpec((1, tm, nb_), lambda bi, i: (bi, i, 0)),
            pl.BlockSpec((na + nb_, d), lambda bi, i: (0, 0)),
            pl.BlockSpec((1, tm, d), lambda bi, i: (bi, i, 0)),
            pl.BlockSpec((1, N_MOD, d), lambda bi, i: (bi, 0, 0)),
            pl.BlockSpec((4, d), lambda bi, i: (0, 0)),
        ],
        out_specs=[
            pl.BlockSpec((1, tm, d), lambda bi, i: (bi, i, 0)),
            pl.BlockSpec((1, tm, d), lambda bi, i: (bi, i, 0)),
        ],
        out_shape=[jax.ShapeDtypeStruct((b, l, d), F32), jax.ShapeDtypeStruct((b, l, d), BF16)],
        compiler_params=_cparams(("parallel", "parallel")),
        name="attn_out_proj",
    )(oa, ob, w_out, x, mod6, ng)


def _ffn_kernel(hp_ref, hm_ref, hn_ref, x_ref, mod_ref, ng_ref, wu_ref, cw_ref, cb_ref, wd_ref, o_ref,
                hcat_ref, ug0_ref, uv0_ref, ug1_ref, uv1_ref, act_ref):
    i = pl.program_id(1)
    last = pl.num_programs(1) - 1
    tm = hm_ref.shape[1]
    f = wd_ref.shape[0]
    fc = ug0_ref.shape[1]
    n_chunks = f // fc
    u_bufs = ((ug0_ref, uv0_ref), (ug1_ref, uv1_ref))
    hp = hp_ref[0]
    hn = hn_ref[0]
    hcat_ref[0:HALO, :] = jnp.where(i > 0, hp, jnp.zeros_like(hp))
    hcat_ref[HALO:HALO + tm, :] = hm_ref[0]
    hcat_ref[HALO + tm:, :] = jnp.where(i < last, hn, jnp.zeros_like(hn))

    def cols(c, half):
        return pl.ds(pl.multiple_of(half * f + c * fc, fc), fc)

    def conv(u_ref, c, half):
        cw = cw_ref[:, cols(c, half)]
        u = u_ref[...]
        rows = u.shape[0]
        prev = pltpu.roll(u, 1, 0)[HALO:HALO + tm]
        nxt = pltpu.roll(u, rows - 1, 0)[HALO:HALO + tm]
        return (prev * cw[0:1, :] + u[HALO:HALO + tm] * cw[1:2, :] + nxt * cw[2:3, :]
                + cb_ref[:, cols(c, half)])

    def up(c, bufs):
        hcat = hcat_ref[...]
        for half in range(2):
            bufs[half][...] = jnp.dot(hcat, wu_ref[:, cols(c, half)], preferred_element_type=F32)

    def mid(c, bufs):
        gate = conv(bufs[0], c, 0)
        val = conv(bufs[1], c, 1)
        act_ref[:, cols(c, 0)] = (gate / (1.0 + jnp.exp(-gate)) * val).astype(BF16)

    def stage(c, parity, do_up):
        if do_up:
            up(c + 1, u_bufs[1 - parity])
        mid(c, u_bufs[parity])

    up(0, u_bufs[0])
    unroll = FFN_UNROLL
    n_groups = (n_chunks - 1) // unroll

    def group(g, carry):
        for u in range(unroll):
            stage(g * unroll + u, u % 2, True)
        return carry

    lax.fori_loop(0, n_groups, group, 0)
    for c in range(n_groups * unroll, n_chunks):
        stage(c, c % 2, c + 1 < n_chunks)
    y = jnp.dot(act_ref[...], wd_ref[...], preferred_element_type=F32)
    o_ref[0] = x_ref[0] + mod_ref[0, 5:6, :] * _rms(y, ng_ref[3:4, :])


def _ffn(h2, x1, mod6, ng, w_up, conv_w, conv_b, w_down, tm, fc):
    b, l, d = x1.shape
    f = w_down.shape[0]
    assert f % fc == 0
    per = tm // HALO
    n_halo = l // HALO
    resident = lambda shape: pl.BlockSpec(shape, lambda bi, i: (0,) * len(shape),
                                          pipeline_mode=pl.Buffered(1))
    return pl.pallas_call(
        _ffn_kernel,
        grid=(b, l // tm),
        in_specs=[
            pl.BlockSpec((1, HALO, d), lambda bi, i: (bi, jnp.maximum(i * per - 1, 0), 0)),
            pl.BlockSpec((1, tm, d), lambda bi, i: (bi, i, 0)),
            pl.BlockSpec((1, HALO, d), lambda bi, i: (bi, jnp.minimum((i + 1) * per, n_halo - 1), 0)),
            pl.BlockSpec((1, tm, d), lambda bi, i: (bi, i, 0)),
            pl.BlockSpec((1, N_MOD, d), lambda bi, i: (bi, 0, 0)),
            pl.BlockSpec((4, d), lambda bi, i: (0, 0)),
            resident((d, 2 * f)), resident((3, 2 * f)), resident((1, 2 * f)), resident((f, d)),
        ],
        out_specs=pl.BlockSpec((1, tm, d), lambda bi, i: (bi, i, 0)),
        out_shape=jax.ShapeDtypeStruct((b, l, d), F32),
        scratch_shapes=[
            pltpu.VMEM((tm + 2 * HALO, d), BF16),
            pltpu.VMEM((tm + 2 * HALO, fc), F32),
            pltpu.VMEM((tm + 2 * HALO, fc), F32),
            pltpu.VMEM((tm + 2 * HALO, fc), F32),
            pltpu.VMEM((tm + 2 * HALO, fc), F32),
            pltpu.VMEM((tm, f), BF16),
        ],
        compiler_params=_cparams(("parallel", "arbitrary")),
        name="conv_ffn",
    )(h2, h2, h2, x1, mod6, ng, w_up.astype(BF16), conv_w, conv_b.reshape(1, 2 * f), w_down.astype(BF16))


def _dft_tables(l, d):
    n1, n2, cg = DFT_N1, l // DFT_N1, d // FNET_GROUPS
    a = np.arange(cg)
    ang_c = 2.0 * np.pi * np.outer(a, a) / cg
    csc = np.concatenate([np.cos(ang_c), np.sin(ang_c)], axis=1) / math.sqrt(cg)
    a1 = np.arange(n1)
    ang1 = 2.0 * np.pi * np.outer(a1, a1) / n1
    c1, s1 = np.cos(ang1), np.sin(ang1)
    m1 = np.block([[c1, -s1], [s1, c1]])
    a2 = np.arange(n2)
    ang2 = 2.0 * np.pi * np.outer(a2, a2) / n2
    c2 = np.cos(ang2) / math.sqrt(l)
    s2 = np.sin(ang2) / math.sqrt(l)
    ang_t = 2.0 * np.pi * np.outer(a1, a2) / l
    f32 = lambda t: jnp.asarray(t, F32)
    return (f32(csc).astype(BF16), f32(m1).astype(BF16), f32(c2).astype(BF16), f32(s2).astype(BF16),
            f32(np.cos(ang_t)), f32(np.sin(ang_t)))


def _fourier1_kernel(x_ref, mod_ref, ng_ref, csc_ref, m1_ref, twc_ref, tws_ref, zr_ref, zi_ref):
    d = mod_ref.shape[2]
    cg = csc_ref.shape[0]
    n1 = x_ref.shape[1]
    t2s = x_ref.shape[2] // d
    csc = csc_ref[...]
    m1 = m1_ref[...]
    stacked = []
    for t in range(t2s):
        xb = x_ref[0, :, t * d:(t + 1) * d]
        h = (_rms(xb, ng_ref[0:1, :]) * (1.0 + mod_ref[0, 1:2, :]) + mod_ref[0, 0:1, :]).astype(BF16)
        ab = [jnp.dot(h[:, g * cg:(g + 1) * cg], csc, preferred_element_type=F32)
              for g in range(d // cg)]
        a = jnp.concatenate([t_[:, :cg] for t_ in ab], axis=1)
        bm = jnp.concatenate([t_[:, cg:] for t_ in ab], axis=1)
        stacked.append(jnp.concatenate([a, bm], axis=0).astype(BF16))
    ys = [jnp.dot(m1, ab2, preferred_element_type=F32) for ab2 in stacked]
    for t, y in enumerate(ys):
        yr = y[:n1]
        yin = y[n1:]
        cc = jnp.tile(twc_ref[:, t * HEAD_DIM:(t + 1) * HEAD_DIM], (1, d // HEAD_DIM))
        ss = jnp.tile(tws_ref[:, t * HEAD_DIM:(t + 1) * HEAD_DIM], (1, d // HEAD_DIM))
        zr_ref[0, :, t * d:(t + 1) * d] = (yr * cc - yin * ss).astype(BF16)
        zi_ref[0, :, t * d:(t + 1) * d] = (-(yin * cc) - yr * ss).astype(BF16)


def _fourier1(x, mod6, ng, csc, m1, twc, tws, t2s):
    b, l, d = x.shape
    n1, n2 = DFT_N1, l // DFT_N1
    xv = x.reshape(b, n1, n2 * d)
    lanes = HEAD_DIM
    twc_x = jnp.repeat(twc, lanes, axis=1)
    tws_x = jnp.repeat(tws, lanes, axis=1)
    cg = csc.shape[0]
    return pl.pallas_call(
        _fourier1_kernel,
        grid=(b, n2 // t2s),
        in_specs=[
            pl.BlockSpec((1, n1, t2s * d), lambda bi, j: (bi, 0, j)),
            pl.BlockSpec((1, N_MOD, d), lambda bi, j: (bi, 0, 0)),
            pl.BlockSpec((4, d), lambda bi, j: (0, 0)),
            pl.BlockSpec((cg, 2 * cg), lambda bi, j: (0, 0)),
            pl.BlockSpec((2 * n1, 2 * n1), lambda bi, j: (0, 0)),
            pl.BlockSpec((n1, t2s * lanes), lambda bi, j: (0, j)),
            pl.BlockSpec((n1, t2s * lanes), lambda bi, j: (0, j)),
        ],
        out_specs=[
            pl.BlockSpec((1, n1, t2s * d), lambda bi, j: (bi, 0, j)),
            pl.BlockSpec((1, n1, t2s * d), lambda bi, j: (bi, 0, j)),
        ],
        out_shape=[jax.ShapeDtypeStruct((b, n1, n2 * d), BF16)] * 2,
        compiler_params=_cparams(("parallel", "parallel")),
        name="fnet_stage1",
    )(xv, mod6, ng, csc, m1, twc_x, tws_x)


def _fourier2_kernel(zr_ref, zi_ref, x_ref, c2_ref, s2_ref, w_ref, mod_ref, ng_ref, x1_ref, h2_ref):
    k1s = zr_ref.shape[1]
    n2 = zr_ref.shape[2]
    d = zr_ref.shape[3]
    c2 = c2_ref[...]
    s2 = s2_ref[...]
    fs = []
    for k in range(k1s):
        f = (jnp.dot(c2, zr_ref[0, k], preferred_element_type=F32)
             + jnp.dot(s2, zi_ref[0, k], preferred_element_type=F32))
        fs.append(f.astype(BF16))
    y = jnp.dot(jnp.concatenate(fs, axis=0), w_ref[...], preferred_element_type=F32)
    for k in range(k1s):
        x1, h2 = _residual_and_next(y[k * n2:(k + 1) * n2], x_ref[0, :, k * d:(k + 1) * d], mod_ref, ng_ref)
        x1_ref[0, :, k * d:(k + 1) * d] = x1
        h2_ref[0, :, k * d:(k + 1) * d] = h2.astype(BF16)


def _fourier2(zr, zi, x, c2, s2, w_f, mod6, ng, k1s):
    b, l, d = x.shape
    n1, n2 = DFT_N1, l // DFT_N1
    zr4 = zr.reshape(b, n1, n2, d)
    zi4 = zi.reshape(b, n1, n2, d)
    xv = x.reshape(b, n2, n1 * d)
    x1, h2 = pl.pallas_call(
        _fourier2_kernel,
        grid=(b, n1 // k1s),
        in_specs=[
            pl.BlockSpec((1, k1s, n2, d), lambda bi, j: (bi, j, 0, 0)),
            pl.BlockSpec((1, k1s, n2, d), lambda bi, j: (bi, j, 0, 0)),
            pl.BlockSpec((1, n2, k1s * d), lambda bi, j: (bi, 0, j)),
            pl.BlockSpec((n2, n2), lambda bi, j: (0, 0)),
            pl.BlockSpec((n2, n2), lambda bi, j: (0, 0)),
            pl.BlockSpec((d, d), lambda bi, j: (0, 0)),
            pl.BlockSpec((1, N_MOD, d), lambda bi, j: (bi, 0, 0)),
            pl.BlockSpec((4, d), lambda bi, j: (0, 0)),
        ],
        out_specs=[
            pl.BlockSpec((1, n2, k1s * d), lambda bi, j: (bi, 0, j)),
            pl.BlockSpec((1, n2, k1s * d), lambda bi, j: (bi, 0, j)),
        ],
        out_shape=[jax.ShapeDtypeStruct((b, n2, n1 * d), F32), jax.ShapeDtypeStruct((b, n2, n1 * d), BF16)],
        compiler_params=_cparams(("parallel", "parallel")),
        name="fnet_stage2",
    )(zr4, zi4, xv, c2, s2, w_f, mod6, ng)
    return x1.reshape(b, l, d), h2.reshape(b, l, d)


def _rope_tables(l):
    n_freq = HEAD_DIM // 4
    inv = ROPE_THETA ** (-jnp.arange(n_freq, dtype=F32) / n_freq)
    ang_row = jnp.arange(l // GRID_W).astype(F32)[:, None] * inv
    ang_col = jnp.arange(GRID_W).astype(F32)[:, None] * inv
    zr = jnp.zeros_like(ang_row)
    zc = jnp.zeros_like(ang_col)
    cos_r = jnp.concatenate([jnp.cos(ang_row), jnp.cos(ang_row), zr, zr], axis=1)
    sin_r = jnp.concatenate([-jnp.sin(ang_row), jnp.sin(ang_row), zr, zr], axis=1)
    cos_c = jnp.concatenate([zc, zc, jnp.cos(ang_col), jnp.cos(ang_col)], axis=1)
    sin_c = jnp.concatenate([zc, zc, -jnp.sin(ang_col), jnp.sin(ang_col)], axis=1)
    return cos_r, sin_r, cos_c, sin_c


def kernel(x, c, ctx, c_ctx, mod_w, mod_b, norm_g, attn_w_in, attn_w_out, q_norm_g, k_norm_g, sink,
           fourier_w_out, ffn_w_up, ffn_conv_w, ffn_conv_b, ffn_w_down):
    b, l, d = x.shape
    c_len = ctx.shape[1]
    assert mod_w.shape[0] == 2 and b + 1 <= COND_ROWS
    assert l % max(QKV_TM, WIN_TQ, FFN_TM, POST_TM, DFT_N1 * F1_T2) == 0

    cond = jnp.zeros((COND_ROWS, d), F32).at[:b].set(c).at[b].set(c_ctx)
    mods = _mods(cond, mod_w, mod_b)
    mod_l0 = mods[0].reshape(COND_ROWS, N_MOD, d)
    mod_l1 = mods[1].reshape(COND_ROWS, N_MOD, d)

    rope = _rope_tables(l)
    w_in = attn_w_in[0].astype(BF16)
    qg = q_norm_g[0].reshape(1, HEAD_DIM)
    kg = k_norm_g[0].reshape(1, HEAD_DIM)
    qa, ka, vat, qb, kb, vb = _qkv(x, mod_l0, lambda bi: bi, norm_g[0], w_in, qg, kg, rope, QKV_TM)
    zeros_r = jnp.zeros((8, HEAD_DIM), F32)
    zeros_c = jnp.zeros((GRID_W, HEAD_DIM), F32)
    no_rope = (jnp.ones_like(zeros_r), zeros_r, zeros_c, zeros_c)
    _, kac, vact, _, kbc, vbc = _qkv(ctx, mod_l0, lambda bi: b, norm_g[0], w_in, qg, kg, no_rope, c_len)
    oa = _attn_a(qa, kac, vact, ka, vat, ATT_TQ, ATT_TK)
    ob = _attn_b(sink[0], qb, kb, vb, kbc, vbc, WIN_TQ)
    x1, h2 = _post_attn(oa, ob, attn_w_out[0].astype(BF16), x, mod_l0, norm_g[0], POST_TM)
    x2 = _ffn(h2, x1, mod_l0, norm_g[0], ffn_w_up[0], ffn_conv_w[0], ffn_conv_b[0], ffn_w_down[0],
              FFN_TM, FFN_FC)

    csc, m1, c2, s2, twc, tws = _dft_tables(l, d)
    zr, zi = _fourier1(x2, mod_l1, norm_g[1], csc, m1, twc, tws, F1_T2)
    x3, h2 = _fourier2(zr, zi, x2, c2, s2, fourier_w_out[0].astype(BF16), mod_l1, norm_g[1], F2_K1)
    return _ffn(h2, x3, mod_l1, norm_g[1], ffn_w_up[1], ffn_conv_w[1], ffn_conv_b[1], ffn_w_down[1],
                FFN_TM, FFN_FC)
```

```python
import functools
import math

import numpy as np
import jax
import jax.numpy as jnp
from jax import lax
from jax.experimental import pallas as pl
from jax.experimental.pallas import tpu as pltpu

F32 = jnp.float32
BF16 = jnp.bfloat16

HEAD_DIM = 128
GRID_W = 64
A_HEADS = 4
A_KV_HEADS = 2
B_HEADS = 4
B_KV_HEADS = 2
GROUP = A_HEADS // A_KV_HEADS
BLOCK = 128
WINDOW = 128
ROPE_THETA = 10000.0
FNET_GROUPS = 4
N_MOD = 6
EPS = 1e-6
LOG2E = math.log2(math.e)
Q_SCALE = HEAD_DIM ** -0.5 * LOG2E
NEG_BIG = -1e30

DFT_N1 = 128
COND_ROWS = 8
VMEM_LIMIT = 56 * 1024 * 1024

QKV_TM = 512
QKV_SUB = 2
ATT_TQ = 1024
ATT_TK = 512
ATT_UNROLL = 2
V_ROWS = HEAD_DIM + 16
WIN_TQ = 1024
POST_TM = 1024
POST_SUB = 8
FFN_TM = 1024
FFN_FC = 256
FFN_UNROLL = 10
HALO = 16
F1_T2 = 8
F2_K1 = 8
F2_GROUPS = 4


def _cparams(sem):
    return pltpu.CompilerParams(dimension_semantics=sem, vmem_limit_bytes=VMEM_LIMIT)


def _rms(x, g):
    return x * lax.rsqrt(jnp.mean(x * x, axis=-1, keepdims=True) + EPS) * g


def _mod_kernel(cond_ref, w_ref, b_ref, o_ref):
    c = cond_ref[...]
    s = c / (1.0 + jnp.exp(-c))
    o_ref[0] = jnp.dot(s, w_ref[0], preferred_element_type=F32) + b_ref[0]


def _mods(cond, mod_w, mod_b):
    depth, d, n = mod_w.shape
    tn = n // 4
    return pl.pallas_call(
        _mod_kernel,
        grid=(depth, n // tn),
        in_specs=[
            pl.BlockSpec((COND_ROWS, d), lambda l, j: (0, 0)),
            pl.BlockSpec((1, d, tn), lambda l, j: (l, 0, j)),
            pl.BlockSpec((1, 1, tn), lambda l, j: (l, 0, j)),
        ],
        out_specs=pl.BlockSpec((1, COND_ROWS, tn), lambda l, j: (l, 0, j)),
        out_shape=jax.ShapeDtypeStruct((depth, COND_ROWS, n), F32),
        compiler_params=_cparams(("arbitrary", "arbitrary")),
        name="adaln_mods",
    )(cond, mod_w, mod_b.reshape(depth, 1, n))


def _qkv_kernel(x_ref, mod_ref, ng_ref, w_ref, qg_ref, kg_ref, cr_ref, sr_ref, cc_ref, sc_ref,
                qa_ref, ka_ref, vat_ref, qb_ref, kb_ref, vb_ref):
    tm = x_ref.shape[1]
    rows = tm // QKV_SUB
    assert rows % GRID_W == 0
    projs = []
    for r in range(QKV_SUB):
        x = x_ref[0, r * rows:(r + 1) * rows, :]
        h = _rms(x, ng_ref[0:1, :]) * (1.0 + mod_ref[0, 1:2, :]) + mod_ref[0, 0:1, :]
        projs.append(jnp.dot(h.astype(BF16), w_ref[...], preferred_element_type=F32))

    lane = lax.broadcasted_iota(jnp.int32, (rows, HEAD_DIM), 1)
    low_half = (lane % (HEAD_DIM // 2)) < (HEAD_DIM // 4)
    qg = qg_ref[...]
    kg = kg_ref[...]
    col_cos = cc_ref[...]
    col_sin = sc_ref[...]

    for r, proj in enumerate(projs):
        sl = slice(r * rows, (r + 1) * rows)
        g0 = r * (rows // GRID_W)

        def table(row_ref, col_part):
            return jnp.concatenate([jnp.broadcast_to(row_ref[g0 + i:g0 + i + 1, :], (GRID_W, HEAD_DIM)) + col_part
                                    for i in range(rows // GRID_W)], axis=0)

        cos = table(cr_ref, col_cos)
        sin = table(sr_ref, col_sin)

        def rope(t):
            partner = jnp.where(low_half,
                                pltpu.roll(t, HEAD_DIM - HEAD_DIM // 4, 1),
                                pltpu.roll(t, HEAD_DIM // 4, 1))
            return t * cos + partner * sin

        def head(j):
            return proj[:, j * HEAD_DIM:(j + 1) * HEAD_DIM]

        col = 0
        for hh in range(A_HEADS):
            qa_ref[0, hh, :, sl] = (rope(_rms(head(col + hh), qg)) * Q_SCALE).T.astype(BF16)
        col += A_HEADS
        for hh in range(B_HEADS):
            qb_ref[0, hh, sl, :] = (rope(head(col + hh)) * Q_SCALE).astype(BF16)
        col += B_HEADS
        for hh in range(A_KV_HEADS):
            ka_ref[0, hh, sl, :] = rope(_rms(head(col + hh), kg)).astype(BF16)
        col += A_KV_HEADS
        for hh in range(A_KV_HEADS):
            vat_ref[0, hh, 0, 0:HEAD_DIM, sl] = head(col + hh).T.astype(BF16)
            vat_ref[0, hh, 0, HEAD_DIM:, sl] = jnp.ones((V_ROWS - HEAD_DIM, rows), BF16)
        col += A_KV_HEADS
        for hh in range(B_KV_HEADS):
            kb_ref[0, hh, sl, :] = rope(head(col + hh)).astype(BF16)
        col += B_KV_HEADS
        for hh in range(B_KV_HEADS):
            vb_ref[0, hh, sl, :] = head(col + hh).astype(BF16)


def _qkv(x, mod6, mod_row, ng, w_in, qg, kg, rope, tm):
    b, l, d = x.shape
    n = w_in.shape[1]
    nt = l // tm
    assert tm % (GRID_W * QKV_SUB) == 0
    rt = max(tm // GRID_W, 8)
    return pl.pallas_call(
        _qkv_kernel,
        grid=(b, nt),
        in_specs=[
            pl.BlockSpec((1, tm, d), lambda bi, i: (bi, i, 0)),
            pl.BlockSpec((1, N_MOD, d), lambda bi, i: (mod_row(bi), 0, 0)),
            pl.BlockSpec((4, d), lambda bi, i: (0, 0)),
            pl.BlockSpec((d, n), lambda bi, i: (0, 0)),
            pl.BlockSpec((1, HEAD_DIM), lambda bi, i: (0, 0)),
            pl.BlockSpec((1, HEAD_DIM), lambda bi, i: (0, 0)),
            pl.BlockSpec((rt, HEAD_DIM), lambda bi, i: (i, 0)),
            pl.BlockSpec((rt, HEAD_DIM), lambda bi, i: (i, 0)),
            pl.BlockSpec((GRID_W, HEAD_DIM), lambda bi, i: (0, 0)),
            pl.BlockSpec((GRID_W, HEAD_DIM), lambda bi, i: (0, 0)),
        ],
        out_specs=[
            pl.BlockSpec((1, A_HEADS, HEAD_DIM, tm), lambda bi, i: (bi, 0, 0, i)),
            pl.BlockSpec((1, A_KV_HEADS, tm, HEAD_DIM), lambda bi, i: (bi, 0, i, 0)),
            pl.BlockSpec((1, A_KV_HEADS, 1, V_ROWS, tm), lambda bi, i: (bi, 0, i, 0, 0)),
            pl.BlockSpec((1, B_HEADS, tm, HEAD_DIM), lambda bi, i: (bi, 0, i, 0)),
            pl.BlockSpec((1, B_KV_HEADS, tm, HEAD_DIM), lambda bi, i: (bi, 0, i, 0)),
            pl.BlockSpec((1, B_KV_HEADS, tm, HEAD_DIM), lambda bi, i: (bi, 0, i, 0)),
        ],
        out_shape=[
            jax.ShapeDtypeStruct((b, A_HEADS, HEAD_DIM, l), BF16),
            jax.ShapeDtypeStruct((b, A_KV_HEADS, l, HEAD_DIM), BF16),
            jax.ShapeDtypeStruct((b, A_KV_HEADS, nt, V_ROWS, tm), BF16),
            jax.ShapeDtypeStruct((b, B_HEADS, l, HEAD_DIM), BF16),
            jax.ShapeDtypeStruct((b, B_KV_HEADS, l, HEAD_DIM), BF16),
            jax.ShapeDtypeStruct((b, B_KV_HEADS, l, HEAD_DIM), BF16),
        ],
        compiler_params=_cparams(("parallel", "parallel")),
        name="qkv_proj",
    )(x, mod6, ng, w_in, qg, kg, *rope)


def _attn_a_kernel(q_ref, kc_ref, vct_ref, k_ref, vt_ref, o_ref,
                   sc_ref, pc_ref, s0_ref, s1_ref, p0_ref, p1_ref, acc_ref):
    tq = q_ref.shape[3]
    ts = vt_ref.shape[4]
    tk = s0_ref.shape[0]
    sub = tk // ts
    n_chunks = vt_ref.shape[2] // sub
    assert n_chunks >= 2 and ATT_UNROLL % 2 == 0
    qt = jnp.concatenate([q_ref[0, g] for g in range(GROUP)], axis=1)

    def k_chunk(j):
        return k_ref[0, 0, pl.ds(pl.multiple_of(j * tk, tk), tk), :]

    def scores(kblk, s_ref):
        s = jnp.dot(kblk, qt, preferred_element_type=F32)
        s_ref[...] = s
        return jnp.max(s, axis=0, keepdims=True)

    def softmax(s_ref, p_ref, mc, m):
        m_new = jnp.maximum(m, mc)
        p_ref[...] = jnp.exp2(s_ref[...] - m_new).astype(BF16)
        return m_new, jnp.exp2(m - m_new)

    def accumulate(vt_blocks, p_ref, alpha):
        pv = None
        for u, vtblk in enumerate(vt_blocks):
            part = jnp.dot(vtblk, p_ref[u * vtblk.shape[1]:(u + 1) * vtblk.shape[1], :],
                           preferred_element_type=F32)
            pv = part if pv is None else pv + part
        acc_ref[...] = alpha * acc_ref[...] + pv

    def v_chunk(j):
        return [vt_ref[0, 0, j * sub + u] for u in range(sub)]

    acc_ref[...] = jnp.zeros(acc_ref.shape, F32)
    m = jnp.full((1, GROUP * tq), NEG_BIG, F32)
    mc_ctx = scores(kc_ref[0, 0], sc_ref)
    mc0 = scores(k_chunk(0), s0_ref)
    m, alpha_ctx = softmax(sc_ref, pc_ref, mc_ctx, m)
    mc1 = scores(k_chunk(1), s1_ref)
    m, alpha0 = softmax(s0_ref, p0_ref, mc0, m)
    accumulate([vct_ref[0, 0, 0]], pc_ref, alpha_ctx)

    s_bufs = (s0_ref, s1_ref)
    p_bufs = (p0_ref, p1_ref)

    def stage(c, parity, carry, has_next):
        m, alpha_prev, mc = carry
        mc_next = scores(k_chunk(c + 1), s_bufs[1 - parity]) if has_next else None
        m, alpha = softmax(s_bufs[parity], p_bufs[parity], mc, m)
        accumulate(v_chunk(c - 1), p_bufs[1 - parity], alpha_prev)
        return m, alpha, mc_next

    unroll = ATT_UNROLL
    n_groups = (n_chunks - 2) // unroll

    def group(g, carry):
        for u in range(unroll):
            carry = stage(1 + g * unroll + u, (1 + u) % 2, carry, True)
        return carry

    carry = lax.fori_loop(0, n_groups, group, (m, alpha0, mc1))
    for c in range(1 + n_groups * unroll, n_chunks):
        carry = stage(c, c % 2, carry, c + 1 < n_chunks)
    accumulate(v_chunk(n_chunks - 1), p_bufs[(n_chunks - 1) % 2], carry[1])

    acc = acc_ref[...]
    o = (acc[:HEAD_DIM] / acc[HEAD_DIM:HEAD_DIM + 1]).T
    for g in range(GROUP):
        o_ref[0, :, g * HEAD_DIM:(g + 1) * HEAD_DIM] = o[g * tq:(g + 1) * tq].astype(o_ref.dtype)


def _attn_a(q, kc, vct, k, vt, tq, tk):
    b, _, _, l = q.shape
    c_len = kc.shape[2]
    n_stored, ts = vt.shape[2], vt.shape[4]
    assert tk % ts == 0 and l % tk == 0
    return pl.pallas_call(
        _attn_a_kernel,
        grid=(b, A_KV_HEADS, l // tq),
        in_specs=[
            pl.BlockSpec((1, GROUP, HEAD_DIM, tq), lambda bi, kv, i: (bi, kv, 0, i)),
            pl.BlockSpec((1, 1, c_len, HEAD_DIM), lambda bi, kv, i: (bi, kv, 0, 0)),
            pl.BlockSpec((1, 1, 1, V_ROWS, c_len), lambda bi, kv, i: (bi, kv, 0, 0, 0)),
            pl.BlockSpec((1, 1, l, HEAD_DIM), lambda bi, kv, i: (bi, kv, 0, 0)),
            pl.BlockSpec((1, 1, n_stored, V_ROWS, ts), lambda bi, kv, i: (bi, kv, 0, 0, 0)),
        ],
        out_specs=pl.BlockSpec((1, tq, GROUP * HEAD_DIM), lambda bi, kv, i: (bi, i, kv)),
        out_shape=jax.ShapeDtypeStruct((b, l, A_HEADS * HEAD_DIM), BF16),
        scratch_shapes=[
            pltpu.VMEM((c_len, GROUP * tq), F32),
            pltpu.VMEM((c_len, GROUP * tq), BF16),
            pltpu.VMEM((tk, GROUP * tq), F32),
            pltpu.VMEM((tk, GROUP * tq), F32),
            pltpu.VMEM((tk, GROUP * tq), BF16),
            pltpu.VMEM((tk, GROUP * tq), BF16),
            pltpu.VMEM((V_ROWS, GROUP * tq), F32),
        ],
        compiler_params=_cparams(("parallel", "parallel", "arbitrary")),
        name="attn_global",
    )(q, kc, vct, k, vt)


def _attn_b_kernel(sink_ref, q_ref, kp_ref, km_ref, kn_ref, vp_ref, vm_ref, vn_ref,
                   kc_ref, vc_ref, o_ref, *, seq_len):
    kv = pl.program_id(1)
    i = pl.program_id(2)
    tq = q_ref.shape[2]
    sub = tq // BLOCK
    band = 3 * BLOCK
    kcat = jnp.concatenate([kp_ref[0, 0], km_ref[0, 0], kn_ref[0, 0]], axis=0)
    vcat = jnp.concatenate([vp_ref[0, 0], vm_ref[0, 0], vn_ref[0, 0]], axis=0)
    kc = kc_ref[0, 0]
    vc = vc_ref[0, 0]
    rows = GROUP * BLOCK
    nt = (((1,), (1,)), ((), ()))

    r = lax.broadcasted_iota(jnp.int32, (rows, band), 0) % BLOCK
    c = lax.broadcasted_iota(jnp.int32, (rows, band), 1)
    in_window = jnp.abs(c - BLOCK - r) <= WINDOW
    head_row = lax.broadcasted_iota(jnp.int32, (rows, 1), 0) // BLOCK
    sink2 = jnp.zeros((rows, 1), F32)
    for g in range(GROUP):
        sink2 = jnp.where(head_row == g, sink_ref[kv * GROUP + g] * LOG2E, sink2)

    scores = []
    for j in range(sub):
        q2 = jnp.concatenate([q_ref[0, g, j * BLOCK:(j + 1) * BLOCK, :] for g in range(GROUP)], axis=0)
        s_loc = lax.dot_general(q2, kcat[j * BLOCK:j * BLOCK + band], nt, preferred_element_type=F32)
        s_ctx = lax.dot_general(q2, kc, nt, preferred_element_type=F32)
        scores.append((s_loc, s_ctx))
    probs = []
    for j, (s_loc, s_ctx) in enumerate(scores):
        kpos = (i * sub + (j - 1)) * BLOCK + c
        valid = in_window & (kpos >= 0) & (kpos < seq_len)
        s_loc = jnp.where(valid, s_loc, NEG_BIG)
        m = jnp.maximum(jnp.maximum(jnp.max(s_loc, axis=-1, keepdims=True),
                                    jnp.max(s_ctx, axis=-1, keepdims=True)), sink2)
        p_loc = jnp.exp2(s_loc - m)
        p_ctx = jnp.exp2(s_ctx - m)
        denom = (jnp.sum(p_loc, axis=-1, keepdims=True) + jnp.sum(p_ctx, axis=-1, keepdims=True)
                 + jnp.exp2(sink2 - m))
        probs.append((p_loc.astype(BF16), p_ctx.astype(BF16), denom))
    for j, (p_loc, p_ctx, denom) in enumerate(probs):
        o = (jnp.dot(p_ctx, vc, preferred_element_type=F32)
             + jnp.dot(p_loc, vcat[j * BLOCK:j * BLOCK + band], preferred_element_type=F32)) / denom
        for g in range(GROUP):
            o_ref[0, j * BLOCK:(j + 1) * BLOCK, g * HEAD_DIM:(g + 1) * HEAD_DIM] = (
                o[g * BLOCK:(g + 1) * BLOCK].astype(o_ref.dtype))


def _attn_b(sink, q, k, v, kc, vc, tq):
    b, _, l, _ = q.shape
    c_len = kc.shape[2]
    sub = tq // BLOCK
    nb = l // BLOCK

    def main(bi, kv, i):
        return (bi, kv, i, 0)

    def prev(bi, kv, i):
        return (bi, kv, jnp.maximum(i * sub - 1, 0), 0)

    def nxt(bi, kv, i):
        return (bi, kv, jnp.minimum((i + 1) * sub, nb - 1), 0)

    def ctx_map(bi, kv, i):
        return (bi, kv, 0, 0)

    edge = lambda fn: pl.BlockSpec((1, 1, BLOCK, HEAD_DIM), fn)
    return pl.pallas_call(
        functools.partial(_attn_b_kernel, seq_len=l),
        grid=(b, B_KV_HEADS, l // tq),
        in_specs=[
            pl.BlockSpec(memory_space=pltpu.SMEM),
            pl.BlockSpec((1, GROUP, tq, HEAD_DIM), main),
            edge(prev), pl.BlockSpec((1, 1, tq, HEAD_DIM), main), edge(nxt),
            edge(prev), pl.BlockSpec((1, 1, tq, HEAD_DIM), main), edge(nxt),
            pl.BlockSpec((1, 1, c_len, HEAD_DIM), ctx_map),
            pl.BlockSpec((1, 1, c_len, HEAD_DIM), ctx_map),
        ],
        out_specs=pl.BlockSpec((1, tq, GROUP * HEAD_DIM), lambda bi, kv, i: (bi, i, kv)),
        out_shape=jax.ShapeDtypeStruct((b, l, B_HEADS * HEAD_DIM), BF16),
        compiler_params=_cparams(("parallel", "parallel", "arbitrary")),
        name="attn_window",
    )(sink, q, k, k, k, v, v, v, kc, vc)


def _residual_and_next(y, x, mod_ref, ng_ref):
    x1 = x + mod_ref[0, 2:3, :] * _rms(y, ng_ref[1:2, :])
    h2 = _rms(x1, ng_ref[2:3, :]) * (1.0 + mod_ref[0, 4:5, :]) + mod_ref[0, 3:4, :]
    return x1, h2


def _post_attn_kernel(oa_ref, ob_ref, w_ref, x_ref, mod_ref, ng_ref, x1_ref, h2_ref):
    na = oa_ref.shape[2]
    rows = oa_ref.shape[1] // POST_SUB
    ys = []
    for r in range(POST_SUB):
        sl = slice(r * rows, (r + 1) * rows)
        ys.append(jnp.dot(oa_ref[0, sl, :], w_ref[0:na, :], preferred_element_type=F32)
                  + jnp.dot(ob_ref[0, sl, :], w_ref[na:, :], preferred_element_type=F32))
    for r, y in enumerate(ys):
        sl = slice(r * rows, (r + 1) * rows)
        x1, h2 = _residual_and_next(y, x_ref[0, sl, :], mod_ref, ng_ref)
        x1_ref[0, sl, :] = x1
        h2_ref[0, sl, :] = h2.astype(BF16)


def _post_attn(oa, ob, w_out, x, mod6, ng, tm):
    b, l, d = x.shape
    na, nb_ = oa.shape[2], ob.shape[2]
    return pl.pallas_call(
        _post_attn_kernel,
        grid=(b, l // tm),
        in_specs=[
            pl.BlockSpec((1, tm, na), lambda bi, i: (bi, i, 0)),
            pl.BlockSpec((1, tm, nb_), lambda bi, i: (bi, i, 0)),
            pl.BlockSpec((na + nb_, d), lambda bi, i: (0, 0)),
            pl.BlockSpec((1, tm, d), lambda bi, i: (bi, i, 0)),
            pl.BlockSpec((1, N_MOD, d), lambda bi, i: (bi, 0, 0)),
            pl.BlockSpec((4, d), lambda bi, i: (0, 0)),
        ],
        out_specs=[
            pl.BlockSpec((1, tm, d), lambda bi, i: (bi, i, 0)),
            pl.BlockSpec((1, tm, d), lambda bi, i: (bi, i, 0)),
        ],
        out_shape=[jax.ShapeDtypeStruct((b, l, d), F32), jax.ShapeDtypeStruct((b, l, d), BF16)],
        compiler_params=_cparams(("parallel", "parallel")),
        name="attn_out_proj",
    )(oa, ob, w_out, x, mod6, ng)


def _ffn_kernel(hp_ref, hm_ref, hn_ref, x_ref, mod_ref, ng_ref, wu_ref, cw_ref, cb_ref, wd_ref, o_ref,
                hcat_ref, ug0_ref, uv0_ref, ug1_ref, uv1_ref, act_ref):
    i = pl.program_id(1)
    last = pl.num_programs(1) - 1
    tm = hm_ref.shape[1]
    f = wd_ref.shape[0]
    fc = ug0_ref.shape[1]
    n_chunks = f // fc
    u_bufs = ((ug0_ref, uv0_ref), (ug1_ref, uv1_ref))
    hp = hp_ref[0]
    hn = hn_ref[0]
    hcat_ref[0:HALO, :] = jnp.where(i > 0, hp, jnp.zeros_like(hp))
    hcat_ref[HALO:HALO + tm, :] = hm_ref[0]
    hcat_ref[HALO + tm:, :] = jnp.where(i < last, hn, jnp.zeros_like(hn))

    def cols(c, half):
        return pl.ds(pl.multiple_of(half * f + c * fc, fc), fc)

    def conv(u_ref, c, half):
        cw = cw_ref[:, cols(c, half)]
        u = u_ref[...]
        rows = u.shape[0]
        prev = pltpu.roll(u, 1, 0)[HALO:HALO + tm]
        nxt = pltpu.roll(u, rows - 1, 0)[HALO:HALO + tm]
        return (prev * cw[0:1, :] + u[HALO:HALO + tm] * cw[1:2, :] + nxt * cw[2:3, :]
                + cb_ref[:, cols(c, half)])

    def up(c, bufs):
        hcat = hcat_ref[...]
        for half in range(2):
            bufs[half][...] = jnp.dot(hcat, wu_ref[:, cols(c, half)], preferred_element_type=F32)

    def mid(c, bufs):
        gate = conv(bufs[0], c, 0)
        val = conv(bufs[1], c, 1)
        act_ref[:, cols(c, 0)] = (gate / (1.0 + jnp.exp(-gate)) * val).astype(BF16)

    def stage(c, parity, do_up):
        if do_up:
            up(c + 1, u_bufs[1 - parity])
        mid(c, u_bufs[parity])

    up(0, u_bufs[0])
    unroll = FFN_UNROLL
    n_groups = (n_chunks - 1) // unroll

    def group(g, carry):
        for u in range(unroll):
            stage(g * unroll + u, u % 2, True)
        return carry

    lax.fori_loop(0, n_groups, group, 0)
    for c in range(n_groups * unroll, n_chunks):
        stage(c, c % 2, c + 1 < n_chunks)
    y = jnp.dot(act_ref[...], wd_ref[...], preferred_element_type=F32)
    o_ref[0] = x_ref[0] + mod_ref[0, 5:6, :] * _rms(y, ng_ref[3:4, :])


def _ffn(h2, x1, mod6, ng, w_up, conv_w, conv_b, w_down, tm, fc):
    b, l, d = x1.shape
    f = w_down.shape[0]
    assert f % fc == 0
    per = tm // HALO
    n_halo = l // HALO
    resident = lambda shape: pl.BlockSpec(shape, lambda bi, i: (0,) * len(shape),
                                          pipeline_mode=pl.Buffered(1))
    return pl.pallas_call(
        _ffn_kernel,
        grid=(b, l // tm),
        in_specs=[
            pl.BlockSpec((1, HALO, d), lambda bi, i: (bi, jnp.maximum(i * per - 1, 0), 0)),
            pl.BlockSpec((1, tm, d), lambda bi, i: (bi, i, 0)),
            pl.BlockSpec((1, HALO, d), lambda bi, i: (bi, jnp.minimum((i + 1) * per, n_halo - 1), 0)),
            pl.BlockSpec((1, tm, d), lambda bi, i: (bi, i, 0)),
            pl.BlockSpec((1, N_MOD, d), lambda bi, i: (bi, 0, 0)),
            pl.BlockSpec((4, d), lambda bi, i: (0, 0)),
            resident((d, 2 * f)), resident((3, 2 * f)), resident((1, 2 * f)), resident((f, d)),
        ],
        out_specs=pl.BlockSpec((1, tm, d), lambda bi, i: (bi, i, 0)),
        out_shape=jax.ShapeDtypeStruct((b, l, d), F32),
        scratch_shapes=[
            pltpu.VMEM((tm + 2 * HALO, d), BF16),
            pltpu.VMEM((tm + 2 * HALO, fc), F32),
            pltpu.VMEM((tm + 2 * HALO, fc), F32),
            pltpu.VMEM((tm + 2 * HALO, fc), F32),
            pltpu.VMEM((tm + 2 * HALO, fc), F32),
            pltpu.VMEM((tm, f), BF16),
        ],
        compiler_params=_cparams(("parallel", "arbitrary")),
        name="conv_ffn",
    )(h2, h2, h2, x1, mod6, ng, w_up.astype(BF16), conv_w, conv_b.reshape(1, 2 * f), w_down.astype(BF16))


def _dft_tables(l, d):
    n1, n2, cg = DFT_N1, l // DFT_N1, d // FNET_GROUPS
    a = np.arange(cg)
    ang_c = 2.0 * np.pi * np.outer(a, a) / cg
    csc = np.concatenate([np.cos(ang_c), np.sin(ang_c)], axis=1) / math.sqrt(cg)
    a1 = np.arange(n1)
    ang1 = 2.0 * np.pi * np.outer(a1, a1) / n1
    c1, s1 = np.cos(ang1), np.sin(ang1)
    m1 = np.block([[c1, -s1], [s1, c1]])
    a2 = np.arange(n2)
    ang2 = 2.0 * np.pi * np.outer(a2, a2) / n2
    c2 = np.cos(ang2) / math.sqrt(l)
    s2 = np.sin(ang2) / math.sqrt(l)
    ang_t = 2.0 * np.pi * np.outer(a1, a2) / l
    f32 = lambda t: jnp.asarray(t, F32)
    return (f32(csc).astype(BF16), f32(m1).astype(BF16), f32(c2).astype(BF16), f32(s2).astype(BF16),
            f32(np.cos(ang_t)), f32(np.sin(ang_t)))


def _fourier1_kernel(x_ref, mod_ref, ng_ref, csc_ref, m1_ref, twc_ref, tws_ref, zr_ref, zi_ref):
    d = mod_ref.shape[2]
    cg = csc_ref.shape[0]
    n1 = x_ref.shape[1]
    t2s = x_ref.shape[2] // d
    csc = csc_ref[...]
    m1 = m1_ref[...]
    stacked = []
    for t in range(t2s):
        xb = x_ref[0, :, t * d:(t + 1) * d]
        h = (_rms(xb, ng_ref[0:1, :]) * (1.0 + mod_ref[0, 1:2, :]) + mod_ref[0, 0:1, :]).astype(BF16)
        ab = [jnp.dot(h[:, g * cg:(g + 1) * cg], csc, preferred_element_type=F32)
              for g in range(d // cg)]
        a = jnp.concatenate([t_[:, :cg] for t_ in ab], axis=1)
        bm = jnp.concatenate([t_[:, cg:] for t_ in ab], axis=1)
        stacked.append(jnp.concatenate([a, bm], axis=0).astype(BF16))
    ys = [jnp.dot(m1, ab2, preferred_element_type=F32) for ab2 in stacked]
    for t, y in enumerate(ys):
        yr = y[:n1]
        yin = y[n1:]
        cc = jnp.tile(twc_ref[:, t * HEAD_DIM:(t + 1) * HEAD_DIM], (1, d // HEAD_DIM))
        ss = jnp.tile(tws_ref[:, t * HEAD_DIM:(t + 1) * HEAD_DIM], (1, d // HEAD_DIM))
        zr_ref[0, :, t * d:(t + 1) * d] = (yr * cc - yin * ss).astype(BF16)
        zi_ref[0, :, t * d:(t + 1) * d] = (-(yin * cc) - yr * ss).astype(BF16)


def _fourier1(x, mod6, ng, csc, m1, twc, tws, t2s):
    b, l, d = x.shape
    n1, n2 = DFT_N1, l // DFT_N1
    xv = x.reshape(b, n1, n2 * d)
    lanes = HEAD_DIM
    twc_x = jnp.repeat(twc, lanes, axis=1)
    tws_x = jnp.repeat(tws, lanes, axis=1)
    cg = csc.shape[0]
    return pl.pallas_call(
        _fourier1_kernel,
        grid=(b, n2 // t2s),
        in_specs=[
            pl.BlockSpec((1, n1, t2s * d), lambda bi, j: (bi, 0, j)),
            pl.BlockSpec((1, N_MOD, d), lambda bi, j: (bi, 0, 0)),
            pl.BlockSpec((4, d), lambda bi, j: (0, 0)),
            pl.BlockSpec((cg, 2 * cg), lambda bi, j: (0, 0)),
            pl.BlockSpec((2 * n1, 2 * n1), lambda bi, j: (0, 0)),
            pl.BlockSpec((n1, t2s * lanes), lambda bi, j: (0, j)),
            pl.BlockSpec((n1, t2s * lanes), lambda bi, j: (0, j)),
        ],
        out_specs=[
            pl.BlockSpec((1, n1, t2s * d), lambda bi, j: (bi, 0, j)),
            pl.BlockSpec((1, n1, t2s * d), lambda bi, j: (bi, 0, j)),
        ],
        out_shape=[jax.ShapeDtypeStruct((b, n1, n2 * d), BF16)] * 2,
        compiler_params=_cparams(("parallel", "parallel")),
        name="fnet_stage1",
    )(xv, mod6, ng, csc, m1, twc_x, tws_x)


def _fourier2_kernel(zr_ref, zi_ref, x_ref, c2_ref, s2_ref, w_ref, mod_ref, ng_ref, x1_ref, h2_ref):
    k1s = zr_ref.shape[1]
    n2 = zr_ref.shape[2]
    d = zr_ref.shape[3]
    c2 = c2_ref[...]
    s2 = s2_ref[...]
    fs = []
    for k in range(k1s):
        f = (jnp.dot(c2, zr_ref[0, k], preferred_element_type=F32)
             + jnp.dot(s2, zi_ref[0, k], preferred_element_type=F32))
        fs.append(f.astype(BF16))
    per = k1s // F2_GROUPS
    ys = [jnp.dot(jnp.concatenate(fs[g * per:(g + 1) * per], axis=0), w_ref[...], preferred_element_type=F32)
          for g in range(F2_GROUPS)]
    for k in range(k1s):
        y = ys[k // per][(k % per) * n2:(k % per + 1) * n2]
        x1, h2 = _residual_and_next(y, x_ref[0, :, k * d:(k + 1) * d], mod_ref, ng_ref)
        x1_ref[0, :, k * d:(k + 1) * d] = x1
        h2_ref[0, :, k * d:(k + 1) * d] = h2.astype(BF16)


def _fourier2(zr, zi, x, c2, s2, w_f, mod6, ng, k1s):
    b, l, d = x.shape
    n1, n2 = DFT_N1, l // DFT_N1
    zr4 = zr.reshape(b, n1, n2, d)
    zi4 = zi.reshape(b, n1, n2, d)
    xv = x.reshape(b, n2, n1 * d)
    x1, h2 = pl.pallas_call(
        _fourier2_kernel,
        grid=(b, n1 // k1s),
        in_specs=[
            pl.BlockSpec((1, k1s, n2, d), lambda bi, j: (bi, j, 0, 0)),
            pl.BlockSpec((1, k1s, n2, d), lambda bi, j: (bi, j, 0, 0)),
            pl.BlockSpec((1, n2, k1s * d), lambda bi, j: (bi, 0, j)),
            pl.BlockSpec((n2, n2), lambda bi, j: (0, 0)),
            pl.BlockSpec((n2, n2), lambda bi, j: (0, 0)),
            pl.BlockSpec((d, d), lambda bi, j: (0, 0)),
            pl.BlockSpec((1, N_MOD, d), lambda bi, j: (bi, 0, 0)),
            pl.BlockSpec((4, d), lambda bi, j: (0, 0)),
        ],
        out_specs=[
            pl.BlockSpec((1, n2, k1s * d), lambda bi, j: (bi, 0, j)),
            pl.BlockSpec((1, n2, k1s * d), lambda bi, j: (bi, 0, j)),
        ],
        out_shape=[jax.ShapeDtypeStruct((b, n2, n1 * d), F32), jax.ShapeDtypeStruct((b, n2, n1 * d), BF16)],
        compiler_params=_cparams(("parallel", "parallel")),
        name="fnet_stage2",
    )(zr4, zi4, xv, c2, s2, w_f, mod6, ng)
    return x1.reshape(b, l, d), h2.reshape(b, l, d)


def _rope_tables(l):
    n_freq = HEAD_DIM // 4
    inv = ROPE_THETA ** (-jnp.arange(n_freq, dtype=F32) / n_freq)
    ang_row = jnp.arange(l // GRID_W).astype(F32)[:, None] * inv
    ang_col = jnp.arange(GRID_W).astype(F32)[:, None] * inv
    zr = jnp.zeros_like(ang_row)
    zc = jnp.zeros_like(ang_col)
    cos_r = jnp.concatenate([jnp.cos(ang_row), jnp.cos(ang_row), zr, zr], axis=1)
    sin_r = jnp.concatenate([-jnp.sin(ang_row), jnp.sin(ang_row), zr, zr], axis=1)
    cos_c = jnp.concatenate([zc, zc, jnp.cos(ang_col), jnp.cos(ang_col)], axis=1)
    sin_c = jnp.concatenate([zc, zc, -jnp.sin(ang_col), jnp.sin(ang_col)], axis=1)
    return cos_r, sin_r, cos_c, sin_c


def kernel(x, c, ctx, c_ctx, mod_w, mod_b, norm_g, attn_w_in, attn_w_out, q_norm_g, k_norm_g, sink,
           fourier_w_out, ffn_w_up, ffn_conv_w, ffn_conv_b, ffn_w_down):
    b, l, d = x.shape
    c_len = ctx.shape[1]
    assert mod_w.shape[0] == 2 and b + 1 <= COND_ROWS
    assert l % max(QKV_TM, WIN_TQ, FFN_TM, POST_TM, DFT_N1 * F1_T2) == 0

    cond = jnp.zeros((COND_ROWS, d), F32).at[:b].set(c).at[b].set(c_ctx)
    mods = _mods(cond, mod_w, mod_b)
    mod_l0 = mods[0].reshape(COND_ROWS, N_MOD, d)
    mod_l1 = mods[1].reshape(COND_ROWS, N_MOD, d)

    rope = _rope_tables(l)
    w_in = attn_w_in[0].astype(BF16)
    qg = q_norm_g[0].reshape(1, HEAD_DIM)
    kg = k_norm_g[0].reshape(1, HEAD_DIM)
    qa, ka, vat, qb, kb, vb = _qkv(x, mod_l0, lambda bi: bi, norm_g[0], w_in, qg, kg, rope, QKV_TM)
    zeros_r = jnp.zeros((8, HEAD_DIM), F32)
    zeros_c = jnp.zeros((GRID_W, HEAD_DIM), F32)
    no_rope = (jnp.ones_like(zeros_r), zeros_r, zeros_c, zeros_c)
    _, kac, vact, _, kbc, vbc = _qkv(ctx, mod_l0, lambda bi: b, norm_g[0], w_in, qg, kg, no_rope, c_len)
    oa = _attn_a(qa, kac, vact, ka, vat, ATT_TQ, ATT_TK)
    ob = _attn_b(sink[0], qb, kb, vb, kbc, vbc, WIN_TQ)
    x1, h2 = _post_attn(oa, ob, attn_w_out[0].astype(BF16), x, mod_l0, norm_g[0], POST_TM)
    x2 = _ffn(h2, x1, mod_l0, norm_g[0], ffn_w_up[0], ffn_conv_w[0], ffn_conv_b[0], ffn_w_down[0],
              FFN_TM, FFN_FC)

    csc, m1, c2, s2, twc, tws = _dft_tables(l, d)
    zr, zi = _fourier1(x2, mod_l1, norm_g[1], csc, m1, twc, tws, F1_T2)
    x3, h2 = _fourier2(zr, zi, x2, c2, s2, fourier_w_out[0].astype(BF16), mod_l1, norm_g[1], F2_K1)
    return _ffn(h2, x3, mod_l1, norm_g[1], ffn_w_up[1], ffn_conv_w[1], ffn_conv_b[1], ffn_w_down[1],
                FFN_TM, FFN_FC)
```

```python
import functools
import math

import numpy as np
import jax
import jax.numpy as jnp
from jax import lax
from jax.experimental import pallas as pl
from jax.experimental.pallas import tpu as pltpu

F32 = jnp.float32
BF16 = jnp.bfloat16

HEAD_DIM = 128
GRID_W = 64
A_HEADS = 4
A_KV_HEADS = 2
B_HEADS = 4
B_KV_HEADS = 2
GROUP = A_HEADS // A_KV_HEADS
BLOCK = 128
WINDOW = 128
ROPE_THETA = 10000.0
FNET_GROUPS = 4
N_MOD = 6
EPS = 1e-6
LOG2E = math.log2(math.e)
Q_SCALE = HEAD_DIM ** -0.5 * LOG2E
NEG_BIG = -1e30

DFT_N1 = 128
COND_ROWS = 8
VMEM_LIMIT = 56 * 1024 * 1024

QKV_TM = 512
QKV_SUB = 2
ATT_TQ = 1024
ATT_TK = 512
ATT_UNROLL = 2
V_ROWS = HEAD_DIM + 16
WIN_TQ = 1024
POST_TM = 1024
POST_SUB = 8
FFN_TM = 1024
FFN_FC = 256
FFN_UNROLL = 10
HALO = 16
F1_T2 = 8
F2_K1 = 8
F2_GROUPS = 4


def _cparams(sem):
    return pltpu.CompilerParams(dimension_semantics=sem, vmem_limit_bytes=VMEM_LIMIT)


def _rms(x, g):
    return x * lax.rsqrt(jnp.mean(x * x, axis=-1, keepdims=True) + EPS) * g


def _mod_kernel(cond_ref, w_ref, b_ref, o_ref):
    c = cond_ref[...]
    s = c / (1.0 + jnp.exp(-c))
    o_ref[0] = jnp.dot(s, w_ref[0], preferred_element_type=F32) + b_ref[0]


def _mods(cond, mod_w, mod_b):
    depth, d, n = mod_w.shape
    tn = n // 4
    return pl.pallas_call(
        _mod_kernel,
        grid=(depth, n // tn),
        in_specs=[
            pl.BlockSpec((COND_ROWS, d), lambda l, j: (0, 0)),
            pl.BlockSpec((1, d, tn), lambda l, j: (l, 0, j)),
            pl.BlockSpec((1, 1, tn), lambda l, j: (l, 0, j)),
        ],
        out_specs=pl.BlockSpec((1, COND_ROWS, tn), lambda l, j: (l, 0, j)),
        out_shape=jax.ShapeDtypeStruct((depth, COND_ROWS, n), F32),
        compiler_params=_cparams(("arbitrary", "arbitrary")),
        name="adaln_mods",
    )(cond, mod_w, mod_b.reshape(depth, 1, n))


def _qkv_kernel(x_ref, mod_ref, ng_ref, w_ref, qg_ref, kg_ref, cr_ref, sr_ref, cc_ref, sc_ref,
                qa_ref, ka_ref, vat_ref, qb_ref, kb_ref, vb_ref):
    tm = x_ref.shape[1]
    rows = tm // QKV_SUB
    assert rows % GRID_W == 0
    projs = []
    for r in range(QKV_SUB):
        x = x_ref[0, r * rows:(r + 1) * rows, :]
        h = _rms(x, ng_ref[0:1, :]) * (1.0 + mod_ref[0, 1:2, :]) + mod_ref[0, 0:1, :]
        projs.append(jnp.dot(h.astype(BF16), w_ref[...], preferred_element_type=F32))

    lane = lax.broadcasted_iota(jnp.int32, (rows, HEAD_DIM), 1)
    low_half = (lane % (HEAD_DIM // 2)) < (HEAD_DIM // 4)
    qg = qg_ref[...]
    kg = kg_ref[...]
    col_cos = cc_ref[...]
    col_sin = sc_ref[...]

    for r, proj in enumerate(projs):
        sl = slice(r * rows, (r + 1) * rows)
        g0 = r * (rows // GRID_W)

        def table(row_ref, col_part):
            return jnp.concatenate([jnp.broadcast_to(row_ref[g0 + i:g0 + i + 1, :], (GRID_W, HEAD_DIM)) + col_part
                                    for i in range(rows // GRID_W)], axis=0)

        cos = table(cr_ref, col_cos)
        sin = table(sr_ref, col_sin)

        def rope(t):
            partner = jnp.where(low_half,
                                pltpu.roll(t, HEAD_DIM - HEAD_DIM // 4, 1),
                                pltpu.roll(t, HEAD_DIM // 4, 1))
            return t * cos + partner * sin

        def head(j):
            return proj[:, j * HEAD_DIM:(j + 1) * HEAD_DIM]

        col = 0
        for hh in range(A_HEADS):
            qa_ref[0, hh, :, sl] = (rope(_rms(head(col + hh), qg)) * Q_SCALE).T.astype(BF16)
        col += A_HEADS
        for hh in range(B_HEADS):
            qb_ref[0, hh, sl, :] = (rope(head(col + hh)) * Q_SCALE).astype(BF16)
        col += B_HEADS
        for hh in range(A_KV_HEADS):
            ka_ref[0, hh, sl, :] = rope(_rms(head(col + hh), kg)).astype(BF16)
        col += A_KV_HEADS
        for hh in range(A_KV_HEADS):
            vat_ref[0, hh, 0, 0:HEAD_DIM, sl] = head(col + hh).T.astype(BF16)
            vat_ref[0, hh, 0, HEAD_DIM:, sl] = jnp.ones((V_ROWS - HEAD_DIM, rows), BF16)
        col += A_KV_HEADS
        for hh in range(B_KV_HEADS):
            kb_ref[0, hh, sl, :] = rope(head(col + hh)).astype(BF16)
        col += B_KV_HEADS
        for hh in range(B_KV_HEADS):
            vb_ref[0, hh, sl, :] = head(col + hh).astype(BF16)


def _qkv(x, mod6, mod_row, ng, w_in, qg, kg, rope, tm):
    b, l, d = x.shape
    n = w_in.shape[1]
    nt = l // tm
    assert tm % (GRID_W * QKV_SUB) == 0
    rt = max(tm // GRID_W, 8)
    return pl.pallas_call(
        _qkv_kernel,
        grid=(b, nt),
        in_specs=[
            pl.BlockSpec((1, tm, d), lambda bi, i: (bi, i, 0)),
            pl.BlockSpec((1, N_MOD, d), lambda bi, i: (mod_row(bi), 0, 0)),
            pl.BlockSpec((4, d), lambda bi, i: (0, 0)),
            pl.BlockSpec((d, n), lambda bi, i: (0, 0)),
            pl.BlockSpec((1, HEAD_DIM), lambda bi, i: (0, 0)),
            pl.BlockSpec((1, HEAD_DIM), lambda bi, i: (0, 0)),
            pl.BlockSpec((rt, HEAD_DIM), lambda bi, i: (i, 0)),
            pl.BlockSpec((rt, HEAD_DIM), lambda bi, i: (i, 0)),
            pl.BlockSpec((GRID_W, HEAD_DIM), lambda bi, i: (0, 0)),
            pl.BlockSpec((GRID_W, HEAD_DIM), lambda bi, i: (0, 0)),
        ],
        out_specs=[
            pl.BlockSpec((1, A_HEADS, HEAD_DIM, tm), lambda bi, i: (bi, 0, 0, i)),
            pl.BlockSpec((1, A_KV_HEADS, tm, HEAD_DIM), lambda bi, i: (bi, 0, i, 0)),
            pl.BlockSpec((1, A_KV_HEADS, 1, V_ROWS, tm), lambda bi, i: (bi, 0, i, 0, 0)),
            pl.BlockSpec((1, B_HEADS, tm, HEAD_DIM), lambda bi, i: (bi, 0, i, 0)),
            pl.BlockSpec((1, B_KV_HEADS, tm, HEAD_DIM), lambda bi, i: (bi, 0, i, 0)),
            pl.BlockSpec((1, B_KV_HEADS, tm, HEAD_DIM), lambda bi, i: (bi, 0, i, 0)),
        ],
        out_shape=[
            jax.ShapeDtypeStruct((b, A_HEADS, HEAD_DIM, l), BF16),
            jax.ShapeDtypeStruct((b, A_KV_HEADS, l, HEAD_DIM), BF16),
            jax.ShapeDtypeStruct((b, A_KV_HEADS, nt, V_ROWS, tm), BF16),
            jax.ShapeDtypeStruct((b, B_HEADS, l, HEAD_DIM), BF16),
            jax.ShapeDtypeStruct((b, B_KV_HEADS, l, HEAD_DIM), BF16),
            jax.ShapeDtypeStruct((b, B_KV_HEADS, l, HEAD_DIM), BF16),
        ],
        compiler_params=_cparams(("parallel", "parallel")),
        name="qkv_proj",
    )(x, mod6, ng, w_in, qg, kg, *rope)


def _attn_a_kernel(q_ref, kc_ref, vct_ref, k_ref, vt_ref, o_ref,
                   sc_ref, pc_ref, s0_ref, s1_ref, p0_ref, p1_ref, acc_ref):
    tq = q_ref.shape[3]
    ts = vt_ref.shape[4]
    tk = s0_ref.shape[0]
    sub = tk // ts
    n_chunks = vt_ref.shape[2] // sub
    assert n_chunks >= 2 and ATT_UNROLL % 2 == 0
    qt = jnp.concatenate([q_ref[0, g] for g in range(GROUP)], axis=1)

    def k_chunk(j):
        return k_ref[0, 0, pl.ds(pl.multiple_of(j * tk, tk), tk), :]

    def scores(kblk, s_ref):
        s = jnp.dot(kblk, qt, preferred_element_type=F32)
        s_ref[...] = s
        return jnp.max(s, axis=0, keepdims=True)

    def softmax(s_ref, p_ref, mc, m):
        m_new = jnp.maximum(m, mc)
        p_ref[...] = jnp.exp2(s_ref[...] - m_new).astype(BF16)
        return m_new, jnp.exp2(m - m_new)

    def accumulate(vt_blocks, p_ref, alpha):
        pv = None
        for u, vtblk in enumerate(vt_blocks):
            part = jnp.dot(vtblk, p_ref[u * vtblk.shape[1]:(u + 1) * vtblk.shape[1], :],
                           preferred_element_type=F32)
            pv = part if pv is None else pv + part
        acc_ref[...] = alpha * acc_ref[...] + pv

    def v_chunk(j):
        return [vt_ref[0, 0, j * sub + u] for u in range(sub)]

    acc_ref[...] = jnp.zeros(acc_ref.shape, F32)
    m = jnp.full((1, GROUP * tq), NEG_BIG, F32)
    mc_ctx = scores(kc_ref[0, 0], sc_ref)
    mc0 = scores(k_chunk(0), s0_ref)
    m, alpha_ctx = softmax(sc_ref, pc_ref, mc_ctx, m)
    mc1 = scores(k_chunk(1), s1_ref)
    m, alpha0 = softmax(s0_ref, p0_ref, mc0, m)
    accumulate([vct_ref[0, 0, 0]], pc_ref, alpha_ctx)

    s_bufs = (s0_ref, s1_ref)
    p_bufs = (p0_ref, p1_ref)

    def stage(c, parity, carry, has_next):
        m, alpha_prev, mc = carry
        mc_next = scores(k_chunk(c + 1), s_bufs[1 - parity]) if has_next else None
        m, alpha = softmax(s_bufs[parity], p_bufs[parity], mc, m)
        accumulate(v_chunk(c - 1), p_bufs[1 - parity], alpha_prev)
        return m, alpha, mc_next

    unroll = ATT_UNROLL
    n_groups = (n_chunks - 2) // unroll

    def group(g, carry):
        for u in range(unroll):
            carry = stage(1 + g * unroll + u, (1 + u) % 2, carry, True)
        return carry

    carry = lax.fori_loop(0, n_groups, group, (m, alpha0, mc1))
    for c in range(1 + n_groups * unroll, n_chunks):
        carry = stage(c, c % 2, carry, c + 1 < n_chunks)
    accumulate(v_chunk(n_chunks - 1), p_bufs[(n_chunks - 1) % 2], carry[1])

    acc = acc_ref[...]
    o = (acc[:HEAD_DIM] / acc[HEAD_DIM:HEAD_DIM + 1]).T
    for g in range(GROUP):
        o_ref[0, :, g * HEAD_DIM:(g + 1) * HEAD_DIM] = o[g * tq:(g + 1) * tq].astype(o_ref.dtype)


def _attn_a(q, kc, vct, k, vt, tq, tk):
    b, _, _, l = q.shape
    c_len = kc.shape[2]
    n_stored, ts = vt.shape[2], vt.shape[4]
    assert tk % ts == 0 and l % tk == 0
    return pl.pallas_call(
        _attn_a_kernel,
        grid=(b, A_KV_HEADS, l // tq),
        in_specs=[
            pl.BlockSpec((1, GROUP, HEAD_DIM, tq), lambda bi, kv, i: (bi, kv, 0, i)),
            pl.BlockSpec((1, 1, c_len, HEAD_DIM), lambda bi, kv, i: (bi, kv, 0, 0)),
            pl.BlockSpec((1, 1, 1, V_ROWS, c_len), lambda bi, kv, i: (bi, kv, 0, 0, 0)),
            pl.BlockSpec((1, 1, l, HEAD_DIM), lambda bi, kv, i: (bi, kv, 0, 0)),
            pl.BlockSpec((1, 1, n_stored, V_ROWS, ts), lambda bi, kv, i: (bi, kv, 0, 0, 0)),
        ],
        out_specs=pl.BlockSpec((1, tq, GROUP * HEAD_DIM), lambda bi, kv, i: (bi, i, kv)),
        out_shape=jax.ShapeDtypeStruct((b, l, A_HEADS * HEAD_DIM), BF16),
        scratch_shapes=[
            pltpu.VMEM((c_len, GROUP * tq), F32),
            pltpu.VMEM((c_len, GROUP * tq), BF16),
            pltpu.VMEM((tk, GROUP * tq), F32),
            pltpu.VMEM((tk, GROUP * tq), F32),
            pltpu.VMEM((tk, GROUP * tq), BF16),
            pltpu.VMEM((tk, GROUP * tq), BF16),
            pltpu.VMEM((V_ROWS, GROUP * tq), F32),
        ],
        compiler_params=_cparams(("parallel", "parallel", "arbitrary")),
        name="attn_global",
    )(q, kc, vct, k, vt)


def _attn_b_kernel(sink_ref, q_ref, kp_ref, km_ref, kn_ref, vp_ref, vm_ref, vn_ref,
                   kc_ref, vc_ref, o_ref, *, seq_len):
    kv = pl.program_id(1)
    i = pl.program_id(2)
    tq = q_ref.shape[2]
    sub = tq // BLOCK
    band = 3 * BLOCK
    kcat = jnp.concatenate([kp_ref[0, 0], km_ref[0, 0], kn_ref[0, 0]], axis=0)
    vcat = jnp.concatenate([vp_ref[0, 0], vm_ref[0, 0], vn_ref[0, 0]], axis=0)
    kc = kc_ref[0, 0]
    vc = vc_ref[0, 0]
    rows = GROUP * BLOCK
    nt = (((1,), (1,)), ((), ()))

    r = lax.broadcasted_iota(jnp.int32, (rows, band), 0) % BLOCK
    c = lax.broadcasted_iota(jnp.int32, (rows, band), 1)
    in_window = jnp.abs(c - BLOCK - r) <= WINDOW
    head_row = lax.broadcasted_iota(jnp.int32, (rows, 1), 0) // BLOCK
    sink2 = jnp.zeros((rows, 1), F32)
    for g in range(GROUP):
        sink2 = jnp.where(head_row == g, sink_ref[kv * GROUP + g] * LOG2E, sink2)

    scores = []
    for j in range(sub):
        q2 = jnp.concatenate([q_ref[0, g, j * BLOCK:(j + 1) * BLOCK, :] for g in range(GROUP)], axis=0)
        s_loc = lax.dot_general(q2, kcat[j * BLOCK:j * BLOCK + band], nt, preferred_element_type=F32)
        s_ctx = lax.dot_general(q2, kc, nt, preferred_element_type=F32)
        scores.append((s_loc, s_ctx))
    probs = []
    for j, (s_loc, s_ctx) in enumerate(scores):
        kpos = (i * sub + (j - 1)) * BLOCK + c
        valid = in_window & (kpos >= 0) & (kpos < seq_len)
        s_loc = jnp.where(valid, s_loc, NEG_BIG)
        m = jnp.maximum(jnp.maximum(jnp.max(s_loc, axis=-1, keepdims=True),
                                    jnp.max(s_ctx, axis=-1, keepdims=True)), sink2)
        p_loc = jnp.exp2(s_loc - m)
        p_ctx = jnp.exp2(s_ctx - m)
        denom = (jnp.sum(p_loc, axis=-1, keepdims=True) + jnp.sum(p_ctx, axis=-1, keepdims=True)
                 + jnp.exp2(sink2 - m))
        probs.append((p_loc.astype(BF16), p_ctx.astype(BF16), denom))
    for j, (p_loc, p_ctx, denom) in enumerate(probs):
        o = (jnp.dot(p_ctx, vc, preferred_element_type=F32)
             + jnp.dot(p_loc, vcat[j * BLOCK:j * BLOCK + band], preferred_element_type=F32)) / denom
        for g in range(GROUP):
            o_ref[0, j * BLOCK:(j + 1) * BLOCK, g * HEAD_DIM:(g + 1) * HEAD_DIM] = (
                o[g * BLOCK:(g + 1) * BLOCK].astype(o_ref.dtype))


def _attn_b(sink, q, k, v, kc, vc, tq):
    b, _, l, _ = q.shape
    c_len = kc.shape[2]
    sub = tq // BLOCK
    nb = l // BLOCK

    def main(bi, kv, i):
        return (bi, kv, i, 0)

    def prev(bi, kv, i):
        return (bi, kv, jnp.maximum(i * sub - 1, 0), 0)

    def nxt(bi, kv, i):
        return (bi, kv, jnp.minimum((i + 1) * sub, nb - 1), 0)

    def ctx_map(bi, kv, i):
        return (bi, kv, 0, 0)

    edge = lambda fn: pl.BlockSpec((1, 1, BLOCK, HEAD_DIM), fn)
    return pl.pallas_call(
        functools.partial(_attn_b_kernel, seq_len=l),
        grid=(b, B_KV_HEADS, l // tq),
        in_specs=[
            pl.BlockSpec(memory_space=pltpu.SMEM),
            pl.BlockSpec((1, GROUP, tq, HEAD_DIM), main),
            edge(prev), pl.BlockSpec((1, 1, tq, HEAD_DIM), main), edge(nxt),
            edge(prev), pl.BlockSpec((1, 1, tq, HEAD_DIM), main), edge(nxt),
            pl.BlockSpec((1, 1, c_len, HEAD_DIM), ctx_map),
            pl.BlockSpec((1, 1, c_len, HEAD_DIM), ctx_map),
        ],
        out_specs=pl.BlockSpec((1, tq, GROUP * HEAD_DIM), lambda bi, kv, i: (bi, i, kv)),
        out_shape=jax.ShapeDtypeStruct((b, l, B_HEADS * HEAD_DIM), BF16),
        compiler_params=_cparams(("parallel", "parallel", "arbitrary")),
        name="attn_window",
    )(sink, q, k, k, k, v, v, v, kc, vc)


def _residual_and_next(y, x, mod_ref, ng_ref):
    x1 = x + mod_ref[0, 2:3, :] * _rms(y, ng_ref[1:2, :])
    h2 = _rms(x1, ng_ref[2:3, :]) * (1.0 + mod_ref[0, 4:5, :]) + mod_ref[0, 3:4, :]
    return x1, h2


def _post_attn_kernel(oa_ref, ob_ref, w_ref, x_ref, mod_ref, ng_ref, x1_ref, h2_ref):
    na = oa_ref.shape[2]
    rows = oa_ref.shape[1] // POST_SUB
    ys = []
    for r in range(POST_SUB):
        sl = slice(r * rows, (r + 1) * rows)
        ys.append(jnp.dot(oa_ref[0, sl, :], w_ref[0:na, :], preferred_element_type=F32)
                  + jnp.dot(ob_ref[0, sl, :], w_ref[na:, :], preferred_element_type=F32))
    for r, y in enumerate(ys):
        sl = slice(r * rows, (r + 1) * rows)
        x1, h2 = _residual_and_next(y, x_ref[0, sl, :], mod_ref, ng_ref)
        x1_ref[0, sl, :] = x1
        h2_ref[0, sl, :] = h2.astype(BF16)


def _post_attn(oa, ob, w_out, x, mod6, ng, tm):
    b, l, d = x.shape
    na, nb_ = oa.shape[2], ob.shape[2]
    return pl.pallas_call(
        _post_attn_kernel,
        grid=(b, l // tm),
        in_specs=[
            pl.BlockSpec((1, tm, na), lambda bi, i: (bi, i, 0)),
            pl.BlockSpec((1, tm, nb_), lambda bi, i: (bi, i, 0)),
            pl.BlockSpec((na + nb_, d), lambda bi, i: (0, 0)),
            pl.BlockSpec((1, tm, d), lambda bi, i: (bi, i, 0)),
            pl.BlockSpec((1, N_MOD, d), lambda bi, i: (bi, 0, 0)),
            pl.BlockSpec((4, d), lambda bi, i: (0, 0)),
        ],
        out_specs=[
            pl.BlockSpec((1, tm, d), lambda bi, i: (bi, i, 0)),
            pl.BlockSpec((1, tm, d), lambda bi, i: (bi, i, 0)),
        ],
        out_shape=[jax.ShapeDtypeStruct((b, l, d), F32), jax.ShapeDtypeStruct((b, l, d), BF16)],
        compiler_params=_cparams(("parallel", "parallel")),
        name="attn_out_proj",
    )(oa, ob, w_out, x, mod6, ng)


def _ffn_kernel(*refs, fused_input, emit_next):
    refs = list(refs)
    if fused_input:
        xp_ref, xm_ref, xn_ref, rp_ref, rm_ref, rn_ref = refs[:6]
        refs = refs[6:]
    else:
        hp_ref, hm_ref, hn_ref, xm_ref = refs[:4]
        refs = refs[4:]
    mod_ref, ng_ref, wu_ref, cw_ref, cb_ref, wd_ref = refs[:6]
    refs = refs[6:]
    if emit_next:
        modn_ref, ngn_ref = refs[:2]
        refs = refs[2:]
    o_ref = refs[0]
    refs = refs[1:]
    if emit_next:
        hnext_ref = refs[0]
        refs = refs[1:]
    hcat_ref, ug0_ref, uv0_ref, ug1_ref, uv1_ref, act_ref = refs

    i = pl.program_id(1)
    last = pl.num_programs(1) - 1
    tm = xm_ref.shape[1]
    f = wd_ref.shape[0]
    fc = ug0_ref.shape[1]
    n_chunks = f // fc
    u_bufs = ((ug0_ref, uv0_ref), (ug1_ref, uv1_ref))

    def pre_norm(x):
        return (_rms(x, ng_ref[2:3, :]) * (1.0 + mod_ref[0, 4:5, :]) + mod_ref[0, 3:4, :]).astype(BF16)

    if fused_input:
        x_in = xm_ref[0] + rm_ref[0].astype(F32)
        hp = pre_norm(xp_ref[0] + rp_ref[0].astype(F32))
        hm = pre_norm(x_in)
        hn = pre_norm(xn_ref[0] + rn_ref[0].astype(F32))
    else:
        x_in = xm_ref[0]
        hp, hm, hn = hp_ref[0], hm_ref[0], hn_ref[0]
    hcat_ref[0:HALO, :] = jnp.where(i > 0, hp, jnp.zeros_like(hp))
    hcat_ref[HALO:HALO + tm, :] = hm
    hcat_ref[HALO + tm:, :] = jnp.where(i < last, hn, jnp.zeros_like(hn))

    def cols(c, half):
        return pl.ds(pl.multiple_of(half * f + c * fc, fc), fc)

    def conv(u_ref, c, half):
        cw = cw_ref[:, cols(c, half)]
        u = u_ref[...]
        rows = u.shape[0]
        prev = pltpu.roll(u, 1, 0)[HALO:HALO + tm]
        nxt = pltpu.roll(u, rows - 1, 0)[HALO:HALO + tm]
        return (prev * cw[0:1, :] + u[HALO:HALO + tm] * cw[1:2, :] + nxt * cw[2:3, :]
                + cb_ref[:, cols(c, half)])

    def up(c, bufs):
        hcat = hcat_ref[...]
        for half in range(2):
            bufs[half][...] = jnp.dot(hcat, wu_ref[:, cols(c, half)], preferred_element_type=F32)

    def mid(c, bufs):
        gate = conv(bufs[0], c, 0)
        val = conv(bufs[1], c, 1)
        act_ref[:, cols(c, 0)] = (gate / (1.0 + jnp.exp(-gate)) * val).astype(BF16)

    def stage(c, parity, do_up):
        if do_up:
            up(c + 1, u_bufs[1 - parity])
        mid(c, u_bufs[parity])

    up(0, u_bufs[0])
    unroll = FFN_UNROLL
    n_groups = (n_chunks - 1) // unroll

    def group(g, carry):
        for u in range(unroll):
            stage(g * unroll + u, u % 2, True)
        return carry

    lax.fori_loop(0, n_groups, group, 0)
    for c in range(n_groups * unroll, n_chunks):
        stage(c, c % 2, c + 1 < n_chunks)
    y = jnp.dot(act_ref[...], wd_ref[...], preferred_element_type=F32)
    out = x_in + mod_ref[0, 5:6, :] * _rms(y, ng_ref[3:4, :])
    o_ref[0] = out
    if emit_next:
        hnext_ref[0] = (_rms(out, ngn_ref[0:1, :]) * (1.0 + modn_ref[0, 1:2, :]) + modn_ref[0, 0:1, :]).astype(BF16)


def _ffn(x, mod6, ng, w_up, conv_w, conv_b, w_down, tm, fc, *, h2=None, branch=None, next_mod=None, next_ng=None):
    b, l, d = x.shape
    f = w_down.shape[0]
    assert f % fc == 0 and (h2 is None) != (branch is None)
    per = tm // HALO
    n_halo = l // HALO
    resident = lambda shape: pl.BlockSpec(shape, lambda bi, i: (0,) * len(shape),
                                          pipeline_mode=pl.Buffered(1))
    prev_halo = pl.BlockSpec((1, HALO, d), lambda bi, i: (bi, jnp.maximum(i * per - 1, 0), 0))
    next_halo = pl.BlockSpec((1, HALO, d), lambda bi, i: (bi, jnp.minimum((i + 1) * per, n_halo - 1), 0))
    tile = pl.BlockSpec((1, tm, d), lambda bi, i: (bi, i, 0))
    mod_spec = pl.BlockSpec((1, N_MOD, d), lambda bi, i: (bi, 0, 0))
    ng_spec = pl.BlockSpec((4, d), lambda bi, i: (0, 0))
    if branch is None:
        operands, in_specs = [h2, h2, h2, x], [prev_halo, tile, next_halo, tile]
    else:
        operands, in_specs = [x, x, x, branch, branch, branch], [prev_halo, tile, next_halo] * 2
    operands += [mod6, ng, w_up.astype(BF16), conv_w, conv_b.reshape(1, 2 * f), w_down.astype(BF16)]
    in_specs += [mod_spec, ng_spec, resident((d, 2 * f)), resident((3, 2 * f)), resident((1, 2 * f)), resident((f, d))]
    out_specs, out_shape = [tile], [jax.ShapeDtypeStruct((b, l, d), F32)]
    if next_mod is not None:
        operands += [next_mod, next_ng]
        in_specs += [mod_spec, ng_spec]
        out_specs.append(tile)
        out_shape.append(jax.ShapeDtypeStruct((b, l, d), BF16))
    outs = pl.pallas_call(
        functools.partial(_ffn_kernel, fused_input=branch is not None, emit_next=next_mod is not None),
        grid=(b, l // tm),
        in_specs=in_specs,
        out_specs=out_specs,
        out_shape=out_shape,
        scratch_shapes=[
            pltpu.VMEM((tm + 2 * HALO, d), BF16),
            pltpu.VMEM((tm + 2 * HALO, fc), F32),
            pltpu.VMEM((tm + 2 * HALO, fc), F32),
            pltpu.VMEM((tm + 2 * HALO, fc), F32),
            pltpu.VMEM((tm + 2 * HALO, fc), F32),
            pltpu.VMEM((tm, f), BF16),
        ],
        compiler_params=_cparams(("parallel", "arbitrary")),
        name="conv_ffn",
    )(*operands)
    return outs if next_mod is not None else outs[0]


def _dft_tables(l, d):
    n1, n2, cg = DFT_N1, l // DFT_N1, d // FNET_GROUPS
    a = np.arange(cg)
    ang_c = 2.0 * np.pi * np.outer(a, a) / cg
    csc = np.concatenate([np.cos(ang_c), np.sin(ang_c)], axis=1) / math.sqrt(cg)
    a1 = np.arange(n1)
    ang1 = 2.0 * np.pi * np.outer(a1, a1) / n1
    c1, s1 = np.cos(ang1), np.sin(ang1)
    m1 = np.block([[c1, -s1], [s1, c1]])
    a2 = np.arange(n2)
    ang2 = 2.0 * np.pi * np.outer(a2, a2) / n2
    c2 = np.cos(ang2) / math.sqrt(l)
    s2 = np.sin(ang2) / math.sqrt(l)
    ang_t = 2.0 * np.pi * np.outer(a1, a2) / l
    f32 = lambda t: jnp.asarray(t, F32)
    return (f32(csc).astype(BF16), f32(m1).astype(BF16), f32(c2).astype(BF16), f32(s2).astype(BF16),
            f32(np.cos(ang_t)), f32(np.sin(ang_t)))


def _fourier1_kernel(h_ref, csc_ref, m1_ref, twc_ref, tws_ref, zr_ref, zi_ref, *, d):
    cg = csc_ref.shape[0]
    n1 = h_ref.shape[1]
    t2s = h_ref.shape[2] // d
    csc = csc_ref[...]
    m1 = m1_ref[...]
    stacked = []
    for t in range(t2s):
        h = h_ref[0, :, t * d:(t + 1) * d]
        ab = [jnp.dot(h[:, g * cg:(g + 1) * cg], csc, preferred_element_type=F32)
              for g in range(d // cg)]
        a = jnp.concatenate([t_[:, :cg] for t_ in ab], axis=1)
        bm = jnp.concatenate([t_[:, cg:] for t_ in ab], axis=1)
        stacked.append(jnp.concatenate([a, bm], axis=0).astype(BF16))
    ys = [jnp.dot(m1, ab2, preferred_element_type=F32) for ab2 in stacked]
    for t, y in enumerate(ys):
        yr = y[:n1]
        yin = y[n1:]
        cc = jnp.tile(twc_ref[:, t * HEAD_DIM:(t + 1) * HEAD_DIM], (1, d // HEAD_DIM))
        ss = jnp.tile(tws_ref[:, t * HEAD_DIM:(t + 1) * HEAD_DIM], (1, d // HEAD_DIM))
        zr_ref[0, :, t * d:(t + 1) * d] = (yr * cc - yin * ss).astype(BF16)
        zi_ref[0, :, t * d:(t + 1) * d] = (-(yin * cc) - yr * ss).astype(BF16)


def _fourier1(h, csc, m1, twc, tws, t2s):
    b, l, d = h.shape
    n1, n2 = DFT_N1, l // DFT_N1
    hv = h.reshape(b, n1, n2 * d)
    lanes = HEAD_DIM
    twc_x = jnp.repeat(twc, lanes, axis=1)
    tws_x = jnp.repeat(tws, lanes, axis=1)
    cg = csc.shape[0]
    return pl.pallas_call(
        functools.partial(_fourier1_kernel, d=d),
        grid=(b, n2 // t2s),
        in_specs=[
            pl.BlockSpec((1, n1, t2s * d), lambda bi, j: (bi, 0, j)),
            pl.BlockSpec((cg, 2 * cg), lambda bi, j: (0, 0)),
            pl.BlockSpec((2 * n1, 2 * n1), lambda bi, j: (0, 0)),
            pl.BlockSpec((n1, t2s * lanes), lambda bi, j: (0, j)),
            pl.BlockSpec((n1, t2s * lanes), lambda bi, j: (0, j)),
        ],
        out_specs=[
            pl.BlockSpec((1, n1, t2s * d), lambda bi, j: (bi, 0, j)),
            pl.BlockSpec((1, n1, t2s * d), lambda bi, j: (bi, 0, j)),
        ],
        out_shape=[jax.ShapeDtypeStruct((b, n1, n2 * d), BF16)] * 2,
        compiler_params=_cparams(("parallel", "parallel")),
        name="fnet_stage1",
    )(hv, csc, m1, twc_x, tws_x)


def _fourier2_kernel(zr_ref, zi_ref, c2_ref, s2_ref, w_ref, mod_ref, ng_ref, r_ref):
    k1s = zr_ref.shape[1]
    n2 = zr_ref.shape[2]
    d = zr_ref.shape[3]
    c2 = c2_ref[...]
    s2 = s2_ref[...]
    fs = []
    for k in range(k1s):
        f = (jnp.dot(c2, zr_ref[0, k], preferred_element_type=F32)
             + jnp.dot(s2, zi_ref[0, k], preferred_element_type=F32))
        fs.append(f.astype(BF16))
    per = k1s // F2_GROUPS
    ys = [jnp.dot(jnp.concatenate(fs[g * per:(g + 1) * per], axis=0), w_ref[...], preferred_element_type=F32)
          for g in range(F2_GROUPS)]
    for k in range(k1s):
        y = ys[k // per][(k % per) * n2:(k % per + 1) * n2]
        r_ref[0, :, k * d:(k + 1) * d] = (mod_ref[0, 2:3, :] * _rms(y, ng_ref[1:2, :])).astype(BF16)


def _fourier2(zr, zi, c2, s2, w_f, mod6, ng, k1s):
    b, n1, nd = zr.shape
    n2 = c2.shape[0]
    d = nd // n2
    zr4 = zr.reshape(b, n1, n2, d)
    zi4 = zi.reshape(b, n1, n2, d)
    r = pl.pallas_call(
        _fourier2_kernel,
        grid=(b, n1 // k1s),
        in_specs=[
            pl.BlockSpec((1, k1s, n2, d), lambda bi, j: (bi, j, 0, 0)),
            pl.BlockSpec((1, k1s, n2, d), lambda bi, j: (bi, j, 0, 0)),
            pl.BlockSpec((n2, n2), lambda bi, j: (0, 0)),
            pl.BlockSpec((n2, n2), lambda bi, j: (0, 0)),
            pl.BlockSpec((d, d), lambda bi, j: (0, 0)),
            pl.BlockSpec((1, N_MOD, d), lambda bi, j: (bi, 0, 0)),
            pl.BlockSpec((4, d), lambda bi, j: (0, 0)),
        ],
        out_specs=pl.BlockSpec((1, n2, k1s * d), lambda bi, j: (bi, 0, j)),
        out_shape=jax.ShapeDtypeStruct((b, n2, n1 * d), BF16),
        compiler_params=_cparams(("parallel", "parallel")),
        name="fnet_stage2",
    )(zr4, zi4, c2, s2, w_f, mod6, ng)
    return r.reshape(b, n2 * n1, d)


def _rope_tables(l):
    n_freq = HEAD_DIM // 4
    inv = ROPE_THETA ** (-jnp.arange(n_freq, dtype=F32) / n_freq)
    ang_row = jnp.arange(l // GRID_W).astype(F32)[:, None] * inv
    ang_col = jnp.arange(GRID_W).astype(F32)[:, None] * inv
    zr = jnp.zeros_like(ang_row)
    zc = jnp.zeros_like(ang_col)
    cos_r = jnp.concatenate([jnp.cos(ang_row), jnp.cos(ang_row), zr, zr], axis=1)
    sin_r = jnp.concatenate([-jnp.sin(ang_row), jnp.sin(ang_row), zr, zr], axis=1)
    cos_c = jnp.concatenate([zc, zc, jnp.cos(ang_col), jnp.cos(ang_col)], axis=1)
    sin_c = jnp.concatenate([zc, zc, -jnp.sin(ang_col), jnp.sin(ang_col)], axis=1)
    return cos_r, sin_r, cos_c, sin_c


def kernel(x, c, ctx, c_ctx, mod_w, mod_b, norm_g, attn_w_in, attn_w_out, q_norm_g, k_norm_g, sink,
           fourier_w_out, ffn_w_up, ffn_conv_w, ffn_conv_b, ffn_w_down):
    b, l, d = x.shape
    c_len = ctx.shape[1]
    assert mod_w.shape[0] == 2 and b + 1 <= COND_ROWS
    assert l % max(QKV_TM, WIN_TQ, FFN_TM, POST_TM, DFT_N1 * F1_T2) == 0

    cond = jnp.zeros((COND_ROWS, d), F32).at[:b].set(c).at[b].set(c_ctx)
    mods = _mods(cond, mod_w, mod_b)
    mod_l0 = mods[0].reshape(COND_ROWS, N_MOD, d)
    mod_l1 = mods[1].reshape(COND_ROWS, N_MOD, d)

    rope = _rope_tables(l)
    w_in = attn_w_in[0].astype(BF16)
    qg = q_norm_g[0].reshape(1, HEAD_DIM)
    kg = k_norm_g[0].reshape(1, HEAD_DIM)
    qa, ka, vat, qb, kb, vb = _qkv(x, mod_l0, lambda bi: bi, norm_g[0], w_in, qg, kg, rope, QKV_TM)
    zeros_r = jnp.zeros((8, HEAD_DIM), F32)
    zeros_c = jnp.zeros((GRID_W, HEAD_DIM), F32)
    no_rope = (jnp.ones_like(zeros_r), zeros_r, zeros_c, zeros_c)
    _, kac, vact, _, kbc, vbc = _qkv(ctx, mod_l0, lambda bi: b, norm_g[0], w_in, qg, kg, no_rope, c_len)
    oa = _attn_a(qa, kac, vact, ka, vat, ATT_TQ, ATT_TK)
    ob = _attn_b(sink[0], qb, kb, vb, kbc, vbc, WIN_TQ)
    x1, h2 = _post_attn(oa, ob, attn_w_out[0].astype(BF16), x, mod_l0, norm_g[0], POST_TM)
    x2, h_next = _ffn(x1, mod_l0, norm_g[0], ffn_w_up[0], ffn_conv_w[0], ffn_conv_b[0], ffn_w_down[0],
                      FFN_TM, FFN_FC, h2=h2, next_mod=mod_l1, next_ng=norm_g[1])

    csc, m1, c2, s2, twc, tws = _dft_tables(l, d)
    zr, zi = _fourier1(h_next, csc, m1, twc, tws, F1_T2)
    branch = _fourier2(zr, zi, c2, s2, fourier_w_out[0].astype(BF16), mod_l1, norm_g[1], F2_K1)
    return _ffn(x2, mod_l1, norm_g[1], ffn_w_up[1], ffn_conv_w[1], ffn_conv_b[1], ffn_w_down[1],
                FFN_TM, FFN_FC, branch=branch)
```

```python
import functools
import math

import numpy as np
import jax
import jax.numpy as jnp
from jax import lax
from jax.experimental import pallas as pl
from jax.experimental.pallas import tpu as pltpu

F32 = jnp.float32
BF16 = jnp.bfloat16

HEAD_DIM = 128
GRID_W = 64
A_HEADS = 4
A_KV_HEADS = 2
B_HEADS = 4
B_KV_HEADS = 2
GROUP = A_HEADS // A_KV_HEADS
BLOCK = 128
WINDOW = 128
ROPE_THETA = 10000.0
FNET_GROUPS = 4
N_MOD = 6
EPS = 1e-6
LOG2E = math.log2(math.e)
Q_SCALE = HEAD_DIM ** -0.5 * LOG2E
NEG_BIG = -1e30

DFT_N1 = 128
COND_ROWS = 8
VMEM_LIMIT = 56 * 1024 * 1024

QKV_TM = 512
QKV_SUB = 2
ATT_TQ = 2048
ATT_TK = 512
ATT_UNROLL = 2
V_ROWS = HEAD_DIM + 16
WIN_TQ = 1024
POST_TM = 1024
POST_SUB = 8
FFN_TM = 1024
FFN_FC = 256
FFN_UNROLL = 10
HALO = 16
F1_T2 = 8
F2_K1 = 8
F2_GROUPS = 4


def _cparams(sem):
    return pltpu.CompilerParams(dimension_semantics=sem, vmem_limit_bytes=VMEM_LIMIT)


def _rms(x, g):
    return x * lax.rsqrt(jnp.mean(x * x, axis=-1, keepdims=True) + EPS) * g


def _mod_kernel(cond_ref, w_ref, b_ref, o_ref):
    c = cond_ref[...]
    s = c / (1.0 + jnp.exp(-c))
    o_ref[0] = jnp.dot(s, w_ref[0], preferred_element_type=F32) + b_ref[0]


def _mods(cond, mod_w, mod_b):
    depth, d, n = mod_w.shape
    tn = n // 4
    return pl.pallas_call(
        _mod_kernel,
        grid=(depth, n // tn),
        in_specs=[
            pl.BlockSpec((COND_ROWS, d), lambda l, j: (0, 0)),
            pl.BlockSpec((1, d, tn), lambda l, j: (l, 0, j)),
            pl.BlockSpec((1, 1, tn), lambda l, j: (l, 0, j)),
        ],
        out_specs=pl.BlockSpec((1, COND_ROWS, tn), lambda l, j: (l, 0, j)),
        out_shape=jax.ShapeDtypeStruct((depth, COND_ROWS, n), F32),
        compiler_params=_cparams(("arbitrary", "arbitrary")),
        name="adaln_mods",
    )(cond, mod_w, mod_b.reshape(depth, 1, n))


def _qkv_kernel(x_ref, mod_ref, ng_ref, w_ref, qg_ref, kg_ref, cr_ref, sr_ref, cc_ref, sc_ref,
                qa_ref, ka_ref, vat_ref, qb_ref, kb_ref, vb_ref):
    tm = x_ref.shape[1]
    rows = tm // QKV_SUB
    assert rows % GRID_W == 0
    projs = []
    for r in range(QKV_SUB):
        x = x_ref[0, r * rows:(r + 1) * rows, :]
        h = _rms(x, ng_ref[0:1, :]) * (1.0 + mod_ref[0, 1:2, :]) + mod_ref[0, 0:1, :]
        projs.append(jnp.dot(h.astype(BF16), w_ref[...], preferred_element_type=F32))

    lane = lax.broadcasted_iota(jnp.int32, (rows, HEAD_DIM), 1)
    low_half = (lane % (HEAD_DIM // 2)) < (HEAD_DIM // 4)
    qg = qg_ref[...]
    kg = kg_ref[...]
    col_cos = cc_ref[...]
    col_sin = sc_ref[...]

    for r, proj in enumerate(projs):
        sl = slice(r * rows, (r + 1) * rows)
        g0 = r * (rows // GRID_W)

        def table(row_ref, col_part):
            return jnp.concatenate([jnp.broadcast_to(row_ref[g0 + i:g0 + i + 1, :], (GRID_W, HEAD_DIM)) + col_part
                                    for i in range(rows // GRID_W)], axis=0)

        cos = table(cr_ref, col_cos)
        sin = table(sr_ref, col_sin)

        def rope(t):
            partner = jnp.where(low_half,
                                pltpu.roll(t, HEAD_DIM - HEAD_DIM // 4, 1),
                                pltpu.roll(t, HEAD_DIM // 4, 1))
            return t * cos + partner * sin

        def head(j):
            return proj[:, j * HEAD_DIM:(j + 1) * HEAD_DIM]

        col = 0
        for hh in range(A_HEADS):
            qa_ref[0, hh, :, sl] = (rope(_rms(head(col + hh), qg)) * Q_SCALE).T.astype(BF16)
        col += A_HEADS
        for hh in range(B_HEADS):
            qb_ref[0, hh, sl, :] = (rope(head(col + hh)) * Q_SCALE).astype(BF16)
        col += B_HEADS
        for hh in range(A_KV_HEADS):
            ka_ref[0, hh, sl, :] = rope(_rms(head(col + hh), kg)).astype(BF16)
        col += A_KV_HEADS
        for hh in range(A_KV_HEADS):
            vat_ref[0, hh, 0, 0:HEAD_DIM, sl] = head(col + hh).T.astype(BF16)
            vat_ref[0, hh, 0, HEAD_DIM:, sl] = jnp.ones((V_ROWS - HEAD_DIM, rows), BF16)
        col += A_KV_HEADS
        for hh in range(B_KV_HEADS):
            kb_ref[0, hh, sl, :] = rope(head(col + hh)).astype(BF16)
        col += B_KV_HEADS
        for hh in range(B_KV_HEADS):
            vb_ref[0, hh, sl, :] = head(col + hh).astype(BF16)


def _qkv(x, mod6, mod_row, ng, w_in, qg, kg, rope, tm):
    b, l, d = x.shape
    n = w_in.shape[1]
    nt = l // tm
    assert tm % (GRID_W * QKV_SUB) == 0
    rt = max(tm // GRID_W, 8)
    return pl.pallas_call(
        _qkv_kernel,
        grid=(b, nt),
        in_specs=[
            pl.BlockSpec((1, tm, d), lambda bi, i: (bi, i, 0)),
            pl.BlockSpec((1, N_MOD, d), lambda bi, i: (mod_row(bi), 0, 0)),
            pl.BlockSpec((4, d), lambda bi, i: (0, 0)),
            pl.BlockSpec((d, n), lambda bi, i: (0, 0)),
            pl.BlockSpec((1, HEAD_DIM), lambda bi, i: (0, 0)),
            pl.BlockSpec((1, HEAD_DIM), lambda bi, i: (0, 0)),
            pl.BlockSpec((rt, HEAD_DIM), lambda bi, i: (i, 0)),
            pl.BlockSpec((rt, HEAD_DIM), lambda bi, i: (i, 0)),
            pl.BlockSpec((GRID_W, HEAD_DIM), lambda bi, i: (0, 0)),
            pl.BlockSpec((GRID_W, HEAD_DIM), lambda bi, i: (0, 0)),
        ],
        out_specs=[
            pl.BlockSpec((1, A_HEADS, HEAD_DIM, tm), lambda bi, i: (bi, 0, 0, i)),
            pl.BlockSpec((1, A_KV_HEADS, tm, HEAD_DIM), lambda bi, i: (bi, 0, i, 0)),
            pl.BlockSpec((1, A_KV_HEADS, 1, V_ROWS, tm), lambda bi, i: (bi, 0, i, 0, 0)),
            pl.BlockSpec((1, B_HEADS, tm, HEAD_DIM), lambda bi, i: (bi, 0, i, 0)),
            pl.BlockSpec((1, B_KV_HEADS, tm, HEAD_DIM), lambda bi, i: (bi, 0, i, 0)),
            pl.BlockSpec((1, B_KV_HEADS, tm, HEAD_DIM), lambda bi, i: (bi, 0, i, 0)),
        ],
        out_shape=[
            jax.ShapeDtypeStruct((b, A_HEADS, HEAD_DIM, l), BF16),
            jax.ShapeDtypeStruct((b, A_KV_HEADS, l, HEAD_DIM), BF16),
            jax.ShapeDtypeStruct((b, A_KV_HEADS, nt, V_ROWS, tm), BF16),
            jax.ShapeDtypeStruct((b, B_HEADS, l, HEAD_DIM), BF16),
            jax.ShapeDtypeStruct((b, B_KV_HEADS, l, HEAD_DIM), BF16),
            jax.ShapeDtypeStruct((b, B_KV_HEADS, l, HEAD_DIM), BF16),
        ],
        compiler_params=_cparams(("parallel", "parallel")),
        name="qkv_proj",
    )(x, mod6, ng, w_in, qg, kg, *rope)


def _attn_a_kernel(q_ref, kc_ref, vct_ref, k_ref, vt_ref, o_ref,
                   sc_ref, pc_ref, s0_ref, s1_ref, p0_ref, p1_ref, acc_ref):
    tq = q_ref.shape[3]
    ts = vt_ref.shape[4]
    tk = s0_ref.shape[0]
    sub = tk // ts
    n_chunks = vt_ref.shape[2] // sub
    assert n_chunks >= 2 and ATT_UNROLL % 2 == 0
    qt = jnp.concatenate([q_ref[0, g] for g in range(GROUP)], axis=1)

    def k_chunk(j):
        return k_ref[0, 0, pl.ds(pl.multiple_of(j * tk, tk), tk), :]

    def scores(kblk, s_ref):
        s = jnp.dot(kblk, qt, preferred_element_type=F32)
        s_ref[...] = s
        return jnp.max(s, axis=0, keepdims=True)

    def softmax(s_ref, p_ref, mc, m):
        m_new = jnp.maximum(m, mc)
        p_ref[...] = jnp.exp2(s_ref[...] - m_new).astype(BF16)
        return m_new, jnp.exp2(m - m_new)

    def accumulate(vt_blocks, p_ref, alpha):
        pv = None
        for u, vtblk in enumerate(vt_blocks):
            part = jnp.dot(vtblk, p_ref[u * vtblk.shape[1]:(u + 1) * vtblk.shape[1], :],
                           preferred_element_type=F32)
            pv = part if pv is None else pv + part
        acc_ref[...] = alpha * acc_ref[...] + pv

    def v_chunk(j):
        return [vt_ref[0, 0, j * sub + u] for u in range(sub)]

    acc_ref[...] = jnp.zeros(acc_ref.shape, F32)
    m = jnp.full((1, GROUP * tq), NEG_BIG, F32)
    mc_ctx = scores(kc_ref[0, 0], sc_ref)
    mc0 = scores(k_chunk(0), s0_ref)
    m, alpha_ctx = softmax(sc_ref, pc_ref, mc_ctx, m)
    mc1 = scores(k_chunk(1), s1_ref)
    m, alpha0 = softmax(s0_ref, p0_ref, mc0, m)
    accumulate([vct_ref[0, 0, 0]], pc_ref, alpha_ctx)

    s_bufs = (s0_ref, s1_ref)
    p_bufs = (p0_ref, p1_ref)

    def stage(c, parity, carry, has_next):
        m, alpha_prev, mc = carry
        mc_next = scores(k_chunk(c + 1), s_bufs[1 - parity]) if has_next else None
        m, alpha = softmax(s_bufs[parity], p_bufs[parity], mc, m)
        accumulate(v_chunk(c - 1), p_bufs[1 - parity], alpha_prev)
        return m, alpha, mc_next

    unroll = ATT_UNROLL
    n_groups = (n_chunks - 2) // unroll

    def group(g, carry):
        for u in range(unroll):
            carry = stage(1 + g * unroll + u, (1 + u) % 2, carry, True)
        return carry

    carry = lax.fori_loop(0, n_groups, group, (m, alpha0, mc1))
    for c in range(1 + n_groups * unroll, n_chunks):
        carry = stage(c, c % 2, carry, c + 1 < n_chunks)
    accumulate(v_chunk(n_chunks - 1), p_bufs[(n_chunks - 1) % 2], carry[1])

    acc = acc_ref[...]
    o = (acc[:HEAD_DIM] / acc[HEAD_DIM:HEAD_DIM + 1]).T
    for g in range(GROUP):
        o_ref[0, :, g * HEAD_DIM:(g + 1) * HEAD_DIM] = o[g * tq:(g + 1) * tq].astype(o_ref.dtype)


def _attn_a(q, kc, vct, k, vt, tq, tk):
    b, _, _, l = q.shape
    c_len = kc.shape[2]
    n_stored, ts = vt.shape[2], vt.shape[4]
    assert tk % ts == 0 and l % tk == 0
    return pl.pallas_call(
        _attn_a_kernel,
        grid=(b, A_KV_HEADS, l // tq),
        in_specs=[
            pl.BlockSpec((1, GROUP, HEAD_DIM, tq), lambda bi, kv, i: (bi, kv, 0, i)),
            pl.BlockSpec((1, 1, c_len, HEAD_DIM), lambda bi, kv, i: (bi, kv, 0, 0)),
            pl.BlockSpec((1, 1, 1, V_ROWS, c_len), lambda bi, kv, i: (bi, kv, 0, 0, 0)),
            pl.BlockSpec((1, 1, l, HEAD_DIM), lambda bi, kv, i: (bi, kv, 0, 0)),
            pl.BlockSpec((1, 1, n_stored, V_ROWS, ts), lambda bi, kv, i: (bi, kv, 0, 0, 0)),
        ],
        out_specs=pl.BlockSpec((1, tq, GROUP * HEAD_DIM), lambda bi, kv, i: (bi, i, kv)),
        out_shape=jax.ShapeDtypeStruct((b, l, A_HEADS * HEAD_DIM), BF16),
        scratch_shapes=[
            pltpu.VMEM((c_len, GROUP * tq), F32),
            pltpu.VMEM((c_len, GROUP * tq), BF16),
            pltpu.VMEM((tk, GROUP * tq), F32),
            pltpu.VMEM((tk, GROUP * tq), F32),
            pltpu.VMEM((tk, GROUP * tq), BF16),
            pltpu.VMEM((tk, GROUP * tq), BF16),
            pltpu.VMEM((V_ROWS, GROUP * tq), F32),
        ],
        compiler_params=_cparams(("parallel", "parallel", "arbitrary")),
        name="attn_global",
    )(q, kc, vct, k, vt)


def _attn_b_kernel(sink_ref, q_ref, kp_ref, km_ref, kn_ref, vp_ref, vm_ref, vn_ref,
                   kc_ref, vc_ref, o_ref, *, seq_len):
    kv = pl.program_id(1)
    i = pl.program_id(2)
    tq = q_ref.shape[2]
    sub = tq // BLOCK
    band = 3 * BLOCK
    kcat = jnp.concatenate([kp_ref[0, 0], km_ref[0, 0], kn_ref[0, 0]], axis=0)
    vcat = jnp.concatenate([vp_ref[0, 0], vm_ref[0, 0], vn_ref[0, 0]], axis=0)
    kc = kc_ref[0, 0]
    vc = vc_ref[0, 0]
    rows = GROUP * BLOCK
    nt = (((1,), (1,)), ((), ()))

    r = lax.broadcasted_iota(jnp.int32, (rows, band), 0) % BLOCK
    c = lax.broadcasted_iota(jnp.int32, (rows, band), 1)
    in_window = jnp.abs(c - BLOCK - r) <= WINDOW
    head_row = lax.broadcasted_iota(jnp.int32, (rows, 1), 0) // BLOCK
    sink2 = jnp.zeros((rows, 1), F32)
    for g in range(GROUP):
        sink2 = jnp.where(head_row == g, sink_ref[kv * GROUP + g] * LOG2E, sink2)

    scores = []
    for j in range(sub):
        q2 = jnp.concatenate([q_ref[0, g, j * BLOCK:(j + 1) * BLOCK, :] for g in range(GROUP)], axis=0)
        s_loc = lax.dot_general(q2, kcat[j * BLOCK:j * BLOCK + band], nt, preferred_element_type=F32)
        s_ctx = lax.dot_general(q2, kc, nt, preferred_element_type=F32)
        scores.append((s_loc, s_ctx))
    probs = []
    for j, (s_loc, s_ctx) in enumerate(scores):
        kpos = (i * sub + (j - 1)) * BLOCK + c
        valid = in_window & (kpos >= 0) & (kpos < seq_len)
        s_loc = jnp.where(valid, s_loc, NEG_BIG)
        m = jnp.maximum(jnp.maximum(jnp.max(s_loc, axis=-1, keepdims=True),
                                    jnp.max(s_ctx, axis=-1, keepdims=True)), sink2)
        p_loc = jnp.exp2(s_loc - m)
        p_ctx = jnp.exp2(s_ctx - m)
        denom = (jnp.sum(p_loc, axis=-1, keepdims=True) + jnp.sum(p_ctx, axis=-1, keepdims=True)
                 + jnp.exp2(sink2 - m))
        probs.append((p_loc.astype(BF16), p_ctx.astype(BF16), denom))
    for j, (p_loc, p_ctx, denom) in enumerate(probs):
        o = (jnp.dot(p_ctx, vc, preferred_element_type=F32)
             + jnp.dot(p_loc, vcat[j * BLOCK:j * BLOCK + band], preferred_element_type=F32)) / denom
        for g in range(GROUP):
            o_ref[0, j * BLOCK:(j + 1) * BLOCK, g * HEAD_DIM:(g + 1) * HEAD_DIM] = (
                o[g * BLOCK:(g + 1) * BLOCK].astype(o_ref.dtype))


def _attn_b(sink, q, k, v, kc, vc, tq):
    b, _, l, _ = q.shape
    c_len = kc.shape[2]
    sub = tq // BLOCK
    nb = l // BLOCK

    def main(bi, kv, i):
        return (bi, kv, i, 0)

    def prev(bi, kv, i):
        return (bi, kv, jnp.maximum(i * sub - 1, 0), 0)

    def nxt(bi, kv, i):
        return (bi, kv, jnp.minimum((i + 1) * sub, nb - 1), 0)

    def ctx_map(bi, kv, i):
        return (bi, kv, 0, 0)

    edge = lambda fn: pl.BlockSpec((1, 1, BLOCK, HEAD_DIM), fn)
    return pl.pallas_call(
        functools.partial(_attn_b_kernel, seq_len=l),
        grid=(b, B_KV_HEADS, l // tq),
        in_specs=[
            pl.BlockSpec(memory_space=pltpu.SMEM),
            pl.BlockSpec((1, GROUP, tq, HEAD_DIM), main),
            edge(prev), pl.BlockSpec((1, 1, tq, HEAD_DIM), main), edge(nxt),
            edge(prev), pl.BlockSpec((1, 1, tq, HEAD_DIM), main), edge(nxt),
            pl.BlockSpec((1, 1, c_len, HEAD_DIM), ctx_map),
            pl.BlockSpec((1, 1, c_len, HEAD_DIM), ctx_map),
        ],
        out_specs=pl.BlockSpec((1, tq, GROUP * HEAD_DIM), lambda bi, kv, i: (bi, i, kv)),
        out_shape=jax.ShapeDtypeStruct((b, l, B_HEADS * HEAD_DIM), BF16),
        compiler_params=_cparams(("parallel", "parallel", "arbitrary")),
        name="attn_window",
    )(sink, q, k, k, k, v, v, v, kc, vc)


def _residual_and_next(y, x, mod_ref, ng_ref):
    x1 = x + mod_ref[0, 2:3, :] * _rms(y, ng_ref[1:2, :])
    h2 = _rms(x1, ng_ref[2:3, :]) * (1.0 + mod_ref[0, 4:5, :]) + mod_ref[0, 3:4, :]
    return x1, h2


def _post_attn_kernel(oa_ref, ob_ref, w_ref, x_ref, mod_ref, ng_ref, x1_ref, h2_ref):
    na = oa_ref.shape[2]
    rows = oa_ref.shape[1] // POST_SUB
    ys = []
    for r in range(POST_SUB):
        sl = slice(r * rows, (r + 1) * rows)
        ys.append(jnp.dot(oa_ref[0, sl, :], w_ref[0:na, :], preferred_element_type=F32)
                  + jnp.dot(ob_ref[0, sl, :], w_ref[na:, :], preferred_element_type=F32))
    for r, y in enumerate(ys):
        sl = slice(r * rows, (r + 1) * rows)
        x1, h2 = _residual_and_next(y, x_ref[0, sl, :], mod_ref, ng_ref)
        x1_ref[0, sl, :] = x1
        h2_ref[0, sl, :] = h2.astype(BF16)


def _post_attn(oa, ob, w_out, x, mod6, ng, tm):
    b, l, d = x.shape
    na, nb_ = oa.shape[2], ob.shape[2]
    return pl.pallas_call(
        _post_attn_kernel,
        grid=(b, l // tm),
        in_specs=[
            pl.BlockSpec((1, tm, na), lambda bi, i: (bi, i, 0)),
            pl.BlockSpec((1, tm, nb_), lambda bi, i: (bi, i, 0)),
            pl.BlockSpec((na + nb_, d), lambda bi, i: (0, 0)),
            pl.BlockSpec((1, tm, d), lambda bi, i: (bi, i, 0)),
            pl.BlockSpec((1, N_MOD, d), lambda bi, i: (bi, 0, 0)),
            pl.BlockSpec((4, d), lambda bi, i: (0, 0)),
        ],
        out_specs=[
            pl.BlockSpec((1, tm, d), lambda bi, i: (bi, i, 0)),
            pl.BlockSpec((1, tm, d), lambda bi, i: (bi, i, 0)),
        ],
        out_shape=[jax.ShapeDtypeStruct((b, l, d), F32), jax.ShapeDtypeStruct((b, l, d), BF16)],
        compiler_params=_cparams(("parallel", "parallel")),
        name="attn_out_proj",
    )(oa, ob, w_out, x, mod6, ng)


def _ffn_kernel(*refs, fused_input, emit_next):
    refs = list(refs)
    if fused_input:
        xp_ref, xm_ref, xn_ref, rp_ref, rm_ref, rn_ref = refs[:6]
        refs = refs[6:]
    else:
        hp_ref, hm_ref, hn_ref, xm_ref = refs[:4]
        refs = refs[4:]
    mod_ref, ng_ref, wu_ref, cw_ref, cb_ref, wd_ref = refs[:6]
    refs = refs[6:]
    if emit_next:
        modn_ref, ngn_ref = refs[:2]
        refs = refs[2:]
    o_ref = refs[0]
    refs = refs[1:]
    if emit_next:
        hnext_ref = refs[0]
        refs = refs[1:]
    hcat_ref, ug0_ref, uv0_ref, ug1_ref, uv1_ref, act_ref = refs

    i = pl.program_id(1)
    last = pl.num_programs(1) - 1
    tm = xm_ref.shape[1]
    f = wd_ref.shape[0]
    fc = ug0_ref.shape[1]
    n_chunks = f // fc
    u_bufs = ((ug0_ref, uv0_ref), (ug1_ref, uv1_ref))

    def pre_norm(x):
        return (_rms(x, ng_ref[2:3, :]) * (1.0 + mod_ref[0, 4:5, :]) + mod_ref[0, 3:4, :]).astype(BF16)

    if fused_input:
        x_in = xm_ref[0] + rm_ref[0].astype(F32)
        hp = pre_norm(xp_ref[0] + rp_ref[0].astype(F32))
        hm = pre_norm(x_in)
        hn = pre_norm(xn_ref[0] + rn_ref[0].astype(F32))
    else:
        x_in = xm_ref[0]
        hp, hm, hn = hp_ref[0], hm_ref[0], hn_ref[0]
    hcat_ref[0:HALO, :] = jnp.where(i > 0, hp, jnp.zeros_like(hp))
    hcat_ref[HALO:HALO + tm, :] = hm
    hcat_ref[HALO + tm:, :] = jnp.where(i < last, hn, jnp.zeros_like(hn))

    def cols(c, half):
        return pl.ds(pl.multiple_of(half * f + c * fc, fc), fc)

    def conv(u_ref, c, half):
        cw = cw_ref[:, cols(c, half)]
        u = u_ref[...]
        rows = u.shape[0]
        prev = pltpu.roll(u, 1, 0)[HALO:HALO + tm]
        nxt = pltpu.roll(u, rows - 1, 0)[HALO:HALO + tm]
        return (prev * cw[0:1, :] + u[HALO:HALO + tm] * cw[1:2, :] + nxt * cw[2:3, :]
                + cb_ref[:, cols(c, half)])

    def up(c, bufs):
        hcat = hcat_ref[...]
        for half in range(2):
            bufs[half][...] = jnp.dot(hcat, wu_ref[:, cols(c, half)], preferred_element_type=F32)

    def mid(c, bufs):
        gate = conv(bufs[0], c, 0)
        val = conv(bufs[1], c, 1)
        act_ref[:, cols(c, 0)] = (gate / (1.0 + jnp.exp(-gate)) * val).astype(BF16)

    def stage(c, parity, do_up):
        if do_up:
            up(c + 1, u_bufs[1 - parity])
        mid(c, u_bufs[parity])

    up(0, u_bufs[0])
    unroll = FFN_UNROLL
    n_groups = (n_chunks - 1) // unroll

    def group(g, carry):
        for u in range(unroll):
            stage(g * unroll + u, u % 2, True)
        return carry

    lax.fori_loop(0, n_groups, group, 0)
    for c in range(n_groups * unroll, n_chunks):
        stage(c, c % 2, c + 1 < n_chunks)
    y = jnp.dot(act_ref[...], wd_ref[...], preferred_element_type=F32)
    out = x_in + mod_ref[0, 5:6, :] * _rms(y, ng_ref[3:4, :])
    o_ref[0] = out
    if emit_next:
        hnext_ref[0] = (_rms(out, ngn_ref[0:1, :]) * (1.0 + modn_ref[0, 1:2, :]) + modn_ref[0, 0:1, :]).astype(BF16)


def _ffn(x, mod6, ng, w_up, conv_w, conv_b, w_down, tm, fc, *, h2=None, branch=None, next_mod=None, next_ng=None):
    b, l, d = x.shape
    f = w_down.shape[0]
    assert f % fc == 0 and (h2 is None) != (branch is None)
    per = tm // HALO
    n_halo = l // HALO
    resident = lambda shape: pl.BlockSpec(shape, lambda bi, i: (0,) * len(shape),
                                          pipeline_mode=pl.Buffered(1))
    prev_halo = pl.BlockSpec((1, HALO, d), lambda bi, i: (bi, jnp.maximum(i * per - 1, 0), 0))
    next_halo = pl.BlockSpec((1, HALO, d), lambda bi, i: (bi, jnp.minimum((i + 1) * per, n_halo - 1), 0))
    tile = pl.BlockSpec((1, tm, d), lambda bi, i: (bi, i, 0))
    mod_spec = pl.BlockSpec((1, N_MOD, d), lambda bi, i: (bi, 0, 0))
    ng_spec = pl.BlockSpec((4, d), lambda bi, i: (0, 0))
    if branch is None:
        operands, in_specs = [h2, h2, h2, x], [prev_halo, tile, next_halo, tile]
    else:
        operands, in_specs = [x, x, x, branch, branch, branch], [prev_halo, tile, next_halo] * 2
    operands += [mod6, ng, w_up.astype(BF16), conv_w, conv_b.reshape(1, 2 * f), w_down.astype(BF16)]
    in_specs += [mod_spec, ng_spec, resident((d, 2 * f)), resident((3, 2 * f)), resident((1, 2 * f)), resident((f, d))]
    out_specs, out_shape = [tile], [jax.ShapeDtypeStruct((b, l, d), F32)]
    if next_mod is not None:
        operands += [next_mod, next_ng]
        in_specs += [mod_spec, ng_spec]
        out_specs.append(tile)
        out_shape.append(jax.ShapeDtypeStruct((b, l, d), BF16))
    outs = pl.pallas_call(
        functools.partial(_ffn_kernel, fused_input=branch is not None, emit_next=next_mod is not None),
        grid=(b, l // tm),
        in_specs=in_specs,
        out_specs=out_specs,
        out_shape=out_shape,
        scratch_shapes=[
            pltpu.VMEM((tm + 2 * HALO, d), BF16),
            pltpu.VMEM((tm + 2 * HALO, fc), F32),
            pltpu.VMEM((tm + 2 * HALO, fc), F32),
            pltpu.VMEM((tm + 2 * HALO, fc), F32),
            pltpu.VMEM((tm + 2 * HALO, fc), F32),
            pltpu.VMEM((tm, f), BF16),
        ],
        compiler_params=_cparams(("parallel", "arbitrary")),
        name="conv_ffn",
    )(*operands)
    return outs if next_mod is not None else outs[0]


def _dft_tables(l, d):
    n1, n2, cg = DFT_N1, l // DFT_N1, d // FNET_GROUPS
    a = np.arange(cg)
    ang_c = 2.0 * np.pi * np.outer(a, a) / cg
    csc = np.concatenate([np.cos(ang_c), np.sin(ang_c)], axis=1) / math.sqrt(cg)
    a1 = np.arange(n1)
    ang1 = 2.0 * np.pi * np.outer(a1, a1) / n1
    c1, s1 = np.cos(ang1), np.sin(ang1)
    m1 = np.block([[c1, -s1], [s1, c1]])
    a2 = np.arange(n2)
    ang2 = 2.0 * np.pi * np.outer(a2, a2) / n2
    c2 = np.cos(ang2) / math.sqrt(l)
    s2 = np.sin(ang2) / math.sqrt(l)
    ang_t = 2.0 * np.pi * np.outer(a1, a2) / l
    f32 = lambda t: jnp.asarray(t, F32)
    return (f32(csc).astype(BF16), f32(m1).astype(BF16), f32(c2).astype(BF16), f32(s2).astype(BF16),
            f32(np.cos(ang_t)), f32(np.sin(ang_t)))


def _fourier1_kernel(h_ref, csc_ref, m1_ref, twc_ref, tws_ref, zr_ref, zi_ref, *, d):
    cg = csc_ref.shape[0]
    n1 = h_ref.shape[1]
    t2s = h_ref.shape[2] // d
    csc = csc_ref[...]
    m1 = m1_ref[...]
    stacked = []
    for t in range(t2s):
        h = h_ref[0, :, t * d:(t + 1) * d]
        ab = [jnp.dot(h[:, g * cg:(g + 1) * cg], csc, preferred_element_type=F32)
              for g in range(d // cg)]
        a = jnp.concatenate([t_[:, :cg] for t_ in ab], axis=1)
        bm = jnp.concatenate([t_[:, cg:] for t_ in ab], axis=1)
        stacked.append(jnp.concatenate([a, bm], axis=0).astype(BF16))
    ys = [jnp.dot(m1, ab2, preferred_element_type=F32) for ab2 in stacked]
    for t, y in enumerate(ys):
        yr = y[:n1]
        yin = y[n1:]
        cc = jnp.tile(twc_ref[:, t * HEAD_DIM:(t + 1) * HEAD_DIM], (1, d // HEAD_DIM))
        ss = jnp.tile(tws_ref[:, t * HEAD_DIM:(t + 1) * HEAD_DIM], (1, d // HEAD_DIM))
        zr_ref[0, :, t * d:(t + 1) * d] = (yr * cc - yin * ss).astype(BF16)
        zi_ref[0, :, t * d:(t + 1) * d] = (-(yin * cc) - yr * ss).astype(BF16)


def _fourier1(h, csc, m1, twc, tws, t2s):
    b, l, d = h.shape
    n1, n2 = DFT_N1, l // DFT_N1
    hv = h.reshape(b, n1, n2 * d)
    lanes = HEAD_DIM
    twc_x = jnp.repeat(twc, lanes, axis=1)
    tws_x = jnp.repeat(tws, lanes, axis=1)
    cg = csc.shape[0]
    return pl.pallas_call(
        functools.partial(_fourier1_kernel, d=d),
        grid=(b, n2 // t2s),
        in_specs=[
            pl.BlockSpec((1, n1, t2s * d), lambda bi, j: (bi, 0, j)),
            pl.BlockSpec((cg, 2 * cg), lambda bi, j: (0, 0)),
            pl.BlockSpec((2 * n1, 2 * n1), lambda bi, j: (0, 0)),
            pl.BlockSpec((n1, t2s * lanes), lambda bi, j: (0, j)),
            pl.BlockSpec((n1, t2s * lanes), lambda bi, j: (0, j)),
        ],
        out_specs=[
            pl.BlockSpec((1, n1, t2s * d), lambda bi, j: (bi, 0, j)),
            pl.BlockSpec((1, n1, t2s * d), lambda bi, j: (bi, 0, j)),
        ],
        out_shape=[jax.ShapeDtypeStruct((b, n1, n2 * d), BF16)] * 2,
        compiler_params=_cparams(("parallel", "parallel")),
        name="fnet_stage1",
    )(hv, csc, m1, twc_x, tws_x)


def _fourier2_kernel(zr_ref, zi_ref, c2_ref, s2_ref, w_ref, mod_ref, ng_ref, r_ref):
    k1s = zr_ref.shape[1]
    n2 = zr_ref.shape[2]
    d = zr_ref.shape[3]
    c2 = c2_ref[...]
    s2 = s2_ref[...]
    fs = []
    for k in range(k1s):
        f = (jnp.dot(c2, zr_ref[0, k], preferred_element_type=F32)
             + jnp.dot(s2, zi_ref[0, k], preferred_element_type=F32))
        fs.append(f.astype(BF16))
    per = k1s // F2_GROUPS
    ys = [jnp.dot(jnp.concatenate(fs[g * per:(g + 1) * per], axis=0), w_ref[...], preferred_element_type=F32)
          for g in range(F2_GROUPS)]
    for k in range(k1s):
        y = ys[k // per][(k % per) * n2:(k % per + 1) * n2]
        r_ref[0, :, k * d:(k + 1) * d] = (mod_ref[0, 2:3, :] * _rms(y, ng_ref[1:2, :])).astype(BF16)


def _fourier2(zr, zi, c2, s2, w_f, mod6, ng, k1s):
    b, n1, nd = zr.shape
    n2 = c2.shape[0]
    d = nd // n2
    zr4 = zr.reshape(b, n1, n2, d)
    zi4 = zi.reshape(b, n1, n2, d)
    r = pl.pallas_call(
        _fourier2_kernel,
        grid=(b, n1 // k1s),
        in_specs=[
            pl.BlockSpec((1, k1s, n2, d), lambda bi, j: (bi, j, 0, 0)),
            pl.BlockSpec((1, k1s, n2, d), lambda bi, j: (bi, j, 0, 0)),
            pl.BlockSpec((n2, n2), lambda bi, j: (0, 0)),
            pl.BlockSpec((n2, n2), lambda bi, j: (0, 0)),
            pl.BlockSpec((d, d), lambda bi, j: (0, 0)),
            pl.BlockSpec((1, N_MOD, d), lambda bi, j: (bi, 0, 0)),
            pl.BlockSpec((4, d), lambda bi, j: (0, 0)),
        ],
        out_specs=pl.BlockSpec((1, n2, k1s * d), lambda bi, j: (bi, 0, j)),
        out_shape=jax.ShapeDtypeStruct((b, n2, n1 * d), BF16),
        compiler_params=_cparams(("parallel", "parallel")),
        name="fnet_stage2",
    )(zr4, zi4, c2, s2, w_f, mod6, ng)
    return r.reshape(b, n2 * n1, d)


def _rope_tables(l):
    n_freq = HEAD_DIM // 4
    inv = ROPE_THETA ** (-jnp.arange(n_freq, dtype=F32) / n_freq)
    ang_row = jnp.arange(l // GRID_W).astype(F32)[:, None] * inv
    ang_col = jnp.arange(GRID_W).astype(F32)[:, None] * inv
    zr = jnp.zeros_like(ang_row)
    zc = jnp.zeros_like(ang_col)
    cos_r = jnp.concatenate([jnp.cos(ang_row), jnp.cos(ang_row), zr, zr], axis=1)
    sin_r = jnp.concatenate([-jnp.sin(ang_row), jnp.sin(ang_row), zr, zr], axis=1)
    cos_c = jnp.concatenate([zc, zc, jnp.cos(ang_col), jnp.cos(ang_col)], axis=1)
    sin_c = jnp.concatenate([zc, zc, -jnp.sin(ang_col), jnp.sin(ang_col)], axis=1)
    return cos_r, sin_r, cos_c, sin_c


def kernel(x, c, ctx, c_ctx, mod_w, mod_b, norm_g, attn_w_in, attn_w_out, q_norm_g, k_norm_g, sink,
           fourier_w_out, ffn_w_up, ffn_conv_w, ffn_conv_b, ffn_w_down):
    b, l, d = x.shape
    c_len = ctx.shape[1]
    assert mod_w.shape[0] == 2 and b + 1 <= COND_ROWS
    assert l % max(QKV_TM, WIN_TQ, FFN_TM, POST_TM, DFT_N1 * F1_T2) == 0

    cond = jnp.zeros((COND_ROWS, d), F32).at[:b].set(c).at[b].set(c_ctx)
    mods = _mods(cond, mod_w, mod_b)
    mod_l0 = mods[0].reshape(COND_ROWS, N_MOD, d)
    mod_l1 = mods[1].reshape(COND_ROWS, N_MOD, d)

    rope = _rope_tables(l)
    w_in = attn_w_in[0].astype(BF16)
    qg = q_norm_g[0].reshape(1, HEAD_DIM)
    kg = k_norm_g[0].reshape(1, HEAD_DIM)
    qa, ka, vat, qb, kb, vb = _qkv(x, mod_l0, lambda bi: bi, norm_g[0], w_in, qg, kg, rope, QKV_TM)
    zeros_r = jnp.zeros((8, HEAD_DIM), F32)
    zeros_c = jnp.zeros((GRID_W, HEAD_DIM), F32)
    no_rope = (jnp.ones_like(zeros_r), zeros_r, zeros_c, zeros_c)
    _, kac, vact, _, kbc, vbc = _qkv(ctx, mod_l0, lambda bi: b, norm_g[0], w_in, qg, kg, no_rope, c_len)
    oa = _attn_a(qa, kac, vact, ka, vat, ATT_TQ, ATT_TK)
    ob = _attn_b(sink[0], qb, kb, vb, kbc, vbc, WIN_TQ)
    x1, h2 = _post_attn(oa, ob, attn_w_out[0].astype(BF16), x, mod_l0, norm_g[0], POST_TM)
    x2, h_next = _ffn(x1, mod_l0, norm_g[0], ffn_w_up[0], ffn_conv_w[0], ffn_conv_b[0], ffn_w_down[0],
                      FFN_TM, FFN_FC, h2=h2, next_mod=mod_l1, next_ng=norm_g[1])

    csc, m1, c2, s2, twc, tws = _dft_tables(l, d)
    zr, zi = _fourier1(h_next, csc, m1, twc, tws, F1_T2)
    branch = _fourier2(zr, zi, c2, s2, fourier_w_out[0].astype(BF16), mod_l1, norm_g[1], F2_K1)
    return _ffn(x2, mod_l1, norm_g[1], ffn_w_up[1], ffn_conv_w[1], ffn_conv_b[1], ffn_w_down[1],
                FFN_TM, FFN_FC, branch=branch)
```

```python
import functools
import math

import numpy as np
import jax
import jax.numpy as jnp
from jax import lax
from jax.experimental import pallas as pl
from jax.experimental.pallas import tpu as pltpu

F32 = jnp.float32
BF16 = jnp.bfloat16

HEAD_DIM = 128
GRID_W = 64
A_HEADS = 4
A_KV_HEADS = 2
B_HEADS = 4
B_KV_HEADS = 2
GROUP = A_HEADS // A_KV_HEADS
BLOCK = 128
WINDOW = 128
ROPE_THETA = 10000.0
FNET_GROUPS = 4
N_MOD = 6
EPS = 1e-6
LOG2E = math.log2(math.e)
Q_SCALE = HEAD_DIM ** -0.5 * LOG2E
NEG_BIG = -1e30

DFT_N1 = 128
COND_ROWS = 8
VMEM_LIMIT = 56 * 1024 * 1024

QKV_TM = 512
QKV_SUB = 2
ATT_TQ = 2048
ATT_TK = 512
ATT_UNROLL = 2
V_ROWS = HEAD_DIM + 16
WIN_TQ = 1024
POST_TM = 2048
POST_SUB = 16
FFN_TM = 1024
FFN_FC = 256
FFN_UNROLL = 10
HALO = 16
F1_T2 = 16
F2_K1 = 16
F2_GROUPS = 8


def _cparams(sem):
    return pltpu.CompilerParams(dimension_semantics=sem, vmem_limit_bytes=VMEM_LIMIT)


def _rms(x, g):
    return x * lax.rsqrt(jnp.mean(x * x, axis=-1, keepdims=True) + EPS) * g


def _mod_kernel(cond_ref, w_ref, b_ref, o_ref):
    c = cond_ref[...]
    s = c / (1.0 + jnp.exp(-c))
    o_ref[0] = jnp.dot(s, w_ref[0], preferred_element_type=F32) + b_ref[0]


def _mods(cond, mod_w, mod_b):
    depth, d, n = mod_w.shape
    tn = n // 4
    return pl.pallas_call(
        _mod_kernel,
        grid=(depth, n // tn),
        in_specs=[
            pl.BlockSpec((COND_ROWS, d), lambda l, j: (0, 0)),
            pl.BlockSpec((1, d, tn), lambda l, j: (l, 0, j)),
            pl.BlockSpec((1, 1, tn), lambda l, j: (l, 0, j)),
        ],
        out_specs=pl.BlockSpec((1, COND_ROWS, tn), lambda l, j: (l, 0, j)),
        out_shape=jax.ShapeDtypeStruct((depth, COND_ROWS, n), F32),
        compiler_params=_cparams(("arbitrary", "arbitrary")),
        name="adaln_mods",
    )(cond, mod_w, mod_b.reshape(depth, 1, n))


def _qkv_kernel(x_ref, mod_ref, ng_ref, w_ref, qg_ref, kg_ref, cr_ref, sr_ref, cc_ref, sc_ref,
                qa_ref, ka_ref, vat_ref, qb_ref, kb_ref, vb_ref):
    tm = x_ref.shape[1]
    rows = tm // QKV_SUB
    assert rows % GRID_W == 0
    projs = []
    for r in range(QKV_SUB):
        x = x_ref[0, r * rows:(r + 1) * rows, :]
        h = _rms(x, ng_ref[0:1, :]) * (1.0 + mod_ref[0, 1:2, :]) + mod_ref[0, 0:1, :]
        projs.append(jnp.dot(h.astype(BF16), w_ref[...], preferred_element_type=F32))

    lane = lax.broadcasted_iota(jnp.int32, (rows, HEAD_DIM), 1)
    low_half = (lane % (HEAD_DIM // 2)) < (HEAD_DIM // 4)
    qg = qg_ref[...]
    kg = kg_ref[...]
    col_cos = cc_ref[...]
    col_sin = sc_ref[...]

    for r, proj in enumerate(projs):
        sl = slice(r * rows, (r + 1) * rows)
        g0 = r * (rows // GRID_W)

        def table(row_ref, col_part):
            return jnp.concatenate([jnp.broadcast_to(row_ref[g0 + i:g0 + i + 1, :], (GRID_W, HEAD_DIM)) + col_part
                                    for i in range(rows // GRID_W)], axis=0)

        cos = table(cr_ref, col_cos)
        sin = table(sr_ref, col_sin)

        def rope(t):
            partner = jnp.where(low_half,
                                pltpu.roll(t, HEAD_DIM - HEAD_DIM // 4, 1),
                                pltpu.roll(t, HEAD_DIM // 4, 1))
            return t * cos + partner * sin

        def head(j):
            return proj[:, j * HEAD_DIM:(j + 1) * HEAD_DIM]

        col = 0
        for hh in range(A_HEADS):
            qa_ref[0, hh, :, sl] = (rope(_rms(head(col + hh), qg)) * Q_SCALE).T.astype(BF16)
        col += A_HEADS
        for hh in range(B_HEADS):
            qb_ref[0, hh, sl, :] = (rope(head(col + hh)) * Q_SCALE).astype(BF16)
        col += B_HEADS
        for hh in range(A_KV_HEADS):
            ka_ref[0, hh, sl, :] = rope(_rms(head(col + hh), kg)).astype(BF16)
        col += A_KV_HEADS
        for hh in range(A_KV_HEADS):
            vat_ref[0, hh, 0, 0:HEAD_DIM, sl] = head(col + hh).T.astype(BF16)
            vat_ref[0, hh, 0, HEAD_DIM:, sl] = jnp.ones((V_ROWS - HEAD_DIM, rows), BF16)
        col += A_KV_HEADS
        for hh in range(B_KV_HEADS):
            kb_ref[0, hh, sl, :] = rope(head(col + hh)).astype(BF16)
        col += B_KV_HEADS
        for hh in range(B_KV_HEADS):
            vb_ref[0, hh, sl, :] = head(col + hh).astype(BF16)


def _qkv(x, mod6, mod_row, ng, w_in, qg, kg, rope, tm):
    b, l, d = x.shape
    n = w_in.shape[1]
    nt = l // tm
    assert tm % (GRID_W * QKV_SUB) == 0
    rt = max(tm // GRID_W, 8)
    return pl.pallas_call(
        _qkv_kernel,
        grid=(b, nt),
        in_specs=[
            pl.BlockSpec((1, tm, d), lambda bi, i: (bi, i, 0)),
            pl.BlockSpec((1, N_MOD, d), lambda bi, i: (mod_row(bi), 0, 0)),
            pl.BlockSpec((4, d), lambda bi, i: (0, 0)),
            pl.BlockSpec((d, n), lambda bi, i: (0, 0)),
            pl.BlockSpec((1, HEAD_DIM), lambda bi, i: (0, 0)),
            pl.BlockSpec((1, HEAD_DIM), lambda bi, i: (0, 0)),
            pl.BlockSpec((rt, HEAD_DIM), lambda bi, i: (i, 0)),
            pl.BlockSpec((rt, HEAD_DIM), lambda bi, i: (i, 0)),
            pl.BlockSpec((GRID_W, HEAD_DIM), lambda bi, i: (0, 0)),
            pl.BlockSpec((GRID_W, HEAD_DIM), lambda bi, i: (0, 0)),
        ],
        out_specs=[
            pl.BlockSpec((1, A_HEADS, HEAD_DIM, tm), lambda bi, i: (bi, 0, 0, i)),
            pl.BlockSpec((1, A_KV_HEADS, tm, HEAD_DIM), lambda bi, i: (bi, 0, i, 0)),
            pl.BlockSpec((1, A_KV_HEADS, 1, V_ROWS, tm), lambda bi, i: (bi, 0, i, 0, 0)),
            pl.BlockSpec((1, B_HEADS, tm, HEAD_DIM), lambda bi, i: (bi, 0, i, 0)),
            pl.BlockSpec((1, B_KV_HEADS, tm, HEAD_DIM), lambda bi, i: (bi, 0, i, 0)),
            pl.BlockSpec((1, B_KV_HEADS, tm, HEAD_DIM), lambda bi, i: (bi, 0, i, 0)),
        ],
        out_shape=[
            jax.ShapeDtypeStruct((b, A_HEADS, HEAD_DIM, l), BF16),
            jax.ShapeDtypeStruct((b, A_KV_HEADS, l, HEAD_DIM), BF16),
            jax.ShapeDtypeStruct((b, A_KV_HEADS, nt, V_ROWS, tm), BF16),
            jax.ShapeDtypeStruct((b, B_HEADS, l, HEAD_DIM), BF16),
            jax.ShapeDtypeStruct((b, B_KV_HEADS, l, HEAD_DIM), BF16),
            jax.ShapeDtypeStruct((b, B_KV_HEADS, l, HEAD_DIM), BF16),
        ],
        compiler_params=_cparams(("parallel", "parallel")),
        name="qkv_proj",
    )(x, mod6, ng, w_in, qg, kg, *rope)


def _attn_a_kernel(q_ref, kc_ref, vct_ref, k_ref, vt_ref, o_ref,
                   sc_ref, pc_ref, s0_ref, s1_ref, p0_ref, p1_ref, acc_ref):
    tq = q_ref.shape[3]
    ts = vt_ref.shape[4]
    tk = s0_ref.shape[0]
    sub = tk // ts
    n_chunks = vt_ref.shape[2] // sub
    assert n_chunks >= 2 and ATT_UNROLL % 2 == 0
    qt = jnp.concatenate([q_ref[0, g] for g in range(GROUP)], axis=1)

    def k_chunk(j):
        return k_ref[0, 0, pl.ds(pl.multiple_of(j * tk, tk), tk), :]

    def scores(kblk, s_ref):
        s = jnp.dot(kblk, qt, preferred_element_type=F32)
        s_ref[...] = s
        return jnp.max(s, axis=0, keepdims=True)

    def softmax(s_ref, p_ref, mc, m):
        m_new = jnp.maximum(m, mc)
        p_ref[...] = jnp.exp2(s_ref[...] - m_new).astype(BF16)
        return m_new, jnp.exp2(m - m_new)

    def accumulate(vt_blocks, p_ref, alpha):
        pv = None
        for u, vtblk in enumerate(vt_blocks):
            part = jnp.dot(vtblk, p_ref[u * vtblk.shape[1]:(u + 1) * vtblk.shape[1], :],
                           preferred_element_type=F32)
            pv = part if pv is None else pv + part
        acc_ref[...] = alpha * acc_ref[...] + pv

    def v_chunk(j):
        return [vt_ref[0, 0, j * sub + u] for u in range(sub)]

    acc_ref[...] = jnp.zeros(acc_ref.shape, F32)
    m = jnp.full((1, GROUP * tq), NEG_BIG, F32)
    mc_ctx = scores(kc_ref[0, 0], sc_ref)
    mc0 = scores(k_chunk(0), s0_ref)
    m, alpha_ctx = softmax(sc_ref, pc_ref, mc_ctx, m)
    mc1 = scores(k_chunk(1), s1_ref)
    m, alpha0 = softmax(s0_ref, p0_ref, mc0, m)
    accumulate([vct_ref[0, 0, 0]], pc_ref, alpha_ctx)

    s_bufs = (s0_ref, s1_ref)
    p_bufs = (p0_ref, p1_ref)

    def stage(c, parity, carry, has_next):
        m, alpha_prev, mc = carry
        mc_next = scores(k_chunk(c + 1), s_bufs[1 - parity]) if has_next else None
        m, alpha = softmax(s_bufs[parity], p_bufs[parity], mc, m)
        accumulate(v_chunk(c - 1), p_bufs[1 - parity], alpha_prev)
        return m, alpha, mc_next

    unroll = ATT_UNROLL
    n_groups = (n_chunks - 2) // unroll

    def group(g, carry):
        for u in range(unroll):
            carry = stage(1 + g * unroll + u, (1 + u) % 2, carry, True)
        return carry

    carry = lax.fori_loop(0, n_groups, group, (m, alpha0, mc1))
    for c in range(1 + n_groups * unroll, n_chunks):
        carry = stage(c, c % 2, carry, c + 1 < n_chunks)
    accumulate(v_chunk(n_chunks - 1), p_bufs[(n_chunks - 1) % 2], carry[1])

    acc = acc_ref[...]
    o = (acc[:HEAD_DIM] / acc[HEAD_DIM:HEAD_DIM + 1]).T
    for g in range(GROUP):
        o_ref[0, :, g * HEAD_DIM:(g + 1) * HEAD_DIM] = o[g * tq:(g + 1) * tq].astype(o_ref.dtype)


def _attn_a(q, kc, vct, k, vt, tq, tk):
    b, _, _, l = q.shape
    c_len = kc.shape[2]
    n_stored, ts = vt.shape[2], vt.shape[4]
    assert tk % ts == 0 and l % tk == 0
    return pl.pallas_call(
        _attn_a_kernel,
        grid=(b, A_KV_HEADS, l // tq),
        in_specs=[
            pl.BlockSpec((1, GROUP, HEAD_DIM, tq), lambda bi, kv, i: (bi, kv, 0, i)),
            pl.BlockSpec((1, 1, c_len, HEAD_DIM), lambda bi, kv, i: (bi, kv, 0, 0)),
            pl.BlockSpec((1, 1, 1, V_ROWS, c_len), lambda bi, kv, i: (bi, kv, 0, 0, 0)),
            pl.BlockSpec((1, 1, l, HEAD_DIM), lambda bi, kv, i: (bi, kv, 0, 0)),
            pl.BlockSpec((1, 1, n_stored, V_ROWS, ts), lambda bi, kv, i: (bi, kv, 0, 0, 0)),
        ],
        out_specs=pl.BlockSpec((1, tq, GROUP * HEAD_DIM), lambda bi, kv, i: (bi, i, kv)),
        out_shape=jax.ShapeDtypeStruct((b, l, A_HEADS * HEAD_DIM), BF16),
        scratch_shapes=[
            pltpu.VMEM((c_len, GROUP * tq), F32),
            pltpu.VMEM((c_len, GROUP * tq), BF16),
            pltpu.VMEM((tk, GROUP * tq), F32),
            pltpu.VMEM((tk, GROUP * tq), F32),
            pltpu.VMEM((tk, GROUP * tq), BF16),
            pltpu.VMEM((tk, GROUP * tq), BF16),
            pltpu.VMEM((V_ROWS, GROUP * tq), F32),
        ],
        compiler_params=_cparams(("parallel", "parallel", "arbitrary")),
        name="attn_global",
    )(q, kc, vct, k, vt)


def _attn_b_kernel(sink_ref, q_ref, kp_ref, km_ref, kn_ref, vp_ref, vm_ref, vn_ref,
                   kc_ref, vc_ref, o_ref, *, seq_len):
    kv = pl.program_id(1)
    i = pl.program_id(2)
    tq = q_ref.shape[2]
    sub = tq // BLOCK
    band = 3 * BLOCK
    kcat = jnp.concatenate([kp_ref[0, 0], km_ref[0, 0], kn_ref[0, 0]], axis=0)
    vcat = jnp.concatenate([vp_ref[0, 0], vm_ref[0, 0], vn_ref[0, 0]], axis=0)
    kc = kc_ref[0, 0]
    vc = vc_ref[0, 0]
    rows = GROUP * BLOCK
    nt = (((1,), (1,)), ((), ()))

    r = lax.broadcasted_iota(jnp.int32, (rows, band), 0) % BLOCK
    c = lax.broadcasted_iota(jnp.int32, (rows, band), 1)
    in_window = jnp.abs(c - BLOCK - r) <= WINDOW
    head_row = lax.broadcasted_iota(jnp.int32, (rows, 1), 0) // BLOCK
    sink2 = jnp.zeros((rows, 1), F32)
    for g in range(GROUP):
        sink2 = jnp.where(head_row == g, sink_ref[kv * GROUP + g] * LOG2E, sink2)

    scores = []
    for j in range(sub):
        q2 = jnp.concatenate([q_ref[0, g, j * BLOCK:(j + 1) * BLOCK, :] for g in range(GROUP)], axis=0)
        s_loc = lax.dot_general(q2, kcat[j * BLOCK:j * BLOCK + band], nt, preferred_element_type=F32)
        s_ctx = lax.dot_general(q2, kc, nt, preferred_element_type=F32)
        scores.append((s_loc, s_ctx))
    probs = []
    for j, (s_loc, s_ctx) in enumerate(scores):
        kpos = (i * sub + (j - 1)) * BLOCK + c
        valid = in_window & (kpos >= 0) & (kpos < seq_len)
        s_loc = jnp.where(valid, s_loc, NEG_BIG)
        m = jnp.maximum(jnp.maximum(jnp.max(s_loc, axis=-1, keepdims=True),
                                    jnp.max(s_ctx, axis=-1, keepdims=True)), sink2)
        p_loc = jnp.exp2(s_loc - m)
        p_ctx = jnp.exp2(s_ctx - m)
        denom = (jnp.sum(p_loc, axis=-1, keepdims=True) + jnp.sum(p_ctx, axis=-1, keepdims=True)
                 + jnp.exp2(sink2 - m))
        probs.append((p_loc.astype(BF16), p_ctx.astype(BF16), denom))
    for j, (p_loc, p_ctx, denom) in enumerate(probs):
        o = (jnp.dot(p_ctx, vc, preferred_element_type=F32)
             + jnp.dot(p_loc, vcat[j * BLOCK:j * BLOCK + band], preferred_element_type=F32)) / denom
        for g in range(GROUP):
            o_ref[0, j * BLOCK:(j + 1) * BLOCK, g * HEAD_DIM:(g + 1) * HEAD_DIM] = (
                o[g * BLOCK:(g + 1) * BLOCK].astype(o_ref.dtype))


def _attn_b(sink, q, k, v, kc, vc, tq):
    b, _, l, _ = q.shape
    c_len = kc.shape[2]
    sub = tq // BLOCK
    nb = l // BLOCK

    def main(bi, kv, i):
        return (bi, kv, i, 0)

    def prev(bi, kv, i):
        return (bi, kv, jnp.maximum(i * sub - 1, 0), 0)

    def nxt(bi, kv, i):
        return (bi, kv, jnp.minimum((i + 1) * sub, nb - 1), 0)

    def ctx_map(bi, kv, i):
        return (bi, kv, 0, 0)

    edge = lambda fn: pl.BlockSpec((1, 1, BLOCK, HEAD_DIM), fn)
    return pl.pallas_call(
        functools.partial(_attn_b_kernel, seq_len=l),
        grid=(b, B_KV_HEADS, l // tq),
        in_specs=[
            pl.BlockSpec(memory_space=pltpu.SMEM),
            pl.BlockSpec((1, GROUP, tq, HEAD_DIM), main),
            edge(prev), pl.BlockSpec((1, 1, tq, HEAD_DIM), main), edge(nxt),
            edge(prev), pl.BlockSpec((1, 1, tq, HEAD_DIM), main), edge(nxt),
            pl.BlockSpec((1, 1, c_len, HEAD_DIM), ctx_map),
            pl.BlockSpec((1, 1, c_len, HEAD_DIM), ctx_map),
        ],
        out_specs=pl.BlockSpec((1, tq, GROUP * HEAD_DIM), lambda bi, kv, i: (bi, i, kv)),
        out_shape=jax.ShapeDtypeStruct((b, l, B_HEADS * HEAD_DIM), BF16),
        compiler_params=_cparams(("parallel", "parallel", "arbitrary")),
        name="attn_window",
    )(sink, q, k, k, k, v, v, v, kc, vc)


def _residual_and_next(y, x, mod_ref, ng_ref):
    x1 = x + mod_ref[0, 2:3, :] * _rms(y, ng_ref[1:2, :])
    h2 = _rms(x1, ng_ref[2:3, :]) * (1.0 + mod_ref[0, 4:5, :]) + mod_ref[0, 3:4, :]
    return x1, h2


def _post_attn_kernel(oa_ref, ob_ref, w_ref, x_ref, mod_ref, ng_ref, x1_ref, h2_ref):
    na = oa_ref.shape[2]
    rows = oa_ref.shape[1] // POST_SUB
    ys = []
    for r in range(POST_SUB):
        sl = slice(r * rows, (r + 1) * rows)
        ys.append(jnp.dot(oa_ref[0, sl, :], w_ref[0:na, :], preferred_element_type=F32)
                  + jnp.dot(ob_ref[0, sl, :], w_ref[na:, :], preferred_element_type=F32))
    for r, y in enumerate(ys):
        sl = slice(r * rows, (r + 1) * rows)
        x1, h2 = _residual_and_next(y, x_ref[0, sl, :], mod_ref, ng_ref)
        x1_ref[0, sl, :] = x1
        h2_ref[0, sl, :] = h2.astype(BF16)


def _post_attn(oa, ob, w_out, x, mod6, ng, tm):
    b, l, d = x.shape
    na, nb_ = oa.shape[2], ob.shape[2]
    return pl.pallas_call(
        _post_attn_kernel,
        grid=(b, l // tm),
        in_specs=[
            pl.BlockSpec((1, tm, na), lambda bi, i: (bi, i, 0)),
            pl.BlockSpec((1, tm, nb_), lambda bi, i: (bi, i, 0)),
            pl.BlockSpec((na + nb_, d), lambda bi, i: (0, 0)),
            pl.BlockSpec((1, tm, d), lambda bi, i: (bi, i, 0)),
            pl.BlockSpec((1, N_MOD, d), lambda bi, i: (bi, 0, 0)),
            pl.BlockSpec((4, d), lambda bi, i: (0, 0)),
        ],
        out_specs=[
            pl.BlockSpec((1, tm, d), lambda bi, i: (bi, i, 0)),
            pl.BlockSpec((1, tm, d), lambda bi, i: (bi, i, 0)),
        ],
        out_shape=[jax.ShapeDtypeStruct((b, l, d), F32), jax.ShapeDtypeStruct((b, l, d), BF16)],
        compiler_params=_cparams(("parallel", "parallel")),
        name="attn_out_proj",
    )(oa, ob, w_out, x, mod6, ng)


def _ffn_kernel(*refs, fused_input, emit_next):
    refs = list(refs)
    if fused_input:
        xp_ref, xm_ref, xn_ref, rp_ref, rm_ref, rn_ref = refs[:6]
        refs = refs[6:]
    else:
        hp_ref, hm_ref, hn_ref, xm_ref = refs[:4]
        refs = refs[4:]
    mod_ref, ng_ref, wu_ref, cw_ref, cb_ref, wd_ref = refs[:6]
    refs = refs[6:]
    if emit_next:
        modn_ref, ngn_ref = refs[:2]
        refs = refs[2:]
    o_ref = refs[0]
    refs = refs[1:]
    if emit_next:
        hnext_ref = refs[0]
        refs = refs[1:]
    hcat_ref, ug0_ref, uv0_ref, ug1_ref, uv1_ref, act_ref = refs

    i = pl.program_id(1)
    last = pl.num_programs(1) - 1
    tm = xm_ref.shape[1]
    f = wd_ref.shape[0]
    fc = ug0_ref.shape[1]
    n_chunks = f // fc
    u_bufs = ((ug0_ref, uv0_ref), (ug1_ref, uv1_ref))

    def pre_norm(x):
        return (_rms(x, ng_ref[2:3, :]) * (1.0 + mod_ref[0, 4:5, :]) + mod_ref[0, 3:4, :]).astype(BF16)

    if fused_input:
        x_in = xm_ref[0] + rm_ref[0].astype(F32)
        hp = pre_norm(xp_ref[0] + rp_ref[0].astype(F32))
        hm = pre_norm(x_in)
        hn = pre_norm(xn_ref[0] + rn_ref[0].astype(F32))
    else:
        x_in = xm_ref[0]
        hp, hm, hn = hp_ref[0], hm_ref[0], hn_ref[0]
    hcat_ref[0:HALO, :] = jnp.where(i > 0, hp, jnp.zeros_like(hp))
    hcat_ref[HALO:HALO + tm, :] = hm
    hcat_ref[HALO + tm:, :] = jnp.where(i < last, hn, jnp.zeros_like(hn))

    def cols(c, half):
        return pl.ds(pl.multiple_of(half * f + c * fc, fc), fc)

    def conv(u_ref, c, half):
        cw = cw_ref[:, cols(c, half)]
        u = u_ref[...]
        rows = u.shape[0]
        prev = pltpu.roll(u, 1, 0)[HALO:HALO + tm]
        nxt = pltpu.roll(u, rows - 1, 0)[HALO:HALO + tm]
        return (prev * cw[0:1, :] + u[HALO:HALO + tm] * cw[1:2, :] + nxt * cw[2:3, :]
                + cb_ref[:, cols(c, half)])

    def up(c, bufs):
        hcat = hcat_ref[...]
        for half in range(2):
            bufs[half][...] = jnp.dot(hcat, wu_ref[:, cols(c, half)], preferred_element_type=F32)

    def mid(c, bufs):
        gate = conv(bufs[0], c, 0)
        val = conv(bufs[1], c, 1)
        act_ref[:, cols(c, 0)] = (gate / (1.0 + jnp.exp(-gate)) * val).astype(BF16)

    def stage(c, parity, do_up):
        if do_up:
            up(c + 1, u_bufs[1 - parity])
        mid(c, u_bufs[parity])

    up(0, u_bufs[0])
    unroll = FFN_UNROLL
    n_groups = (n_chunks - 1) // unroll

    def group(g, carry):
        for u in range(unroll):
            stage(g * unroll + u, u % 2, True)
        return carry

    lax.fori_loop(0, n_groups, group, 0)
    for c in range(n_groups * unroll, n_chunks):
        stage(c, c % 2, c + 1 < n_chunks)
    y = jnp.dot(act_ref[...], wd_ref[...], preferred_element_type=F32)
    out = x_in + mod_ref[0, 5:6, :] * _rms(y, ng_ref[3:4, :])
    o_ref[0] = out
    if emit_next:
        hnext_ref[0] = (_rms(out, ngn_ref[0:1, :]) * (1.0 + modn_ref[0, 1:2, :]) + modn_ref[0, 0:1, :]).astype(BF16)


def _ffn(x, mod6, ng, w_up, conv_w, conv_b, w_down, tm, fc, *, h2=None, branch=None, next_mod=None, next_ng=None):
    b, l, d = x.shape
    f = w_down.shape[0]
    assert f % fc == 0 and (h2 is None) != (branch is None)
    per = tm // HALO
    n_halo = l // HALO
    resident = lambda shape: pl.BlockSpec(shape, lambda bi, i: (0,) * len(shape),
                                          pipeline_mode=pl.Buffered(1))
    prev_halo = pl.BlockSpec((1, HALO, d), lambda bi, i: (bi, jnp.maximum(i * per - 1, 0), 0))
    next_halo = pl.BlockSpec((1, HALO, d), lambda bi, i: (bi, jnp.minimum((i + 1) * per, n_halo - 1), 0))
    tile = pl.BlockSpec((1, tm, d), lambda bi, i: (bi, i, 0))
    mod_spec = pl.BlockSpec((1, N_MOD, d), lambda bi, i: (bi, 0, 0))
    ng_spec = pl.BlockSpec((4, d), lambda bi, i: (0, 0))
    if branch is None:
        operands, in_specs = [h2, h2, h2, x], [prev_halo, tile, next_halo, tile]
    else:
        operands, in_specs = [x, x, x, branch, branch, branch], [prev_halo, tile, next_halo] * 2
    operands += [mod6, ng, w_up.astype(BF16), conv_w, conv_b.reshape(1, 2 * f), w_down.astype(BF16)]
    in_specs += [mod_spec, ng_spec, resident((d, 2 * f)), resident((3, 2 * f)), resident((1, 2 * f)), resident((f, d))]
    out_specs, out_shape = [tile], [jax.ShapeDtypeStruct((b, l, d), F32)]
    if next_mod is not None:
        operands += [next_mod, next_ng]
        in_specs += [mod_spec, ng_spec]
        out_specs.append(tile)
        out_shape.append(jax.ShapeDtypeStruct((b, l, d), BF16))
    outs = pl.pallas_call(
        functools.partial(_ffn_kernel, fused_input=branch is not None, emit_next=next_mod is not None),
        grid=(b, l // tm),
        in_specs=in_specs,
        out_specs=out_specs,
        out_shape=out_shape,
        scratch_shapes=[
            pltpu.VMEM((tm + 2 * HALO, d), BF16),
            pltpu.VMEM((tm + 2 * HALO, fc), F32),
            pltpu.VMEM((tm + 2 * HALO, fc), F32),
            pltpu.VMEM((tm + 2 * HALO, fc), F32),
            pltpu.VMEM((tm + 2 * HALO, fc), F32),
            pltpu.VMEM((tm, f), BF16),
        ],
        compiler_params=_cparams(("parallel", "arbitrary")),
        name="conv_ffn",
    )(*operands)
    return outs if next_mod is not None else outs[0]


def _dft_tables(l, d):
    n1, n2, cg = DFT_N1, l // DFT_N1, d // FNET_GROUPS
    a = np.arange(cg)
    ang_c = 2.0 * np.pi * np.outer(a, a) / cg
    csc = np.concatenate([np.cos(ang_c), np.sin(ang_c)], axis=1) / math.sqrt(cg)
    a1 = np.arange(n1)
    ang1 = 2.0 * np.pi * np.outer(a1, a1) / n1
    c1, s1 = np.cos(ang1), np.sin(ang1)
    m1 = np.block([[c1, -s1], [s1, c1]])
    a2 = np.arange(n2)
    ang2 = 2.0 * np.pi * np.outer(a2, a2) / n2
    c2 = np.cos(ang2) / math.sqrt(l)
    s2 = np.sin(ang2) / math.sqrt(l)
    ang_t = 2.0 * np.pi * np.outer(a1, a2) / l
    f32 = lambda t: jnp.asarray(t, F32)
    return (f32(csc).astype(BF16), f32(m1).astype(BF16), f32(c2).astype(BF16), f32(s2).astype(BF16),
            f32(np.cos(ang_t)), f32(np.sin(ang_t)))


def _fourier1_kernel(h_ref, csc_ref, m1_ref, twc_ref, tws_ref, zr_ref, zi_ref, *, d):
    cg = csc_ref.shape[0]
    n1 = h_ref.shape[1]
    t2s = h_ref.shape[2] // d
    csc = csc_ref[...]
    m1 = m1_ref[...]
    stacked = []
    for t in range(t2s):
        h = h_ref[0, :, t * d:(t + 1) * d]
        ab = [jnp.dot(h[:, g * cg:(g + 1) * cg], csc, preferred_element_type=F32)
              for g in range(d // cg)]
        a = jnp.concatenate([t_[:, :cg] for t_ in ab], axis=1)
        bm = jnp.concatenate([t_[:, cg:] for t_ in ab], axis=1)
        stacked.append(jnp.concatenate([a, bm], axis=0).astype(BF16))
    ys = [jnp.dot(m1, ab2, preferred_element_type=F32) for ab2 in stacked]
    for t, y in enumerate(ys):
        yr = y[:n1]
        yin = y[n1:]
        cc = jnp.tile(twc_ref[:, t * HEAD_DIM:(t + 1) * HEAD_DIM], (1, d // HEAD_DIM))
        ss = jnp.tile(tws_ref[:, t * HEAD_DIM:(t + 1) * HEAD_DIM], (1, d // HEAD_DIM))
        zr_ref[0, :, t * d:(t + 1) * d] = (yr * cc - yin * ss).astype(BF16)
        zi_ref[0, :, t * d:(t + 1) * d] = (-(yin * cc) - yr * ss).astype(BF16)


def _fourier1(h, csc, m1, twc, tws, t2s):
    b, l, d = h.shape
    n1, n2 = DFT_N1, l // DFT_N1
    hv = h.reshape(b, n1, n2 * d)
    lanes = HEAD_DIM
    twc_x = jnp.repeat(twc, lanes, axis=1)
    tws_x = jnp.repeat(tws, lanes, axis=1)
    cg = csc.shape[0]
    return pl.pallas_call(
        functools.partial(_fourier1_kernel, d=d),
        grid=(b, n2 // t2s),
        in_specs=[
            pl.BlockSpec((1, n1, t2s * d), lambda bi, j: (bi, 0, j)),
            pl.BlockSpec((cg, 2 * cg), lambda bi, j: (0, 0)),
            pl.BlockSpec((2 * n1, 2 * n1), lambda bi, j: (0, 0)),
            pl.BlockSpec((n1, t2s * lanes), lambda bi, j: (0, j)),
            pl.BlockSpec((n1, t2s * lanes), lambda bi, j: (0, j)),
        ],
        out_specs=[
            pl.BlockSpec((1, n1, t2s * d), lambda bi, j: (bi, 0, j)),
            pl.BlockSpec((1, n1, t2s * d), lambda bi, j: (bi, 0, j)),
        ],
        out_shape=[jax.ShapeDtypeStruct((b, n1, n2 * d), BF16)] * 2,
        compiler_params=_cparams(("parallel", "parallel")),
        name="fnet_stage1",
    )(hv, csc, m1, twc_x, tws_x)


def _fourier2_kernel(zr_ref, zi_ref, c2_ref, s2_ref, w_ref, mod_ref, ng_ref, r_ref):
    k1s = zr_ref.shape[1]
    n2 = zr_ref.shape[2]
    d = zr_ref.shape[3]
    c2 = c2_ref[...]
    s2 = s2_ref[...]
    fs = []
    for k in range(k1s):
        f = (jnp.dot(c2, zr_ref[0, k], preferred_element_type=F32)
             + jnp.dot(s2, zi_ref[0, k], preferred_element_type=F32))
        fs.append(f.astype(BF16))
    per = k1s // F2_GROUPS
    ys = [jnp.dot(jnp.concatenate(fs[g * per:(g + 1) * per], axis=0), w_ref[...], preferred_element_type=F32)
          for g in range(F2_GROUPS)]
    for k in range(k1s):
        y = ys[k // per][(k % per) * n2:(k % per + 1) * n2]
        r_ref[0, :, k * d:(k + 1) * d] = (mod_ref[0, 2:3, :] * _rms(y, ng_ref[1:2, :])).astype(BF16)


def _fourier2(zr, zi, c2, s2, w_f, mod6, ng, k1s):
    b, n1, nd = zr.shape
    n2 = c2.shape[0]
    d = nd // n2
    zr4 = zr.reshape(b, n1, n2, d)
    zi4 = zi.reshape(b, n1, n2, d)
    r = pl.pallas_call(
        _fourier2_kernel,
        grid=(b, n1 // k1s),
        in_specs=[
            pl.BlockSpec((1, k1s, n2, d), lambda bi, j: (bi, j, 0, 0)),
            pl.BlockSpec((1, k1s, n2, d), lambda bi, j: (bi, j, 0, 0)),
            pl.BlockSpec((n2, n2), lambda bi, j: (0, 0)),
            pl.BlockSpec((n2, n2), lambda bi, j: (0, 0)),
            pl.BlockSpec((d, d), lambda bi, j: (0, 0)),
            pl.BlockSpec((1, N_MOD, d), lambda bi, j: (bi, 0, 0)),
            pl.BlockSpec((4, d), lambda bi, j: (0, 0)),
        ],
        out_specs=pl.BlockSpec((1, n2, k1s * d), lambda bi, j: (bi, 0, j)),
        out_shape=jax.ShapeDtypeStruct((b, n2, n1 * d), BF16),
        compiler_params=_cparams(("parallel", "parallel")),
        name="fnet_stage2",
    )(zr4, zi4, c2, s2, w_f, mod6, ng)
    return r.reshape(b, n2 * n1, d)


def _rope_tables(l):
    n_freq = HEAD_DIM // 4
    inv = ROPE_THETA ** (-jnp.arange(n_freq, dtype=F32) / n_freq)
    ang_row = jnp.arange(l // GRID_W).astype(F32)[:, None] * inv
    ang_col = jnp.arange(GRID_W).astype(F32)[:, None] * inv
    zr = jnp.zeros_like(ang_row)
    zc = jnp.zeros_like(ang_col)
    cos_r = jnp.concatenate([jnp.cos(ang_row), jnp.cos(ang_row), zr, zr], axis=1)
    sin_r = jnp.concatenate([-jnp.sin(ang_row), jnp.sin(ang_row), zr, zr], axis=1)
    cos_c = jnp.concatenate([zc, zc, jnp.cos(ang_col), jnp.cos(ang_col)], axis=1)
    sin_c = jnp.concatenate([zc, zc, -jnp.sin(ang_col), jnp.sin(ang_col)], axis=1)
    return cos_r, sin_r, cos_c, sin_c


def kernel(x, c, ctx, c_ctx, mod_w, mod_b, norm_g, attn_w_in, attn_w_out, q_norm_g, k_norm_g, sink,
           fourier_w_out, ffn_w_up, ffn_conv_w, ffn_conv_b, ffn_w_down):
    b, l, d = x.shape
    c_len = ctx.shape[1]
    assert mod_w.shape[0] == 2 and b + 1 <= COND_ROWS
    assert l % max(QKV_TM, WIN_TQ, FFN_TM, POST_TM, DFT_N1 * F1_T2) == 0

    cond = jnp.zeros((COND_ROWS, d), F32).at[:b].set(c).at[b].set(c_ctx)
    mods = _mods(cond, mod_w, mod_b)
    mod_l0 = mods[0].reshape(COND_ROWS, N_MOD, d)
    mod_l1 = mods[1].reshape(COND_ROWS, N_MOD, d)

    rope = _rope_tables(l)
    w_in = attn_w_in[0].astype(BF16)
    qg = q_norm_g[0].reshape(1, HEAD_DIM)
    kg = k_norm_g[0].reshape(1, HEAD_DIM)
    qa, ka, vat, qb, kb, vb = _qkv(x, mod_l0, lambda bi: bi, norm_g[0], w_in, qg, kg, rope, QKV_TM)
    zeros_r = jnp.zeros((8, HEAD_DIM), F32)
    zeros_c = jnp.zeros((GRID_W, HEAD_DIM), F32)
    no_rope = (jnp.ones_like(zeros_r), zeros_r, zeros_c, zeros_c)
    _, kac, vact, _, kbc, vbc = _qkv(ctx, mod_l0, lambda bi: b, norm_g[0], w_in, qg, kg, no_rope, c_len)
    oa = _attn_a(qa, kac, vact, ka, vat, ATT_TQ, ATT_TK)
    ob = _attn_b(sink[0], qb, kb, vb, kbc, vbc, WIN_TQ)
    x1, h2 = _post_attn(oa, ob, attn_w_out[0].astype(BF16), x, mod_l0, norm_g[0], POST_TM)
    x2, h_next = _ffn(x1, mod_l0, norm_g[0], ffn_w_up[0], ffn_conv_w[0], ffn_conv_b[0], ffn_w_down[0],
                      FFN_TM, FFN_FC, h2=h2, next_mod=mod_l1, next_ng=norm_g[1])

    csc, m1, c2, s2, twc, tws = _dft_tables(l, d)
    zr, zi = _fourier1(h_next, csc, m1, twc, tws, F1_T2)
    branch = _fourier2(zr, zi, c2, s2, fourier_w_out[0].astype(BF16), mod_l1, norm_g[1], F2_K1)
    return _ffn(x2, mod_l1, norm_g[1], ffn_w_up[1], ffn_conv_w[1], ffn_conv_b[1], ffn_w_down[1],
                FFN_TM, FFN_FC, branch=branch)
```

```python
import functools
import math

import numpy as np
import jax
import jax.numpy as jnp
from jax import lax
from jax.experimental import pallas as pl
from jax.experimental.pallas import tpu as pltpu

F32 = jnp.float32
BF16 = jnp.bfloat16

HEAD_DIM = 128
GRID_W = 64
A_HEADS = 4
A_KV_HEADS = 2
B_HEADS = 4
B_KV_HEADS = 2
GROUP = A_HEADS // A_KV_HEADS
BLOCK = 128
WINDOW = 128
ROPE_THETA = 10000.0
FNET_GROUPS = 4
N_MOD = 6
EPS = 1e-6
LOG2E = math.log2(math.e)
Q_SCALE = HEAD_DIM ** -0.5 * LOG2E
NEG_BIG = -1e30

DFT_N1 = 128
COND_ROWS = 8
VMEM_LIMIT = 56 * 1024 * 1024

QKV_TM = 512
QKV_SUB = 2
ATT_TQ = 2048
ATT_TK = 512
ATT_UNROLL = 2
V_ROWS = HEAD_DIM + 16
WIN_TQ = 2048
POST_TM = 2048
POST_SUB = 16
FFN_TM = 1024
FFN_FC = 256
FFN_UNROLL = 10
HALO = 16
F1_T2 = 16
F2_K1 = 16
F2_GROUPS = 8


def _cparams(sem):
    return pltpu.CompilerParams(dimension_semantics=sem, vmem_limit_bytes=VMEM_LIMIT)


def _rms(x, g):
    return x * lax.rsqrt(jnp.mean(x * x, axis=-1, keepdims=True) + EPS) * g


def _mod_kernel(cond_ref, w_ref, b_ref, o_ref):
    c = cond_ref[...]
    s = c / (1.0 + jnp.exp(-c))
    o_ref[0] = jnp.dot(s, w_ref[0], preferred_element_type=F32) + b_ref[0]


def _mods(cond, mod_w, mod_b):
    depth, d, n = mod_w.shape
    tn = n // 4
    return pl.pallas_call(
        _mod_kernel,
        grid=(depth, n // tn),
        in_specs=[
            pl.BlockSpec((COND_ROWS, d), lambda l, j: (0, 0)),
            pl.BlockSpec((1, d, tn), lambda l, j: (l, 0, j)),
            pl.BlockSpec((1, 1, tn), lambda l, j: (l, 0, j)),
        ],
        out_specs=pl.BlockSpec((1, COND_ROWS, tn), lambda l, j: (l, 0, j)),
        out_shape=jax.ShapeDtypeStruct((depth, COND_ROWS, n), F32),
        compiler_params=_cparams(("arbitrary", "arbitrary")),
        name="adaln_mods",
    )(cond, mod_w, mod_b.reshape(depth, 1, n))


def _qkv_kernel(x_ref, mod_ref, ng_ref, w_ref, qg_ref, kg_ref, cr_ref, sr_ref, cc_ref, sc_ref,
                qa_ref, ka_ref, vat_ref, qb_ref, kb_ref, vb_ref):
    tm = x_ref.shape[1]
    rows = tm // QKV_SUB
    assert rows % GRID_W == 0
    projs = []
    for r in range(QKV_SUB):
        x = x_ref[0, r * rows:(r + 1) * rows, :]
        h = _rms(x, ng_ref[0:1, :]) * (1.0 + mod_ref[0, 1:2, :]) + mod_ref[0, 0:1, :]
        projs.append(jnp.dot(h.astype(BF16), w_ref[...], preferred_element_type=F32))

    lane = lax.broadcasted_iota(jnp.int32, (rows, HEAD_DIM), 1)
    low_half = (lane % (HEAD_DIM // 2)) < (HEAD_DIM // 4)
    qg = qg_ref[...]
    kg = kg_ref[...]
    col_cos = cc_ref[...]
    col_sin = sc_ref[...]

    for r, proj in enumerate(projs):
        sl = slice(r * rows, (r + 1) * rows)
        g0 = r * (rows // GRID_W)

        def table(row_ref, col_part):
            return jnp.concatenate([jnp.broadcast_to(row_ref[g0 + i:g0 + i + 1, :], (GRID_W, HEAD_DIM)) + col_part
                                    for i in range(rows // GRID_W)], axis=0)

        cos = table(cr_ref, col_cos)
        sin = table(sr_ref, col_sin)

        def rope(t):
            partner = jnp.where(low_half,
                                pltpu.roll(t, HEAD_DIM - HEAD_DIM // 4, 1),
                                pltpu.roll(t, HEAD_DIM // 4, 1))
            return t * cos + partner * sin

        def head(j):
            return proj[:, j * HEAD_DIM:(j + 1) * HEAD_DIM]

        col = 0
        for hh in range(A_HEADS):
            qa_ref[0, hh, :, sl] = (rope(_rms(head(col + hh), qg)) * Q_SCALE).T.astype(BF16)
        col += A_HEADS
        for hh in range(B_HEADS):
            qb_ref[0, hh, sl, :] = (rope(head(col + hh)) * Q_SCALE).astype(BF16)
        col += B_HEADS
        for hh in range(A_KV_HEADS):
            ka_ref[0, hh, sl, :] = rope(_rms(head(col + hh), kg)).astype(BF16)
        col += A_KV_HEADS
        for hh in range(A_KV_HEADS):
            vat_ref[0, hh, 0, 0:HEAD_DIM, sl] = head(col + hh).T.astype(BF16)
            vat_ref[0, hh, 0, HEAD_DIM:, sl] = jnp.ones((V_ROWS - HEAD_DIM, rows), BF16)
        col += A_KV_HEADS
        for hh in range(B_KV_HEADS):
            kb_ref[0, hh, sl, :] = rope(head(col + hh)).astype(BF16)
        col += B_KV_HEADS
        for hh in range(B_KV_HEADS):
            vb_ref[0, hh, sl, :] = head(col + hh).astype(BF16)


def _qkv(x, mod6, mod_row, ng, w_in, qg, kg, rope, tm):
    b, l, d = x.shape
    n = w_in.shape[1]
    nt = l // tm
    assert tm % (GRID_W * QKV_SUB) == 0
    rt = max(tm // GRID_W, 8)
    return pl.pallas_call(
        _qkv_kernel,
        grid=(b, nt),
        in_specs=[
            pl.BlockSpec((1, tm, d), lambda bi, i: (bi, i, 0)),
            pl.BlockSpec((1, N_MOD, d), lambda bi, i: (mod_row(bi), 0, 0)),
            pl.BlockSpec((4, d), lambda bi, i: (0, 0)),
            pl.BlockSpec((d, n), lambda bi, i: (0, 0)),
            pl.BlockSpec((1, HEAD_DIM), lambda bi, i: (0, 0)),
            pl.BlockSpec((1, HEAD_DIM), lambda bi, i: (0, 0)),
            pl.BlockSpec((rt, HEAD_DIM), lambda bi, i: (i, 0)),
            pl.BlockSpec((rt, HEAD_DIM), lambda bi, i: (i, 0)),
            pl.BlockSpec((GRID_W, HEAD_DIM), lambda bi, i: (0, 0)),
            pl.BlockSpec((GRID_W, HEAD_DIM), lambda bi, i: (0, 0)),
        ],
        out_specs=[
            pl.BlockSpec((1, A_HEADS, HEAD_DIM, tm), lambda bi, i: (bi, 0, 0, i)),
            pl.BlockSpec((1, A_KV_HEADS, tm, HEAD_DIM), lambda bi, i: (bi, 0, i, 0)),
            pl.BlockSpec((1, A_KV_HEADS, 1, V_ROWS, tm), lambda bi, i: (bi, 0, i, 0, 0)),
            pl.BlockSpec((1, B_HEADS, tm, HEAD_DIM), lambda bi, i: (bi, 0, i, 0)),
            pl.BlockSpec((1, B_KV_HEADS, tm, HEAD_DIM), lambda bi, i: (bi, 0, i, 0)),
            pl.BlockSpec((1, B_KV_HEADS, tm, HEAD_DIM), lambda bi, i: (bi, 0, i, 0)),
        ],
        out_shape=[
            jax.ShapeDtypeStruct((b, A_HEADS, HEAD_DIM, l), BF16),
            jax.ShapeDtypeStruct((b, A_KV_HEADS, l, HEAD_DIM), BF16),
            jax.ShapeDtypeStruct((b, A_KV_HEADS, nt, V_ROWS, tm), BF16),
            jax.ShapeDtypeStruct((b, B_HEADS, l, HEAD_DIM), BF16),
            jax.ShapeDtypeStruct((b, B_KV_HEADS, l, HEAD_DIM), BF16),
            jax.ShapeDtypeStruct((b, B_KV_HEADS, l, HEAD_DIM), BF16),
        ],
        compiler_params=_cparams(("parallel", "parallel")),
        name="qkv_proj",
    )(x, mod6, ng, w_in, qg, kg, *rope)


def _attn_a_kernel(q_ref, kc_ref, vct_ref, k_ref, vt_ref, o_ref,
                   sc_ref, pc_ref, s0_ref, s1_ref, p0_ref, p1_ref, acc_ref):
    tq = q_ref.shape[3]
    ts = vt_ref.shape[4]
    tk = s0_ref.shape[0]
    sub = tk // ts
    n_chunks = vt_ref.shape[2] // sub
    assert n_chunks >= 2 and ATT_UNROLL % 2 == 0
    qt = jnp.concatenate([q_ref[0, g] for g in range(GROUP)], axis=1)

    def k_chunk(j):
        return k_ref[0, 0, pl.ds(pl.multiple_of(j * tk, tk), tk), :]

    def scores(kblk, s_ref):
        s = jnp.dot(kblk, qt, preferred_element_type=F32)
        s_ref[...] = s
        return jnp.max(s, axis=0, keepdims=True)

    def softmax(s_ref, p_ref, mc, m):
        m_new = jnp.maximum(m, mc)
        p_ref[...] = jnp.exp2(s_ref[...] - m_new).astype(BF16)
        return m_new, jnp.exp2(m - m_new)

    def accumulate(vt_blocks, p_ref, alpha):
        pv = None
        for u, vtblk in enumerate(vt_blocks):
            part = jnp.dot(vtblk, p_ref[u * vtblk.shape[1]:(u + 1) * vtblk.shape[1], :],
                           preferred_element_type=F32)
            pv = part if pv is None else pv + part
        acc_ref[...] = alpha * acc_ref[...] + pv

    def v_chunk(j):
        return [vt_ref[0, 0, j * sub + u] for u in range(sub)]

    acc_ref[...] = jnp.zeros(acc_ref.shape, F32)
    m = jnp.full((1, GROUP * tq), NEG_BIG, F32)
    mc_ctx = scores(kc_ref[0, 0], sc_ref)
    mc0 = scores(k_chunk(0), s0_ref)
    m, alpha_ctx = softmax(sc_ref, pc_ref, mc_ctx, m)
    mc1 = scores(k_chunk(1), s1_ref)
    m, alpha0 = softmax(s0_ref, p0_ref, mc0, m)
    accumulate([vct_ref[0, 0, 0]], pc_ref, alpha_ctx)

    s_bufs = (s0_ref, s1_ref)
    p_bufs = (p0_ref, p1_ref)

    def stage(c, parity, carry, has_next):
        m, alpha_prev, mc = carry
        mc_next = scores(k_chunk(c + 1), s_bufs[1 - parity]) if has_next else None
        m, alpha = softmax(s_bufs[parity], p_bufs[parity], mc, m)
        accumulate(v_chunk(c - 1), p_bufs[1 - parity], alpha_prev)
        return m, alpha, mc_next

    unroll = ATT_UNROLL
    n_groups = (n_chunks - 2) // unroll

    def group(g, carry):
        for u in range(unroll):
            carry = stage(1 + g * unroll + u, (1 + u) % 2, carry, True)
        return carry

    carry = lax.fori_loop(0, n_groups, group, (m, alpha0, mc1))
    for c in range(1 + n_groups * unroll, n_chunks):
        carry = stage(c, c % 2, carry, c + 1 < n_chunks)
    accumulate(v_chunk(n_chunks - 1), p_bufs[(n_chunks - 1) % 2], carry[1])

    acc = acc_ref[...]
    o = (acc[:HEAD_DIM] / acc[HEAD_DIM:HEAD_DIM + 1]).T
    for g in range(GROUP):
        o_ref[0, :, g * HEAD_DIM:(g + 1) * HEAD_DIM] = o[g * tq:(g + 1) * tq].astype(o_ref.dtype)


def _attn_a(q, kc, vct, k, vt, tq, tk):
    b, _, _, l = q.shape
    c_len = kc.shape[2]
    n_stored, ts = vt.shape[2], vt.shape[4]
    assert tk % ts == 0 and l % tk == 0
    return pl.pallas_call(
        _attn_a_kernel,
        grid=(b, A_KV_HEADS, l // tq),
        in_specs=[
            pl.BlockSpec((1, GROUP, HEAD_DIM, tq), lambda bi, kv, i: (bi, kv, 0, i)),
            pl.BlockSpec((1, 1, c_len, HEAD_DIM), lambda bi, kv, i: (bi, kv, 0, 0)),
            pl.BlockSpec((1, 1, 1, V_ROWS, c_len), lambda bi, kv, i: (bi, kv, 0, 0, 0)),
            pl.BlockSpec((1, 1, l, HEAD_DIM), lambda bi, kv, i: (bi, kv, 0, 0)),
            pl.BlockSpec((1, 1, n_stored, V_ROWS, ts), lambda bi, kv, i: (bi, kv, 0, 0, 0)),
        ],
        out_specs=pl.BlockSpec((1, tq, GROUP * HEAD_DIM), lambda bi, kv, i: (bi, i, kv)),
        out_shape=jax.ShapeDtypeStruct((b, l, A_HEADS * HEAD_DIM), BF16),
        scratch_shapes=[
            pltpu.VMEM((c_len, GROUP * tq), F32),
            pltpu.VMEM((c_len, GROUP * tq), BF16),
            pltpu.VMEM((tk, GROUP * tq), F32),
            pltpu.VMEM((tk, GROUP * tq), F32),
            pltpu.VMEM((tk, GROUP * tq), BF16),
            pltpu.VMEM((tk, GROUP * tq), BF16),
            pltpu.VMEM((V_ROWS, GROUP * tq), F32),
        ],
        compiler_params=_cparams(("parallel", "parallel", "arbitrary")),
        name="attn_global",
    )(q, kc, vct, k, vt)


def _attn_b_kernel(sink_ref, q_ref, kp_ref, km_ref, kn_ref, vp_ref, vm_ref, vn_ref,
                   kc_ref, vc_ref, o_ref, *, seq_len):
    kv = pl.program_id(1)
    i = pl.program_id(2)
    tq = q_ref.shape[2]
    sub = tq // BLOCK
    band = 3 * BLOCK
    kcat = jnp.concatenate([kp_ref[0, 0], km_ref[0, 0], kn_ref[0, 0]], axis=0)
    vcat = jnp.concatenate([vp_ref[0, 0], vm_ref[0, 0], vn_ref[0, 0]], axis=0)
    kc = kc_ref[0, 0]
    vc = vc_ref[0, 0]
    rows = GROUP * BLOCK
    nt = (((1,), (1,)), ((), ()))

    r = lax.broadcasted_iota(jnp.int32, (rows, band), 0) % BLOCK
    c = lax.broadcasted_iota(jnp.int32, (rows, band), 1)
    in_window = jnp.abs(c - BLOCK - r) <= WINDOW
    head_row = lax.broadcasted_iota(jnp.int32, (rows, 1), 0) // BLOCK
    sink2 = jnp.zeros((rows, 1), F32)
    for g in range(GROUP):
        sink2 = jnp.where(head_row == g, sink_ref[kv * GROUP + g] * LOG2E, sink2)

    def scores(j):
        q2 = jnp.concatenate([q_ref[0, g, j * BLOCK:(j + 1) * BLOCK, :] for g in range(GROUP)], axis=0)
        return (lax.dot_general(q2, kcat[j * BLOCK:j * BLOCK + band], nt, preferred_element_type=F32),
                lax.dot_general(q2, kc, nt, preferred_element_type=F32))

    def softmax(j, s_loc, s_ctx):
        kpos = (i * sub + (j - 1)) * BLOCK + c
        valid = in_window & (kpos >= 0) & (kpos < seq_len)
        s_loc = jnp.where(valid, s_loc, NEG_BIG)
        m = jnp.maximum(jnp.maximum(jnp.max(s_loc, axis=-1, keepdims=True),
                                    jnp.max(s_ctx, axis=-1, keepdims=True)), sink2)
        p_loc = jnp.exp2(s_loc - m)
        p_ctx = jnp.exp2(s_ctx - m)
        denom = (jnp.sum(p_loc, axis=-1, keepdims=True) + jnp.sum(p_ctx, axis=-1, keepdims=True)
                 + jnp.exp2(sink2 - m))
        return p_loc.astype(BF16), p_ctx.astype(BF16), denom

    def values(j, p_loc, p_ctx, denom):
        o = (jnp.dot(p_ctx, vc, preferred_element_type=F32)
             + jnp.dot(p_loc, vcat[j * BLOCK:j * BLOCK + band], preferred_element_type=F32)) / denom
        for g in range(GROUP):
            o_ref[0, j * BLOCK:(j + 1) * BLOCK, g * HEAD_DIM:(g + 1) * HEAD_DIM] = (
                o[g * BLOCK:(g + 1) * BLOCK].astype(o_ref.dtype))

    s_prev = None
    p_prev = None
    for j in range(sub + 2):
        s_cur = scores(j) if j < sub else None
        p_cur = softmax(j - 1, *s_prev) if 1 <= j <= sub else None
        if j >= 2:
            values(j - 2, *p_prev)
        s_prev, p_prev = s_cur, p_cur


def _attn_b(sink, q, k, v, kc, vc, tq):
    b, _, l, _ = q.shape
    c_len = kc.shape[2]
    sub = tq // BLOCK
    nb = l // BLOCK

    def main(bi, kv, i):
        return (bi, kv, i, 0)

    def prev(bi, kv, i):
        return (bi, kv, jnp.maximum(i * sub - 1, 0), 0)

    def nxt(bi, kv, i):
        return (bi, kv, jnp.minimum((i + 1) * sub, nb - 1), 0)

    def ctx_map(bi, kv, i):
        return (bi, kv, 0, 0)

    edge = lambda fn: pl.BlockSpec((1, 1, BLOCK, HEAD_DIM), fn)
    return pl.pallas_call(
        functools.partial(_attn_b_kernel, seq_len=l),
        grid=(b, B_KV_HEADS, l // tq),
        in_specs=[
            pl.BlockSpec(memory_space=pltpu.SMEM),
            pl.BlockSpec((1, GROUP, tq, HEAD_DIM), main),
            edge(prev), pl.BlockSpec((1, 1, tq, HEAD_DIM), main), edge(nxt),
            edge(prev), pl.BlockSpec((1, 1, tq, HEAD_DIM), main), edge(nxt),
            pl.BlockSpec((1, 1, c_len, HEAD_DIM), ctx_map),
            pl.BlockSpec((1, 1, c_len, HEAD_DIM), ctx_map),
        ],
        out_specs=pl.BlockSpec((1, tq, GROUP * HEAD_DIM), lambda bi, kv, i: (bi, i, kv)),
        out_shape=jax.ShapeDtypeStruct((b, l, B_HEADS * HEAD_DIM), BF16),
        compiler_params=_cparams(("parallel", "parallel", "arbitrary")),
        name="attn_window",
    )(sink, q, k, k, k, v, v, v, kc, vc)


def _residual_and_next(y, x, mod_ref, ng_ref):
    x1 = x + mod_ref[0, 2:3, :] * _rms(y, ng_ref[1:2, :])
    h2 = _rms(x1, ng_ref[2:3, :]) * (1.0 + mod_ref[0, 4:5, :]) + mod_ref[0, 3:4, :]
    return x1, h2


def _post_attn_kernel(oa_ref, ob_ref, w_ref, x_ref, mod_ref, ng_ref, x1_ref, h2_ref):
    na = oa_ref.shape[2]
    rows = oa_ref.shape[1] // POST_SUB
    ys = []
    for r in range(POST_SUB):
        sl = slice(r * rows, (r + 1) * rows)
        ys.append(jnp.dot(oa_ref[0, sl, :], w_ref[0:na, :], preferred_element_type=F32)
                  + jnp.dot(ob_ref[0, sl, :], w_ref[na:, :], preferred_element_type=F32))
    for r, y in enumerate(ys):
        sl = slice(r * rows, (r + 1) * rows)
        x1, h2 = _residual_and_next(y, x_ref[0, sl, :], mod_ref, ng_ref)
        x1_ref[0, sl, :] = x1
        h2_ref[0, sl, :] = h2.astype(BF16)


def _post_attn(oa, ob, w_out, x, mod6, ng, tm):
    b, l, d = x.shape
    na, nb_ = oa.shape[2], ob.shape[2]
    return pl.pallas_call(
        _post_attn_kernel,
        grid=(b, l // tm),
        in_specs=[
            pl.BlockSpec((1, tm, na), lambda bi, i: (bi, i, 0)),
            pl.BlockSpec((1, tm, nb_), lambda bi, i: (bi, i, 0)),
            pl.BlockSpec((na + nb_, d), lambda bi, i: (0, 0)),
            pl.BlockSpec((1, tm, d), lambda bi, i: (bi, i, 0)),
            pl.BlockSpec((1, N_MOD, d), lambda bi, i: (bi, 0, 0)),
            pl.BlockSpec((4, d), lambda bi, i: (0, 0)),
        ],
        out_specs=[
            pl.BlockSpec((1, tm, d), lambda bi, i: (bi, i, 0)),
            pl.BlockSpec((1, tm, d), lambda bi, i: (bi, i, 0)),
        ],
        out_shape=[jax.ShapeDtypeStruct((b, l, d), F32), jax.ShapeDtypeStruct((b, l, d), BF16)],
        compiler_params=_cparams(("parallel", "parallel")),
        name="attn_out_proj",
    )(oa, ob, w_out, x, mod6, ng)


def _ffn_kernel(*refs, fused_input, emit_next):
    refs = list(refs)
    if fused_input:
        xp_ref, xm_ref, xn_ref, rp_ref, rm_ref, rn_ref = refs[:6]
        refs = refs[6:]
    else:
        hp_ref, hm_ref, hn_ref, xm_ref = refs[:4]
        refs = refs[4:]
    mod_ref, ng_ref, wu_ref, cw_ref, cb_ref, wd_ref = refs[:6]
    refs = refs[6:]
    if emit_next:
        modn_ref, ngn_ref = refs[:2]
        refs = refs[2:]
    o_ref = refs[0]
    refs = refs[1:]
    if emit_next:
        hnext_ref = refs[0]
        refs = refs[1:]
    hcat_ref, ug0_ref, uv0_ref, ug1_ref, uv1_ref, act_ref = refs

    i = pl.program_id(1)
    last = pl.num_programs(1) - 1
    tm = xm_ref.shape[1]
    f = wd_ref.shape[0]
    fc = ug0_ref.shape[1]
    n_chunks = f // fc
    u_bufs = ((ug0_ref, uv0_ref), (ug1_ref, uv1_ref))

    def pre_norm(x):
        return (_rms(x, ng_ref[2:3, :]) * (1.0 + mod_ref[0, 4:5, :]) + mod_ref[0, 3:4, :]).astype(BF16)

    if fused_input:
        x_in = xm_ref[0] + rm_ref[0].astype(F32)
        hp = pre_norm(xp_ref[0] + rp_ref[0].astype(F32))
        hm = pre_norm(x_in)
        hn = pre_norm(xn_ref[0] + rn_ref[0].astype(F32))
    else:
        x_in = xm_ref[0]
        hp, hm, hn = hp_ref[0], hm_ref[0], hn_ref[0]
    hcat_ref[0:HALO, :] = jnp.where(i > 0, hp, jnp.zeros_like(hp))
    hcat_ref[HALO:HALO + tm, :] = hm
    hcat_ref[HALO + tm:, :] = jnp.where(i < last, hn, jnp.zeros_like(hn))

    def cols(c, half):
        return pl.ds(pl.multiple_of(half * f + c * fc, fc), fc)

    def conv(u_ref, c, half):
        cw = cw_ref[:, cols(c, half)]
        u = u_ref[...]
        rows = u.shape[0]
        prev = pltpu.roll(u, 1, 0)[HALO:HALO + tm]
        nxt = pltpu.roll(u, rows - 1, 0)[HALO:HALO + tm]
        return (prev * cw[0:1, :] + u[HALO:HALO + tm] * cw[1:2, :] + nxt * cw[2:3, :]
                + cb_ref[:, cols(c, half)])

    def up(c, bufs):
        hcat = hcat_ref[...]
        for half in range(2):
            bufs[half][...] = jnp.dot(hcat, wu_ref[:, cols(c, half)], preferred_element_type=F32)

    def mid(c, bufs):
        gate = conv(bufs[0], c, 0)
        val = conv(bufs[1], c, 1)
        act_ref[:, cols(c, 0)] = (gate / (1.0 + jnp.exp(-gate)) * val).astype(BF16)

    def stage(c, parity, do_up):
        if do_up:
            up(c + 1, u_bufs[1 - parity])
        mid(c, u_bufs[parity])

    up(0, u_bufs[0])
    unroll = FFN_UNROLL
    n_groups = (n_chunks - 1) // unroll

    def group(g, carry):
        for u in range(unroll):
            stage(g * unroll + u, u % 2, True)
        return carry

    lax.fori_loop(0, n_groups, group, 0)
    for c in range(n_groups * unroll, n_chunks):
        stage(c, c % 2, c + 1 < n_chunks)
    y = jnp.dot(act_ref[...], wd_ref[...], preferred_element_type=F32)
    out = x_in + mod_ref[0, 5:6, :] * _rms(y, ng_ref[3:4, :])
    o_ref[0] = out
    if emit_next:
        hnext_ref[0] = (_rms(out, ngn_ref[0:1, :]) * (1.0 + modn_ref[0, 1:2, :]) + modn_ref[0, 0:1, :]).astype(BF16)


def _ffn(x, mod6, ng, w_up, conv_w, conv_b, w_down, tm, fc, *, h2=None, branch=None, next_mod=None, next_ng=None):
    b, l, d = x.shape
    f = w_down.shape[0]
    assert f % fc == 0 and (h2 is None) != (branch is None)
    per = tm // HALO
    n_halo = l // HALO
    resident = lambda shape: pl.BlockSpec(shape, lambda bi, i: (0,) * len(shape),
                                          pipeline_mode=pl.Buffered(1))
    prev_halo = pl.BlockSpec((1, HALO, d), lambda bi, i: (bi, jnp.maximum(i * per - 1, 0), 0))
    next_halo = pl.BlockSpec((1, HALO, d), lambda bi, i: (bi, jnp.minimum((i + 1) * per, n_halo - 1), 0))
    tile = pl.BlockSpec((1, tm, d), lambda bi, i: (bi, i, 0))
    mod_spec = pl.BlockSpec((1, N_MOD, d), lambda bi, i: (bi, 0, 0))
    ng_spec = pl.BlockSpec((4, d), lambda bi, i: (0, 0))
    if branch is None:
        operands, in_specs = [h2, h2, h2, x], [prev_halo, tile, next_halo, tile]
    else:
        operands, in_specs = [x, x, x, branch, branch, branch], [prev_halo, tile, next_halo] * 2
    operands += [mod6, ng, w_up.astype(BF16), conv_w, conv_b.reshape(1, 2 * f), w_down.astype(BF16)]
    in_specs += [mod_spec, ng_spec, resident((d, 2 * f)), resident((3, 2 * f)), resident((1, 2 * f)), resident((f, d))]
    out_specs, out_shape = [tile], [jax.ShapeDtypeStruct((b, l, d), F32)]
    if next_mod is not None:
        operands += [next_mod, next_ng]
        in_specs += [mod_spec, ng_spec]
        out_specs.append(tile)
        out_shape.append(jax.ShapeDtypeStruct((b, l, d), BF16))
    outs = pl.pallas_call(
        functools.partial(_ffn_kernel, fused_input=branch is not None, emit_next=next_mod is not None),
        grid=(b, l // tm),
        in_specs=in_specs,
        out_specs=out_specs,
        out_shape=out_shape,
        scratch_shapes=[
            pltpu.VMEM((tm + 2 * HALO, d), BF16),
            pltpu.VMEM((tm + 2 * HALO, fc), F32),
            pltpu.VMEM((tm + 2 * HALO, fc), F32),
            pltpu.VMEM((tm + 2 * HALO, fc), F32),
            pltpu.VMEM((tm + 2 * HALO, fc), F32),
            pltpu.VMEM((tm, f), BF16),
        ],
        compiler_params=_cparams(("parallel", "arbitrary")),
        name="conv_ffn",
    )(*operands)
    return outs if next_mod is not None else outs[0]


def _dft_tables(l, d):
    n1, n2, cg = DFT_N1, l // DFT_N1, d // FNET_GROUPS
    a = np.arange(cg)
    ang_c = 2.0 * np.pi * np.outer(a, a) / cg
    csc = np.concatenate([np.cos(ang_c), np.sin(ang_c)], axis=1) / math.sqrt(cg)
    a1 = np.arange(n1)
    ang1 = 2.0 * np.pi * np.outer(a1, a1) / n1
    c1, s1 = np.cos(ang1), np.sin(ang1)
    m1 = np.block([[c1, -s1], [s1, c1]])
    a2 = np.arange(n2)
    ang2 = 2.0 * np.pi * np.outer(a2, a2) / n2
    c2 = np.cos(ang2) / math.sqrt(l)
    s2 = np.sin(ang2) / math.sqrt(l)
    ang_t = 2.0 * np.pi * np.outer(a1, a2) / l
    f32 = lambda t: jnp.asarray(t, F32)
    return (f32(csc).astype(BF16), f32(m1).astype(BF16), f32(c2).astype(BF16), f32(s2).astype(BF16),
            f32(np.cos(ang_t)), f32(np.sin(ang_t)))


def _fourier1_kernel(h_ref, csc_ref, m1_ref, twc_ref, tws_ref, zr_ref, zi_ref, *, d):
    cg = csc_ref.shape[0]
    n1 = h_ref.shape[1]
    t2s = h_ref.shape[2] // d
    csc = csc_ref[...]
    m1 = m1_ref[...]
    stacked = []
    for t in range(t2s):
        h = h_ref[0, :, t * d:(t + 1) * d]
        ab = [jnp.dot(h[:, g * cg:(g + 1) * cg], csc, preferred_element_type=F32)
              for g in range(d // cg)]
        a = jnp.concatenate([t_[:, :cg] for t_ in ab], axis=1)
        bm = jnp.concatenate([t_[:, cg:] for t_ in ab], axis=1)
        stacked.append(jnp.concatenate([a, bm], axis=0).astype(BF16))
    ys = [jnp.dot(m1, ab2, preferred_element_type=F32) for ab2 in stacked]
    for t, y in enumerate(ys):
        yr = y[:n1]
        yin = y[n1:]
        cc = jnp.tile(twc_ref[:, t * HEAD_DIM:(t + 1) * HEAD_DIM], (1, d // HEAD_DIM))
        ss = jnp.tile(tws_ref[:, t * HEAD_DIM:(t + 1) * HEAD_DIM], (1, d // HEAD_DIM))
        zr_ref[0, :, t * d:(t + 1) * d] = (yr * cc - yin * ss).astype(BF16)
        zi_ref[0, :, t * d:(t + 1) * d] = (-(yin * cc) - yr * ss).astype(BF16)


def _fourier1(h, csc, m1, twc, tws, t2s):
    b, l, d = h.shape
    n1, n2 = DFT_N1, l // DFT_N1
    hv = h.reshape(b, n1, n2 * d)
    lanes = HEAD_DIM
    twc_x = jnp.repeat(twc, lanes, axis=1)
    tws_x = jnp.repeat(tws, lanes, axis=1)
    cg = csc.shape[0]
    return pl.pallas_call(
        functools.partial(_fourier1_kernel, d=d),
        grid=(b, n2 // t2s),
        in_specs=[
            pl.BlockSpec((1, n1, t2s * d), lambda bi, j: (bi, 0, j)),
            pl.BlockSpec((cg, 2 * cg), lambda bi, j: (0, 0)),
            pl.BlockSpec((2 * n1, 2 * n1), lambda bi, j: (0, 0)),
            pl.BlockSpec((n1, t2s * lanes), lambda bi, j: (0, j)),
            pl.BlockSpec((n1, t2s * lanes), lambda bi, j: (0, j)),
        ],
        out_specs=[
            pl.BlockSpec((1, n1, t2s * d), lambda bi, j: (bi, 0, j)),
            pl.BlockSpec((1, n1, t2s * d), lambda bi, j: (bi, 0, j)),
        ],
        out_shape=[jax.ShapeDtypeStruct((b, n1, n2 * d), BF16)] * 2,
        compiler_params=_cparams(("parallel", "parallel")),
        name="fnet_stage1",
    )(hv, csc, m1, twc_x, tws_x)


def _fourier2_kernel(zr_ref, zi_ref, c2_ref, s2_ref, w_ref, mod_ref, ng_ref, r_ref):
    k1s = zr_ref.shape[1]
    n2 = zr_ref.shape[2]
    d = zr_ref.shape[3]
    c2 = c2_ref[...]
    s2 = s2_ref[...]
    fs = []
    for k in range(k1s):
        f = (jnp.dot(c2, zr_ref[0, k], preferred_element_type=F32)
             + jnp.dot(s2, zi_ref[0, k], preferred_element_type=F32))
        fs.append(f.astype(BF16))
    per = k1s // F2_GROUPS
    ys = [jnp.dot(jnp.concatenate(fs[g * per:(g + 1) * per], axis=0), w_ref[...], preferred_element_type=F32)
          for g in range(F2_GROUPS)]
    for k in range(k1s):
        y = ys[k // per][(k % per) * n2:(k % per + 1) * n2]
        r_ref[0, :, k * d:(k + 1) * d] = (mod_ref[0, 2:3, :] * _rms(y, ng_ref[1:2, :])).astype(BF16)


def _fourier2(zr, zi, c2, s2, w_f, mod6, ng, k1s):
    b, n1, nd = zr.shape
    n2 = c2.shape[0]
    d = nd // n2
    zr4 = zr.reshape(b, n1, n2, d)
    zi4 = zi.reshape(b, n1, n2, d)
    r = pl.pallas_call(
        _fourier2_kernel,
        grid=(b, n1 // k1s),
        in_specs=[
            pl.BlockSpec((1, k1s, n2, d), lambda bi, j: (bi, j, 0, 0)),
            pl.BlockSpec((1, k1s, n2, d), lambda bi, j: (bi, j, 0, 0)),
            pl.BlockSpec((n2, n2), lambda bi, j: (0, 0)),
            pl.BlockSpec((n2, n2), lambda bi, j: (0, 0)),
            pl.BlockSpec((d, d), lambda bi, j: (0, 0)),
            pl.BlockSpec((1, N_MOD, d), lambda bi, j: (bi, 0, 0)),
            pl.BlockSpec((4, d), lambda bi, j: (0, 0)),
        ],
        out_specs=pl.BlockSpec((1, n2, k1s * d), lambda bi, j: (bi, 0, j)),
        out_shape=jax.ShapeDtypeStruct((b, n2, n1 * d), BF16),
        compiler_params=_cparams(("parallel", "parallel")),
        name="fnet_stage2",
    )(zr4, zi4, c2, s2, w_f, mod6, ng)
    return r.reshape(b, n2 * n1, d)


def _rope_tables(l):
    n_freq = HEAD_DIM // 4
    inv = ROPE_THETA ** (-jnp.arange(n_freq, dtype=F32) / n_freq)
    ang_row = jnp.arange(l // GRID_W).astype(F32)[:, None] * inv
    ang_col = jnp.arange(GRID_W).astype(F32)[:, None] * inv
    zr = jnp.zeros_like(ang_row)
    zc = jnp.zeros_like(ang_col)
    cos_r = jnp.concatenate([jnp.cos(ang_row), jnp.cos(ang_row), zr, zr], axis=1)
    sin_r = jnp.concatenate([-jnp.sin(ang_row), jnp.sin(ang_row), zr, zr], axis=1)
    cos_c = jnp.concatenate([zc, zc, jnp.cos(ang_col), jnp.cos(ang_col)], axis=1)
    sin_c = jnp.concatenate([zc, zc, -jnp.sin(ang_col), jnp.sin(ang_col)], axis=1)
    return cos_r, sin_r, cos_c, sin_c


def kernel(x, c, ctx, c_ctx, mod_w, mod_b, norm_g, attn_w_in, attn_w_out, q_norm_g, k_norm_g, sink,
           fourier_w_out, ffn_w_up, ffn_conv_w, ffn_conv_b, ffn_w_down):
    b, l, d = x.shape
    c_len = ctx.shape[1]
    assert mod_w.shape[0] == 2 and b + 1 <= COND_ROWS
    assert l % max(QKV_TM, WIN_TQ, FFN_TM, POST_TM, DFT_N1 * F1_T2) == 0

    cond = jnp.zeros((COND_ROWS, d), F32).at[:b].set(c).at[b].set(c_ctx)
    mods = _mods(cond, mod_w, mod_b)
    mod_l0 = mods[0].reshape(COND_ROWS, N_MOD, d)
    mod_l1 = mods[1].reshape(COND_ROWS, N_MOD, d)

    rope = _rope_tables(l)
    w_in = attn_w_in[0].astype(BF16)
    qg = q_norm_g[0].reshape(1, HEAD_DIM)
    kg = k_norm_g[0].reshape(1, HEAD_DIM)
    qa, ka, vat, qb, kb, vb = _qkv(x, mod_l0, lambda bi: bi, norm_g[0], w_in, qg, kg, rope, QKV_TM)
    zeros_r = jnp.zeros((8, HEAD_DIM), F32)
    zeros_c = jnp.zeros((GRID_W, HEAD_DIM), F32)
    no_rope = (jnp.ones_like(zeros_r), zeros_r, zeros_c, zeros_c)
    _, kac, vact, _, kbc, vbc = _qkv(ctx, mod_l0, lambda bi: b, norm_g[0], w_in, qg, kg, no_rope, c_len)
    oa = _attn_a(qa, kac, vact, ka, vat, ATT_TQ, ATT_TK)
    ob = _attn_b(sink[0], qb, kb, vb, kbc, vbc, WIN_TQ)
    x1, h2 = _post_attn(oa, ob, attn_w_out[0].astype(BF16), x, mod_l0, norm_g[0], POST_TM)
    x2, h_next = _ffn(x1, mod_l0, norm_g[0], ffn_w_up[0], ffn_conv_w[0], ffn_conv_b[0], ffn_w_down[0],
                      FFN_TM, FFN_FC, h2=h2, next_mod=mod_l1, next_ng=norm_g[1])

    csc, m1, c2, s2, twc, tws = _dft_tables(l, d)
    zr, zi = _fourier1(h_next, csc, m1, twc, tws, F1_T2)
    branch = _fourier2(zr, zi, c2, s2, fourier_w_out[0].astype(BF16), mod_l1, norm_g[1], F2_K1)
    return _ffn(x2, mod_l1, norm_g[1], ffn_w_up[1], ffn_conv_w[1], ffn_conv_b[1], ffn_w_down[1],
                FFN_TM, FFN_FC, branch=branch)
```

```python
import functools
import math

import numpy as np
import jax
import jax.numpy as jnp
from jax import lax
from jax.experimental import pallas as pl
from jax.experimental.pallas import tpu as pltpu

F32 = jnp.float32
BF16 = jnp.bfloat16

HEAD_DIM = 128
GRID_W = 64
A_HEADS = 4
A_KV_HEADS = 2
B_HEADS = 4
B_KV_HEADS = 2
GROUP = A_HEADS // A_KV_HEADS
BLOCK = 128
WINDOW = 128
ROPE_THETA = 10000.0
FNET_GROUPS = 4
N_MOD = 6
EPS = 1e-6
LOG2E = math.log2(math.e)
Q_SCALE = HEAD_DIM ** -0.5 * LOG2E
NEG_BIG = -1e30

DFT_N1 = 128
COND_ROWS = 8
VMEM_LIMIT = 56 * 1024 * 1024

QKV_TM = 512
QKV_SUB = 2
ATT_TQ = 2048
ATT_TK = 512
ATT_UNROLL = 2
V_ROWS = HEAD_DIM + 16
WIN_TQ = 2048
POST_TM = 2048
POST_SUB = 16
FFN_TM = 1024
FFN_FC = 256
FFN_UNROLL = 10
HALO = 16
F1_T2 = 16
F2_K1 = 16
F2_GROUPS = 8


def _cparams(sem):
    return pltpu.CompilerParams(dimension_semantics=sem, vmem_limit_bytes=VMEM_LIMIT)


def _rms(x, g):
    return x * lax.rsqrt(jnp.mean(x * x, axis=-1, keepdims=True) + EPS) * g


def _mod_kernel(cond_ref, w_ref, b_ref, o_ref):
    c = cond_ref[...]
    s = c / (1.0 + jnp.exp(-c))
    o_ref[0] = jnp.dot(s, w_ref[0], preferred_element_type=F32) + b_ref[0]


def _mods(cond, mod_w, mod_b):
    depth, d, n = mod_w.shape
    tn = n // 4
    return pl.pallas_call(
        _mod_kernel,
        grid=(depth, n // tn),
        in_specs=[
            pl.BlockSpec((COND_ROWS, d), lambda l, j: (0, 0)),
            pl.BlockSpec((1, d, tn), lambda l, j: (l, 0, j)),
            pl.BlockSpec((1, 1, tn), lambda l, j: (l, 0, j)),
        ],
        out_specs=pl.BlockSpec((1, COND_ROWS, tn), lambda l, j: (l, 0, j)),
        out_shape=jax.ShapeDtypeStruct((depth, COND_ROWS, n), F32),
        compiler_params=_cparams(("arbitrary", "arbitrary")),
        name="adaln_mods",
    )(cond, mod_w, mod_b.reshape(depth, 1, n))


def _qkv_kernel(x_ref, mod_ref, ng_ref, w_ref, qg_ref, kg_ref, cr_ref, sr_ref, cc_ref, sc_ref, perm_ref,
                qa_ref, ka_ref, vat_ref, qb_ref, kb_ref, vb_ref):
    tm = x_ref.shape[1]
    rows = tm // QKV_SUB
    assert rows % GRID_W == 0
    projs = []
    for r in range(QKV_SUB):
        x = x_ref[0, r * rows:(r + 1) * rows, :]
        h = _rms(x, ng_ref[0:1, :]) * (1.0 + mod_ref[0, 1:2, :]) + mod_ref[0, 0:1, :]
        projs.append(jnp.dot(h.astype(BF16), w_ref[...], preferred_element_type=F32))

    lane = lax.broadcasted_iota(jnp.int32, (rows, HEAD_DIM), 1)
    low_half = (lane % (HEAD_DIM // 2)) < (HEAD_DIM // 4)
    qg = qg_ref[...]
    kg = kg_ref[...]
    col_cos = cc_ref[...]
    col_sin = sc_ref[...]

    for r, proj in enumerate(projs):
        sl = slice(r * rows, (r + 1) * rows)
        g0 = r * (rows // GRID_W)

        def table(row_ref, col_part):
            return jnp.concatenate([jnp.broadcast_to(row_ref[g0 + i:g0 + i + 1, :], (GRID_W, HEAD_DIM)) + col_part
                                    for i in range(rows // GRID_W)], axis=0)

        cos = table(cr_ref, col_cos)
        sin = table(sr_ref, col_sin)

        def rope(t):
            partner = jnp.dot(t.astype(BF16), perm_ref[...], preferred_element_type=F32)
            return t * cos + partner * sin

        def head(j):
            return proj[:, j * HEAD_DIM:(j + 1) * HEAD_DIM]

        col = 0
        for hh in range(A_HEADS):
            qa_ref[0, hh, :, sl] = (rope(_rms(head(col + hh), qg)) * Q_SCALE).T.astype(BF16)
        col += A_HEADS
        for hh in range(B_HEADS):
            qb_ref[0, hh, sl, :] = (rope(head(col + hh)) * Q_SCALE).astype(BF16)
        col += B_HEADS
        for hh in range(A_KV_HEADS):
            ka_ref[0, hh, sl, :] = rope(_rms(head(col + hh), kg)).astype(BF16)
        col += A_KV_HEADS
        for hh in range(A_KV_HEADS):
            vat_ref[0, hh, 0, 0:HEAD_DIM, sl] = head(col + hh).T.astype(BF16)
            vat_ref[0, hh, 0, HEAD_DIM:, sl] = jnp.ones((V_ROWS - HEAD_DIM, rows), BF16)
        col += A_KV_HEADS
        for hh in range(B_KV_HEADS):
            kb_ref[0, hh, sl, :] = rope(head(col + hh)).astype(BF16)
        col += B_KV_HEADS
        for hh in range(B_KV_HEADS):
            vb_ref[0, hh, sl, :] = head(col + hh).astype(BF16)


def _qkv(x, mod6, mod_row, ng, w_in, qg, kg, rope, tm):
    b, l, d = x.shape
    n = w_in.shape[1]
    nt = l // tm
    assert tm % (GRID_W * QKV_SUB) == 0
    rt = max(tm // GRID_W, 8)
    return pl.pallas_call(
        _qkv_kernel,
        grid=(b, nt),
        in_specs=[
            pl.BlockSpec((1, tm, d), lambda bi, i: (bi, i, 0)),
            pl.BlockSpec((1, N_MOD, d), lambda bi, i: (mod_row(bi), 0, 0)),
            pl.BlockSpec((4, d), lambda bi, i: (0, 0)),
            pl.BlockSpec((d, n), lambda bi, i: (0, 0)),
            pl.BlockSpec((1, HEAD_DIM), lambda bi, i: (0, 0)),
            pl.BlockSpec((1, HEAD_DIM), lambda bi, i: (0, 0)),
            pl.BlockSpec((rt, HEAD_DIM), lambda bi, i: (i, 0)),
            pl.BlockSpec((rt, HEAD_DIM), lambda bi, i: (i, 0)),
            pl.BlockSpec((GRID_W, HEAD_DIM), lambda bi, i: (0, 0)),
            pl.BlockSpec((GRID_W, HEAD_DIM), lambda bi, i: (0, 0)),
            pl.BlockSpec((HEAD_DIM, HEAD_DIM), lambda bi, i: (0, 0)),
        ],
        out_specs=[
            pl.BlockSpec((1, A_HEADS, HEAD_DIM, tm), lambda bi, i: (bi, 0, 0, i)),
            pl.BlockSpec((1, A_KV_HEADS, tm, HEAD_DIM), lambda bi, i: (bi, 0, i, 0)),
            pl.BlockSpec((1, A_KV_HEADS, 1, V_ROWS, tm), lambda bi, i: (bi, 0, i, 0, 0)),
            pl.BlockSpec((1, B_HEADS, tm, HEAD_DIM), lambda bi, i: (bi, 0, i, 0)),
            pl.BlockSpec((1, B_KV_HEADS, tm, HEAD_DIM), lambda bi, i: (bi, 0, i, 0)),
            pl.BlockSpec((1, B_KV_HEADS, tm, HEAD_DIM), lambda bi, i: (bi, 0, i, 0)),
        ],
        out_shape=[
            jax.ShapeDtypeStruct((b, A_HEADS, HEAD_DIM, l), BF16),
            jax.ShapeDtypeStruct((b, A_KV_HEADS, l, HEAD_DIM), BF16),
            jax.ShapeDtypeStruct((b, A_KV_HEADS, nt, V_ROWS, tm), BF16),
            jax.ShapeDtypeStruct((b, B_HEADS, l, HEAD_DIM), BF16),
            jax.ShapeDtypeStruct((b, B_KV_HEADS, l, HEAD_DIM), BF16),
            jax.ShapeDtypeStruct((b, B_KV_HEADS, l, HEAD_DIM), BF16),
        ],
        compiler_params=_cparams(("parallel", "parallel")),
        name="qkv_proj",
    )(x, mod6, ng, w_in, qg, kg, *rope, _rope_permutation())


def _attn_a_kernel(q_ref, kc_ref, vct_ref, k_ref, vt_ref, o_ref,
                   sc_ref, pc_ref, s0_ref, s1_ref, p0_ref, p1_ref, acc_ref):
    tq = q_ref.shape[3]
    ts = vt_ref.shape[4]
    tk = s0_ref.shape[0]
    sub = tk // ts
    n_chunks = vt_ref.shape[2] // sub
    assert n_chunks >= 2 and ATT_UNROLL % 2 == 0
    qt = jnp.concatenate([q_ref[0, g] for g in range(GROUP)], axis=1)

    def k_chunk(j):
        return k_ref[0, 0, pl.ds(pl.multiple_of(j * tk, tk), tk), :]

    def scores(kblk, s_ref):
        s = jnp.dot(kblk, qt, preferred_element_type=F32)
        s_ref[...] = s
        return jnp.max(s, axis=0, keepdims=True)

    def softmax(s_ref, p_ref, mc, m):
        m_new = jnp.maximum(m, mc)
        p_ref[...] = jnp.exp2(s_ref[...] - m_new).astype(BF16)
        return m_new, jnp.exp2(m - m_new)

    def accumulate(vt_blocks, p_ref, alpha):
        pv = None
        for u, vtblk in enumerate(vt_blocks):
            part = jnp.dot(vtblk, p_ref[u * vtblk.shape[1]:(u + 1) * vtblk.shape[1], :],
                           preferred_element_type=F32)
            pv = part if pv is None else pv + part
        acc_ref[...] = alpha * acc_ref[...] + pv

    def v_chunk(j):
        return [vt_ref[0, 0, j * sub + u] for u in range(sub)]

    acc_ref[...] = jnp.zeros(acc_ref.shape, F32)
    m = jnp.full((1, GROUP * tq), NEG_BIG, F32)
    mc_ctx = scores(kc_ref[0, 0], sc_ref)
    mc0 = scores(k_chunk(0), s0_ref)
    m, alpha_ctx = softmax(sc_ref, pc_ref, mc_ctx, m)
    mc1 = scores(k_chunk(1), s1_ref)
    m, alpha0 = softmax(s0_ref, p0_ref, mc0, m)
    accumulate([vct_ref[0, 0, 0]], pc_ref, alpha_ctx)

    s_bufs = (s0_ref, s1_ref)
    p_bufs = (p0_ref, p1_ref)

    def stage(c, parity, carry, has_next):
        m, alpha_prev, mc = carry
        mc_next = scores(k_chunk(c + 1), s_bufs[1 - parity]) if has_next else None
        m, alpha = softmax(s_bufs[parity], p_bufs[parity], mc, m)
        accumulate(v_chunk(c - 1), p_bufs[1 - parity], alpha_prev)
        return m, alpha, mc_next

    unroll = ATT_UNROLL
    n_groups = (n_chunks - 2) // unroll

    def group(g, carry):
        for u in range(unroll):
            carry = stage(1 + g * unroll + u, (1 + u) % 2, carry, True)
        return carry

    carry = lax.fori_loop(0, n_groups, group, (m, alpha0, mc1))
    for c in range(1 + n_groups * unroll, n_chunks):
        carry = stage(c, c % 2, carry, c + 1 < n_chunks)
    accumulate(v_chunk(n_chunks - 1), p_bufs[(n_chunks - 1) % 2], carry[1])

    acc = acc_ref[...]
    o = (acc[:HEAD_DIM] / acc[HEAD_DIM:HEAD_DIM + 1]).T
    for g in range(GROUP):
        o_ref[0, :, g * HEAD_DIM:(g + 1) * HEAD_DIM] = o[g * tq:(g + 1) * tq].astype(o_ref.dtype)


def _attn_a(q, kc, vct, k, vt, tq, tk):
    b, _, _, l = q.shape
    c_len = kc.shape[2]
    n_stored, ts = vt.shape[2], vt.shape[4]
    assert tk % ts == 0 and l % tk == 0
    return pl.pallas_call(
        _attn_a_kernel,
        grid=(b, A_KV_HEADS, l // tq),
        in_specs=[
            pl.BlockSpec((1, GROUP, HEAD_DIM, tq), lambda bi, kv, i: (bi, kv, 0, i)),
            pl.BlockSpec((1, 1, c_len, HEAD_DIM), lambda bi, kv, i: (bi, kv, 0, 0)),
            pl.BlockSpec((1, 1, 1, V_ROWS, c_len), lambda bi, kv, i: (bi, kv, 0, 0, 0)),
            pl.BlockSpec((1, 1, l, HEAD_DIM), lambda bi, kv, i: (bi, kv, 0, 0)),
            pl.BlockSpec((1, 1, n_stored, V_ROWS, ts), lambda bi, kv, i: (bi, kv, 0, 0, 0)),
        ],
        out_specs=pl.BlockSpec((1, tq, GROUP * HEAD_DIM), lambda bi, kv, i: (bi, i, kv)),
        out_shape=jax.ShapeDtypeStruct((b, l, A_HEADS * HEAD_DIM), BF16),
        scratch_shapes=[
            pltpu.VMEM((c_len, GROUP * tq), F32),
            pltpu.VMEM((c_len, GROUP * tq), BF16),
            pltpu.VMEM((tk, GROUP * tq), F32),
            pltpu.VMEM((tk, GROUP * tq), F32),
            pltpu.VMEM((tk, GROUP * tq), BF16),
            pltpu.VMEM((tk, GROUP * tq), BF16),
            pltpu.VMEM((V_ROWS, GROUP * tq), F32),
        ],
        compiler_params=_cparams(("parallel", "parallel", "arbitrary")),
        name="attn_global",
    )(q, kc, vct, k, vt)


def _attn_b_kernel(sink_ref, q_ref, kp_ref, km_ref, kn_ref, vp_ref, vm_ref, vn_ref,
                   kc_ref, vc_ref, o_ref, *, seq_len):
    kv = pl.program_id(1)
    i = pl.program_id(2)
    tq = q_ref.shape[2]
    sub = tq // BLOCK
    band = 3 * BLOCK
    kcat = jnp.concatenate([kp_ref[0, 0], km_ref[0, 0], kn_ref[0, 0]], axis=0)
    vcat = jnp.concatenate([vp_ref[0, 0], vm_ref[0, 0], vn_ref[0, 0]], axis=0)
    kc = kc_ref[0, 0]
    vc = vc_ref[0, 0]
    rows = GROUP * BLOCK
    nt = (((1,), (1,)), ((), ()))

    r = lax.broadcasted_iota(jnp.int32, (rows, band), 0) % BLOCK
    c = lax.broadcasted_iota(jnp.int32, (rows, band), 1)
    in_window = jnp.abs(c - BLOCK - r) <= WINDOW
    head_row = lax.broadcasted_iota(jnp.int32, (rows, 1), 0) // BLOCK
    sink2 = jnp.zeros((rows, 1), F32)
    for g in range(GROUP):
        sink2 = jnp.where(head_row == g, sink_ref[kv * GROUP + g] * LOG2E, sink2)

    def scores(j):
        q2 = jnp.concatenate([q_ref[0, g, j * BLOCK:(j + 1) * BLOCK, :] for g in range(GROUP)], axis=0)
        return (lax.dot_general(q2, kcat[j * BLOCK:j * BLOCK + band], nt, preferred_element_type=F32),
                lax.dot_general(q2, kc, nt, preferred_element_type=F32))

    def softmax(j, s_loc, s_ctx):
        kpos = (i * sub + (j - 1)) * BLOCK + c
        valid = in_window & (kpos >= 0) & (kpos < seq_len)
        s_loc = jnp.where(valid, s_loc, NEG_BIG)
        m = jnp.maximum(jnp.maximum(jnp.max(s_loc, axis=-1, keepdims=True),
                                    jnp.max(s_ctx, axis=-1, keepdims=True)), sink2)
        p_loc = jnp.exp2(s_loc - m)
        p_ctx = jnp.exp2(s_ctx - m)
        denom = (jnp.sum(p_loc, axis=-1, keepdims=True) + jnp.sum(p_ctx, axis=-1, keepdims=True)
                 + jnp.exp2(sink2 - m))
        return p_loc.astype(BF16), p_ctx.astype(BF16), denom

    def values(j, p_loc, p_ctx, denom):
        o = (jnp.dot(p_ctx, vc, preferred_element_type=F32)
             + jnp.dot(p_loc, vcat[j * BLOCK:j * BLOCK + band], preferred_element_type=F32)) / denom
        for g in range(GROUP):
            o_ref[0, j * BLOCK:(j + 1) * BLOCK, g * HEAD_DIM:(g + 1) * HEAD_DIM] = (
                o[g * BLOCK:(g + 1) * BLOCK].astype(o_ref.dtype))

    s_prev = None
    p_prev = None
    for j in range(sub + 2):
        s_cur = scores(j) if j < sub else None
        p_cur = softmax(j - 1, *s_prev) if 1 <= j <= sub else None
        if j >= 2:
            values(j - 2, *p_prev)
        s_prev, p_prev = s_cur, p_cur


def _attn_b(sink, q, k, v, kc, vc, tq):
    b, _, l, _ = q.shape
    c_len = kc.shape[2]
    sub = tq // BLOCK
    nb = l // BLOCK

    def main(bi, kv, i):
        return (bi, kv, i, 0)

    def prev(bi, kv, i):
        return (bi, kv, jnp.maximum(i * sub - 1, 0), 0)

    def nxt(bi, kv, i):
        return (bi, kv, jnp.minimum((i + 1) * sub, nb - 1), 0)

    def ctx_map(bi, kv, i):
        return (bi, kv, 0, 0)

    edge = lambda fn: pl.BlockSpec((1, 1, BLOCK, HEAD_DIM), fn)
    return pl.pallas_call(
        functools.partial(_attn_b_kernel, seq_len=l),
        grid=(b, B_KV_HEADS, l // tq),
        in_specs=[
            pl.BlockSpec(memory_space=pltpu.SMEM),
            pl.BlockSpec((1, GROUP, tq, HEAD_DIM), main),
            edge(prev), pl.BlockSpec((1, 1, tq, HEAD_DIM), main), edge(nxt),
            edge(prev), pl.BlockSpec((1, 1, tq, HEAD_DIM), main), edge(nxt),
            pl.BlockSpec((1, 1, c_len, HEAD_DIM), ctx_map),
            pl.BlockSpec((1, 1, c_len, HEAD_DIM), ctx_map),
        ],
        out_specs=pl.BlockSpec((1, tq, GROUP * HEAD_DIM), lambda bi, kv, i: (bi, i, kv)),
        out_shape=jax.ShapeDtypeStruct((b, l, B_HEADS * HEAD_DIM), BF16),
        compiler_params=_cparams(("parallel", "parallel", "arbitrary")),
        name="attn_window",
    )(sink, q, k, k, k, v, v, v, kc, vc)


def _residual_and_next(y, x, mod_ref, ng_ref):
    x1 = x + mod_ref[0, 2:3, :] * _rms(y, ng_ref[1:2, :])
    h2 = _rms(x1, ng_ref[2:3, :]) * (1.0 + mod_ref[0, 4:5, :]) + mod_ref[0, 3:4, :]
    return x1, h2


def _post_attn_kernel(oa_ref, ob_ref, w_ref, x_ref, mod_ref, ng_ref, x1_ref, h2_ref):
    na = oa_ref.shape[2]
    rows = oa_ref.shape[1] // POST_SUB
    ys = []
    for r in range(POST_SUB):
        sl = slice(r * rows, (r + 1) * rows)
        ys.append(jnp.dot(oa_ref[0, sl, :], w_ref[0:na, :], preferred_element_type=F32)
                  + jnp.dot(ob_ref[0, sl, :], w_ref[na:, :], preferred_element_type=F32))
    for r, y in enumerate(ys):
        sl = slice(r * rows, (r + 1) * rows)
        x1, h2 = _residual_and_next(y, x_ref[0, sl, :], mod_ref, ng_ref)
        x1_ref[0, sl, :] = x1
        h2_ref[0, sl, :] = h2.astype(BF16)


def _post_attn(oa, ob, w_out, x, mod6, ng, tm):
    b, l, d = x.shape
    na, nb_ = oa.shape[2], ob.shape[2]
    return pl.pallas_call(
        _post_attn_kernel,
        grid=(b, l // tm),
        in_specs=[
            pl.BlockSpec((1, tm, na), lambda bi, i: (bi, i, 0)),
            pl.BlockSpec((1, tm, nb_), lambda bi, i: (bi, i, 0)),
            pl.BlockSpec((na + nb_, d), lambda bi, i: (0, 0)),
            pl.BlockSpec((1, tm, d), lambda bi, i: (bi, i, 0)),
            pl.BlockSpec((1, N_MOD, d), lambda bi, i: (bi, 0, 0)),
            pl.BlockSpec((4, d), lambda bi, i: (0, 0)),
        ],
        out_specs=[
            pl.BlockSpec((1, tm, d), lambda bi, i: (bi, i, 0)),
            pl.BlockSpec((1, tm, d), lambda bi, i: (bi, i, 0)),
        ],
        out_shape=[jax.ShapeDtypeStruct((b, l, d), F32), jax.ShapeDtypeStruct((b, l, d), BF16)],
        compiler_params=_cparams(("parallel", "parallel")),
        name="attn_out_proj",
    )(oa, ob, w_out, x, mod6, ng)


def _ffn_kernel(*refs, fused_input, emit_next):
    refs = list(refs)
    if fused_input:
        xp_ref, xm_ref, xn_ref, rp_ref, rm_ref, rn_ref = refs[:6]
        refs = refs[6:]
    else:
        hp_ref, hm_ref, hn_ref, xm_ref = refs[:4]
        refs = refs[4:]
    mod_ref, ng_ref, wu_ref, cw_ref, cb_ref, wd_ref = refs[:6]
    refs = refs[6:]
    if emit_next:
        modn_ref, ngn_ref = refs[:2]
        refs = refs[2:]
    o_ref = refs[0]
    refs = refs[1:]
    if emit_next:
        hnext_ref = refs[0]
        refs = refs[1:]
    hcat_ref, ug0_ref, uv0_ref, ug1_ref, uv1_ref, act_ref = refs

    i = pl.program_id(1)
    last = pl.num_programs(1) - 1
    tm = xm_ref.shape[1]
    f = wd_ref.shape[0]
    fc = ug0_ref.shape[1]
    n_chunks = f // fc
    u_bufs = ((ug0_ref, uv0_ref), (ug1_ref, uv1_ref))

    def pre_norm(x):
        return (_rms(x, ng_ref[2:3, :]) * (1.0 + mod_ref[0, 4:5, :]) + mod_ref[0, 3:4, :]).astype(BF16)

    if fused_input:
        x_in = xm_ref[0] + rm_ref[0].astype(F32)
        hp = pre_norm(xp_ref[0] + rp_ref[0].astype(F32))
        hm = pre_norm(x_in)
        hn = pre_norm(xn_ref[0] + rn_ref[0].astype(F32))
    else:
        x_in = xm_ref[0]
        hp, hm, hn = hp_ref[0], hm_ref[0], hn_ref[0]
    hcat_ref[0:HALO, :] = jnp.where(i > 0, hp, jnp.zeros_like(hp))
    hcat_ref[HALO:HALO + tm, :] = hm
    hcat_ref[HALO + tm:, :] = jnp.where(i < last, hn, jnp.zeros_like(hn))

    def cols(c, half):
        return pl.ds(pl.multiple_of(half * f + c * fc, fc), fc)

    def conv(u_ref, c, half):
        cw = cw_ref[:, cols(c, half)]
        u = u_ref[...]
        rows = u.shape[0]
        prev = pltpu.roll(u, 1, 0)[HALO:HALO + tm]
        nxt = pltpu.roll(u, rows - 1, 0)[HALO:HALO + tm]
        return (prev * cw[0:1, :] + u[HALO:HALO + tm] * cw[1:2, :] + nxt * cw[2:3, :]
                + cb_ref[:, cols(c, half)])

    def up(c, bufs):
        hcat = hcat_ref[...]
        for half in range(2):
            bufs[half][...] = jnp.dot(hcat, wu_ref[:, cols(c, half)], preferred_element_type=F32)

    def mid(c, bufs):
        gate = conv(bufs[0], c, 0)
        val = conv(bufs[1], c, 1)
        act_ref[:, cols(c, 0)] = (gate / (1.0 + jnp.exp(-gate)) * val).astype(BF16)

    def stage(c, parity, do_up):
        if do_up:
            up(c + 1, u_bufs[1 - parity])
        mid(c, u_bufs[parity])

    up(0, u_bufs[0])
    unroll = FFN_UNROLL
    n_groups = (n_chunks - 1) // unroll

    def group(g, carry):
        for u in range(unroll):
            stage(g * unroll + u, u % 2, True)
        return carry

    lax.fori_loop(0, n_groups, group, 0)
    for c in range(n_groups * unroll, n_chunks):
        stage(c, c % 2, c + 1 < n_chunks)
    y = jnp.dot(act_ref[...], wd_ref[...], preferred_element_type=F32)
    out = x_in + mod_ref[0, 5:6, :] * _rms(y, ng_ref[3:4, :])
    o_ref[0] = out
    if emit_next:
        hnext_ref[0] = (_rms(out, ngn_ref[0:1, :]) * (1.0 + modn_ref[0, 1:2, :]) + modn_ref[0, 0:1, :]).astype(BF16)


def _ffn(x, mod6, ng, w_up, conv_w, conv_b, w_down, tm, fc, *, h2=None, branch=None, next_mod=None, next_ng=None):
    b, l, d = x.shape
    f = w_down.shape[0]
    assert f % fc == 0 and (h2 is None) != (branch is None)
    per = tm // HALO
    n_halo = l // HALO
    resident = lambda shape: pl.BlockSpec(shape, lambda bi, i: (0,) * len(shape),
                                          pipeline_mode=pl.Buffered(1))
    prev_halo = pl.BlockSpec((1, HALO, d), lambda bi, i: (bi, jnp.maximum(i * per - 1, 0), 0))
    next_halo = pl.BlockSpec((1, HALO, d), lambda bi, i: (bi, jnp.minimum((i + 1) * per, n_halo - 1), 0))
    tile = pl.BlockSpec((1, tm, d), lambda bi, i: (bi, i, 0))
    mod_spec = pl.BlockSpec((1, N_MOD, d), lambda bi, i: (bi, 0, 0))
    ng_spec = pl.BlockSpec((4, d), lambda bi, i: (0, 0))
    if branch is None:
        operands, in_specs = [h2, h2, h2, x], [prev_halo, tile, next_halo, tile]
    else:
        operands, in_specs = [x, x, x, branch, branch, branch], [prev_halo, tile, next_halo] * 2
    operands += [mod6, ng, w_up.astype(BF16), conv_w, conv_b.reshape(1, 2 * f), w_down.astype(BF16)]
    in_specs += [mod_spec, ng_spec, resident((d, 2 * f)), resident((3, 2 * f)), resident((1, 2 * f)), resident((f, d))]
    out_specs, out_shape = [tile], [jax.ShapeDtypeStruct((b, l, d), F32)]
    if next_mod is not None:
        operands += [next_mod, next_ng]
        in_specs += [mod_spec, ng_spec]
        out_specs.append(tile)
        out_shape.append(jax.ShapeDtypeStruct((b, l, d), BF16))
    outs = pl.pallas_call(
        functools.partial(_ffn_kernel, fused_input=branch is not None, emit_next=next_mod is not None),
        grid=(b, l // tm),
        in_specs=in_specs,
        out_specs=out_specs,
        out_shape=out_shape,
        scratch_shapes=[
            pltpu.VMEM((tm + 2 * HALO, d), BF16),
            pltpu.VMEM((tm + 2 * HALO, fc), F32),
            pltpu.VMEM((tm + 2 * HALO, fc), F32),
            pltpu.VMEM((tm + 2 * HALO, fc), F32),
            pltpu.VMEM((tm + 2 * HALO, fc), F32),
            pltpu.VMEM((tm, f), BF16),
        ],
        compiler_params=_cparams(("parallel", "arbitrary")),
        name="conv_ffn",
    )(*operands)
    return outs if next_mod is not None else outs[0]


def _dft_tables(l, d):
    n1, n2, cg = DFT_N1, l // DFT_N1, d // FNET_GROUPS
    a = np.arange(cg)
    ang_c = 2.0 * np.pi * np.outer(a, a) / cg
    csc = np.concatenate([np.cos(ang_c), np.sin(ang_c)], axis=1) / math.sqrt(cg)
    a1 = np.arange(n1)
    ang1 = 2.0 * np.pi * np.outer(a1, a1) / n1
    c1, s1 = np.cos(ang1), np.sin(ang1)
    m1 = np.block([[c1, -s1], [s1, c1]])
    a2 = np.arange(n2)
    ang2 = 2.0 * np.pi * np.outer(a2, a2) / n2
    c2 = np.cos(ang2) / math.sqrt(l)
    s2 = np.sin(ang2) / math.sqrt(l)
    ang_t = 2.0 * np.pi * np.outer(a1, a2) / l
    f32 = lambda t: jnp.asarray(t, F32)
    return (f32(csc).astype(BF16), f32(m1).astype(BF16), f32(c2).astype(BF16), f32(s2).astype(BF16),
            f32(np.cos(ang_t)), f32(np.sin(ang_t)))


def _fourier1_kernel(h_ref, csc_ref, m1_ref, twc_ref, tws_ref, zr_ref, zi_ref, *, d):
    cg = csc_ref.shape[0]
    n1 = h_ref.shape[1]
    t2s = h_ref.shape[2] // d
    csc = csc_ref[...]
    m1 = m1_ref[...]
    stacked = []
    for t in range(t2s):
        h = h_ref[0, :, t * d:(t + 1) * d]
        ab = [jnp.dot(h[:, g * cg:(g + 1) * cg], csc, preferred_element_type=F32)
              for g in range(d // cg)]
        a = jnp.concatenate([t_[:, :cg] for t_ in ab], axis=1)
        bm = jnp.concatenate([t_[:, cg:] for t_ in ab], axis=1)
        stacked.append(jnp.concatenate([a, bm], axis=0).astype(BF16))
    ys = [jnp.dot(m1, ab2, preferred_element_type=F32) for ab2 in stacked]
    for t, y in enumerate(ys):
        yr = y[:n1]
        yin = y[n1:]
        cc = jnp.tile(twc_ref[:, t * HEAD_DIM:(t + 1) * HEAD_DIM], (1, d // HEAD_DIM))
        ss = jnp.tile(tws_ref[:, t * HEAD_DIM:(t + 1) * HEAD_DIM], (1, d // HEAD_DIM))
        zr_ref[0, :, t * d:(t + 1) * d] = (yr * cc - yin * ss).astype(BF16)
        zi_ref[0, :, t * d:(t + 1) * d] = (-(yin * cc) - yr * ss).astype(BF16)


def _fourier1(h, csc, m1, twc, tws, t2s):
    b, l, d = h.shape
    n1, n2 = DFT_N1, l // DFT_N1
    hv = h.reshape(b, n1, n2 * d)
    lanes = HEAD_DIM
    twc_x = jnp.repeat(twc, lanes, axis=1)
    tws_x = jnp.repeat(tws, lanes, axis=1)
    cg = csc.shape[0]
    return pl.pallas_call(
        functools.partial(_fourier1_kernel, d=d),
        grid=(b, n2 // t2s),
        in_specs=[
            pl.BlockSpec((1, n1, t2s * d), lambda bi, j: (bi, 0, j)),
            pl.BlockSpec((cg, 2 * cg), lambda bi, j: (0, 0)),
            pl.BlockSpec((2 * n1, 2 * n1), lambda bi, j: (0, 0)),
            pl.BlockSpec((n1, t2s * lanes), lambda bi, j: (0, j)),
            pl.BlockSpec((n1, t2s * lanes), lambda bi, j: (0, j)),
        ],
        out_specs=[
            pl.BlockSpec((1, n1, t2s * d), lambda bi, j: (bi, 0, j)),
            pl.BlockSpec((1, n1, t2s * d), lambda bi, j: (bi, 0, j)),
        ],
        out_shape=[jax.ShapeDtypeStruct((b, n1, n2 * d), BF16)] * 2,
        compiler_params=_cparams(("parallel", "parallel")),
        name="fnet_stage1",
    )(hv, csc, m1, twc_x, tws_x)


def _fourier2_kernel(zr_ref, zi_ref, c2_ref, s2_ref, w_ref, mod_ref, ng_ref, r_ref):
    k1s = zr_ref.shape[1]
    n2 = zr_ref.shape[2]
    d = zr_ref.shape[3]
    c2 = c2_ref[...]
    s2 = s2_ref[...]
    fs = []
    for k in range(k1s):
        f = (jnp.dot(c2, zr_ref[0, k], preferred_element_type=F32)
             + jnp.dot(s2, zi_ref[0, k], preferred_element_type=F32))
        fs.append(f.astype(BF16))
    per = k1s // F2_GROUPS
    ys = [jnp.dot(jnp.concatenate(fs[g * per:(g + 1) * per], axis=0), w_ref[...], preferred_element_type=F32)
          for g in range(F2_GROUPS)]
    for k in range(k1s):
        y = ys[k // per][(k % per) * n2:(k % per + 1) * n2]
        r_ref[0, :, k * d:(k + 1) * d] = (mod_ref[0, 2:3, :] * _rms(y, ng_ref[1:2, :])).astype(BF16)


def _fourier2(zr, zi, c2, s2, w_f, mod6, ng, k1s):
    b, n1, nd = zr.shape
    n2 = c2.shape[0]
    d = nd // n2
    zr4 = zr.reshape(b, n1, n2, d)
    zi4 = zi.reshape(b, n1, n2, d)
    r = pl.pallas_call(
        _fourier2_kernel,
        grid=(b, n1 // k1s),
        in_specs=[
            pl.BlockSpec((1, k1s, n2, d), lambda bi, j: (bi, j, 0, 0)),
            pl.BlockSpec((1, k1s, n2, d), lambda bi, j: (bi, j, 0, 0)),
            pl.BlockSpec((n2, n2), lambda bi, j: (0, 0)),
            pl.BlockSpec((n2, n2), lambda bi, j: (0, 0)),
            pl.BlockSpec((d, d), lambda bi, j: (0, 0)),
            pl.BlockSpec((1, N_MOD, d), lambda bi, j: (bi, 0, 0)),
            pl.BlockSpec((4, d), lambda bi, j: (0, 0)),
        ],
        out_specs=pl.BlockSpec((1, n2, k1s * d), lambda bi, j: (bi, 0, j)),
        out_shape=jax.ShapeDtypeStruct((b, n2, n1 * d), BF16),
        compiler_params=_cparams(("parallel", "parallel")),
        name="fnet_stage2",
    )(zr4, zi4, c2, s2, w_f, mod6, ng)
    return r.reshape(b, n2 * n1, d)


def _rope_permutation():
    quarter = HEAD_DIM // 4
    j = np.arange(HEAD_DIM)
    partner = np.where((j % (2 * quarter)) < quarter, j + quarter, j - quarter)
    p = np.zeros((HEAD_DIM, HEAD_DIM), np.float32)
    p[partner, j] = 1.0
    return jnp.asarray(p, F32).astype(BF16)


def _rope_tables(l):
    n_freq = HEAD_DIM // 4
    inv = ROPE_THETA ** (-jnp.arange(n_freq, dtype=F32) / n_freq)
    ang_row = jnp.arange(l // GRID_W).astype(F32)[:, None] * inv
    ang_col = jnp.arange(GRID_W).astype(F32)[:, None] * inv
    zr = jnp.zeros_like(ang_row)
    zc = jnp.zeros_like(ang_col)
    cos_r = jnp.concatenate([jnp.cos(ang_row), jnp.cos(ang_row), zr, zr], axis=1)
    sin_r = jnp.concatenate([-jnp.sin(ang_row), jnp.sin(ang_row), zr, zr], axis=1)
    cos_c = jnp.concatenate([zc, zc, jnp.cos(ang_col), jnp.cos(ang_col)], axis=1)
    sin_c = jnp.concatenate([zc, zc, -jnp.sin(ang_col), jnp.sin(ang_col)], axis=1)
    return cos_r, sin_r, cos_c, sin_c


def kernel(x, c, ctx, c_ctx, mod_w, mod_b, norm_g, attn_w_in, attn_w_out, q_norm_g, k_norm_g, sink,
           fourier_w_out, ffn_w_up, ffn_conv_w, ffn_conv_b, ffn_w_down):
    b, l, d = x.shape
    c_len = ctx.shape[1]
    assert mod_w.shape[0] == 2 and b + 1 <= COND_ROWS
    assert l % max(QKV_TM, WIN_TQ, FFN_TM, POST_TM, DFT_N1 * F1_T2) == 0

    cond = jnp.zeros((COND_ROWS, d), F32).at[:b].set(c).at[b].set(c_ctx)
    mods = _mods(cond, mod_w, mod_b)
    mod_l0 = mods[0].reshape(COND_ROWS, N_MOD, d)
    mod_l1 = mods[1].reshape(COND_ROWS, N_MOD, d)

    rope = _rope_tables(l)
    w_in = attn_w_in[0].astype(BF16)
    qg = q_norm_g[0].reshape(1, HEAD_DIM)
    kg = k_norm_g[0].reshape(1, HEAD_DIM)
    qa, ka, vat, qb, kb, vb = _qkv(x, mod_l0, lambda bi: bi, norm_g[0], w_in, qg, kg, rope, QKV_TM)
    zeros_r = jnp.zeros((8, HEAD_DIM), F32)
    zeros_c = jnp.zeros((GRID_W, HEAD_DIM), F32)
    no_rope = (jnp.ones_like(zeros_r), zeros_r, zeros_c, zeros_c)
    _, kac, vact, _, kbc, vbc = _qkv(ctx, mod_l0, lambda bi: b, norm_g[0], w_in, qg, kg, no_rope, c_len)
    oa = _attn_a(qa, kac, vact, ka, vat, ATT_TQ, ATT_TK)
    ob = _attn_b(sink[0], qb, kb, vb, kbc, vbc, WIN_TQ)
    x1, h2 = _post_attn(oa, ob, attn_w_out[0].astype(BF16), x, mod_l0, norm_g[0], POST_TM)
    x2, h_next = _ffn(x1, mod_l0, norm_g[0], ffn_w_up[0], ffn_conv_w[0], ffn_conv_b[0], ffn_w_down[0],
                      FFN_TM, FFN_FC, h2=h2, next_mod=mod_l1, next_ng=norm_g[1])

    csc, m1, c2, s2, twc, tws = _dft_tables(l, d)
    zr, zi = _fourier1(h_next, csc, m1, twc, tws, F1_T2)
    branch = _fourier2(zr, zi, c2, s2, fourier_w_out[0].astype(BF16), mod_l1, norm_g[1], F2_K1)
    return _ffn(x2, mod_l1, norm_g[1], ffn_w_up[1], ffn_conv_w[1], ffn_conv_b[1], ffn_w_down[1],
                FFN_TM, FFN_FC, branch=branch)
```
